```python
import math
import jax, jax.numpy as jnp
from jax import lax
import numpy as np

D_MODEL = 1024
BATCH = 2
SEQ = 16384
DEPTH = 2

EPS = 1e-6
D_MIX = D_MODEL
CONV_WIDTH = 4
MLA_HEADS = 4
MLA_Q_RANK = 256
MLA_KV_RANK = 128
MLA_NOPE = 64
MLA_ROPE = 32
MLA_V = 64
ROPE_THETA = 10000.0
Q_BLOCK = 128
MLA_IN = MLA_Q_RANK + MLA_KV_RANK + MLA_ROPE
MLA_OUT = MLA_HEADS * MLA_V
SSD_HEADS = 8
SSD_HEAD_DIM = 64
SSD_GROUPS = 2
SSD_STATE = 64
SSD_CHUNK = 128
SSD_D_INNER = SSD_HEADS * SSD_HEAD_DIM
SSD_CONV_DIM = SSD_D_INNER + 2 * SSD_GROUPS * SSD_STATE
SSD_IN = SSD_D_INNER + SSD_CONV_DIM + SSD_HEADS
GDN_HEADS = 4
GDN_DK = 64
GDN_DV = 64
GDN_CHUNK = 64
GDN_QKV = GDN_HEADS * (2 * GDN_DK + GDN_DV)
GDN_IN = GDN_QKV + GDN_HEADS * GDN_DV + 2 * GDN_HEADS
GDN_OUT = GDN_HEADS * GDN_DV
D_IN = MLA_IN + SSD_IN + GDN_IN
D_FF_DENSE = 2816
N_EXPERTS = 8
TOP_K = 2
D_FF_EXPERT = 3584
MOE_BLOCK = 512
N_DENSE = (DEPTH + 1) // 2
N_MOE = DEPTH // 2

kernel_name = "hybrid_mla_ssd_gdn_moe_block"


def rms_norm(x):
    xf = x.astype(jnp.float32)
    return (xf * lax.rsqrt(jnp.mean(xf * xf, -1, keepdims=True) + EPS)).astype(x.dtype)


def l2_norm(x):
    xf = x.astype(jnp.float32)
    return xf * lax.rsqrt(jnp.sum(xf * xf, -1, keepdims=True) + EPS)


def causal_dwconv(x, w, b):
    y = lax.conv_general_dilated(x, w[:, None, :].astype(x.dtype), window_strides=(1,),
                                 padding=[(CONV_WIDTH - 1, 0)],
                                 dimension_numbers=('NWC', 'WIO', 'NWC'),
                                 feature_group_count=x.shape[-1])
    return y + b


def apply_rope(t, cos, sin):
    t1, t2 = jnp.split(t, 2, axis=-1)
    return jnp.concatenate([t1 * cos - t2 * sin, t2 * cos + t1 * sin], -1).astype(t.dtype)


def segsum(a):
    T = a.shape[-1]
    xs = jnp.where(jnp.tril(jnp.ones((T, T), bool), -1), a[..., :, None], 0.0)
    xs = jnp.cumsum(xs, axis=-2)
    return jnp.where(jnp.tril(jnp.ones((T, T), bool)), xs, -jnp.inf)


def mla_mixer(u, q_norm, w_uq, kv_norm, w_ukv):
    Bsz, S, _ = u.shape
    H = MLA_HEADS
    c_q, c_kv, k_rope = jnp.split(u, [MLA_Q_RANK, MLA_Q_RANK + MLA_KV_RANK], axis=-1)
    q = ((rms_norm(c_q) * q_norm) @ w_uq).reshape(Bsz, S, H, MLA_NOPE + MLA_ROPE)
    q_nope, q_rope = q[..., :MLA_NOPE], q[..., MLA_NOPE:]
    kv = ((rms_norm(c_kv) * kv_norm) @ w_ukv).reshape(Bsz, S, H, MLA_NOPE + MLA_V)
    k_nope, v = kv[..., :MLA_NOPE], kv[..., MLA_NOPE:]
    pos = jnp.arange(S, dtype=jnp.float32)
    inv_freq = ROPE_THETA ** (-jnp.arange(0, MLA_ROPE, 2, dtype=jnp.float32) / MLA_ROPE)
    ang = pos[:, None] * inv_freq[None, :]
    cos, sin = jnp.cos(ang), jnp.sin(ang)
    q_rope = apply_rope(q_rope, cos[None, :, None, :], sin[None, :, None, :])
    k_rope = apply_rope(k_rope, cos[None], sin[None])
    scale = (MLA_NOPE + MLA_ROPE) ** -0.5
    nb = S // Q_BLOCK
    qn_b = jnp.moveaxis(q_nope.reshape(Bsz, nb, Q_BLOCK, H, MLA_NOPE), 1, 0)
    qr_b = jnp.moveaxis(q_rope.reshape(Bsz, nb, Q_BLOCK, H, MLA_ROPE), 1, 0)
    kpos = jnp.arange(S)

    def block(args):
        qn, qr, i = args
        s = (jnp.einsum('bqhd,bkhd->bhqk', qn, k_nope)
             + jnp.einsum('bqhr,bkr->bhqk', qr, k_rope)).astype(jnp.float32) * scale
        qpos = i * Q_BLOCK + jnp.arange(Q_BLOCK)
        s = jnp.where(kpos[None, :] <= qpos[:, None], s, -jnp.inf)
        p = jax.nn.softmax(s, axis=-1).astype(v.dtype)
        return jnp.einsum('bhqk,bkhd->bqhd', p, v)

    o = lax.map(block, (qn_b, qr_b, jnp.arange(nb)))
    return jnp.moveaxis(o, 0, 1).reshape(Bsz, S, MLA_OUT).astype(u.dtype)


def ssd_mixer(u, conv_w, conv_b, dt_bias, a_log, d_skip, norm_w):
    Bsz, S, _ = u.shape
    H, P, G, N, L = SSD_HEADS, SSD_HEAD_DIM, SSD_GROUPS, SSD_STATE, SSD_CHUNK
    nc = S // L
    z, xbc, dt = jnp.split(u, [SSD_D_INNER, SSD_D_INNER + SSD_CONV_DIM], axis=-1)
    xbc = jax.nn.silu(causal_dwconv(xbc, conv_w, conv_b)).astype(jnp.float32)
    x, Bm, Cm = jnp.split(xbc, [SSD_D_INNER, SSD_D_INNER + G * N], axis=-1)
    x = x.reshape(Bsz, S, H, P)
    Bm = jnp.repeat(Bm.reshape(Bsz, S, G, N), H // G, axis=2)
    Cm = jnp.repeat(Cm.reshape(Bsz, S, G, N), H // G, axis=2)
    dt = jax.nn.softplus(dt.astype(jnp.float32) + dt_bias.astype(jnp.float32))
    A = -jnp.exp(a_log.astype(jnp.float32))
    xc = (x * dt[..., None]).reshape(Bsz, nc, L, H, P)
    Bc = Bm.reshape(Bsz, nc, L, H, N)
    Cc = Cm.reshape(Bsz, nc, L, H, N)
    a = (dt * A).reshape(Bsz, nc, L, H).transpose(0, 3, 1, 2)
    a_cum = jnp.cumsum(a, axis=-1)
    scores = jnp.einsum('bclhn,bcshn->bhcls', Cc, Bc) * jnp.exp(segsum(a))
    y_diag = jnp.einsum('bhcls,bcshp->bclhp', scores, xc)
    decay_states = jnp.exp(a_cum[..., -1:] - a_cum)
    states = jnp.einsum('bclhn,bhcl,bclhp->bchpn', Bc, decay_states, xc)
    chunk_decay = jnp.exp(a_cum[..., -1])

    def step(hstate, inp):
        st, dec = inp
        return hstate * dec[..., None, None] + st, hstate

    _, prev = lax.scan(step, jnp.zeros((Bsz, H, P, N), jnp.float32),
                       (jnp.moveaxis(states, 1, 0), jnp.moveaxis(chunk_decay, 2, 0)))
    prev = jnp.moveaxis(prev, 0, 1)
    y_off = jnp.einsum('bclhn,bchpn,bhcl->bclhp', Cc, prev, jnp.exp(a_cum))
    y = (y_diag + y_off).reshape(Bsz, S, H, P) + x * d_skip.astype(jnp.float32)[:, None]
    y = y.reshape(Bsz, S, SSD_D_INNER) * jax.nn.silu(z.astype(jnp.float32))
    y = rms_norm(y.reshape(Bsz, S, G, SSD_D_INNER // G)).reshape(Bsz, S, SSD_D_INNER)
    return (y * norm_w).astype(u.dtype)


def gdn_mixer(u, conv_w, conv_b, dt_bias, a_log, norm_w):
    Bsz, S, _ = u.shape
    H, K, V, L = GDN_HEADS, GDN_DK, GDN_DV, GDN_CHUNK
    nc = S // L
    qkv, z, b, a = jnp.split(u, [GDN_QKV, GDN_QKV + H * V, GDN_QKV + H * V + H], axis=-1)
    qkv = jax.nn.silu(causal_dwconv(qkv, conv_w, conv_b))
    q, k, v = jnp.split(qkv, [H * K, 2 * H * K], axis=-1)
    q = l2_norm(q.reshape(Bsz, S, H, K)) * (K ** -0.5)
    k = l2_norm(k.reshape(Bsz, S, H, K))
    v = v.reshape(Bsz, S, H, V).astype(jnp.float32)
    beta = jax.nn.sigmoid(b.astype(jnp.float32))
    g = -jnp.exp(a_log.astype(jnp.float32)) * jax.nn.softplus(
        a.astype(jnp.float32) + dt_bias.astype(jnp.float32))

    def chunks(t):
        return t.reshape(Bsz, nc, L, H, -1).transpose(0, 3, 1, 2, 4)

    qc, kc, vc = chunks(q), chunks(k), chunks(v)
    bc = beta.reshape(Bsz, nc, L, H).transpose(0, 3, 1, 2)
    gc = jnp.cumsum(g.reshape(Bsz, nc, L, H).transpose(0, 3, 1, 2), axis=-1)
    tril = jnp.tril(jnp.ones((L, L), bool))
    decay = jnp.exp(jnp.where(tril, gc[..., :, None] - gc[..., None, :], -jnp.inf))
    k_beta = kc * bc[..., None]
    v_beta = vc * bc[..., None]
    A = jnp.where(jnp.tril(jnp.ones((L, L), bool), -1),
                  jnp.einsum('bhcik,bhcjk->bhcij', k_beta, kc) * decay, 0.0)
    eye = jnp.eye(L, dtype=jnp.float32)
    T = lax.linalg.triangular_solve(eye + A, jnp.broadcast_to(eye, A.shape),
                                    left_side=True, lower=True, unit_diagonal=True)
    u_val = T @ v_beta
    k_cum = T @ (k_beta * jnp.exp(gc)[..., None])
    qk = jnp.where(tril, jnp.einsum('bhcik,bhcjk->bhcij', qc, kc) * decay, 0.0)
    q_dec = qc * jnp.exp(gc)[..., None]
    k_tail = kc * jnp.exp(gc[..., -1:] - gc)[..., None]
    g_last = jnp.exp(gc[..., -1])

    def step(state, inp):
        qk_i, u_i, kc_i, qd_i, kt_i, gl_i = inp
        v_new = u_i - kc_i @ state
        o = qd_i @ state + qk_i @ v_new
        state = state * gl_i[..., None, None] + jnp.swapaxes(kt_i, -1, -2) @ v_new
        return state, o

    xs = tuple(jnp.moveaxis(t, 2, 0) for t in (qk, u_val, k_cum, q_dec, k_tail, g_last))
    _, o = lax.scan(step, jnp.zeros((Bsz, H, K, V), jnp.float32), xs)
    o = o.transpose(1, 0, 3, 2, 4).reshape(Bsz, S, H, V)
    o = rms_norm(o) * norm_w * jax.nn.silu(z.reshape(Bsz, S, H, V).astype(jnp.float32))
    return o.reshape(Bsz, S, GDN_OUT).astype(u.dtype)


def hybrid_mixer(h, w_in, q_norm, w_uq, kv_norm, w_ukv, ssd_conv_w, ssd_conv_b, ssd_dt_bias,
                 ssd_a_log, ssd_d, ssd_norm, gdn_conv_w, gdn_conv_b, gdn_dt_bias, gdn_a_log,
                 gdn_norm, w_out):
    u = h @ w_in
    u_mla, u_ssd, u_gdn = jnp.split(u, [MLA_IN, MLA_IN + SSD_IN], axis=-1)
    o = jnp.concatenate([
        mla_mixer(u_mla, q_norm, w_uq, kv_norm, w_ukv),
        ssd_mixer(u_ssd, ssd_conv_w, ssd_conv_b, ssd_dt_bias, ssd_a_log, ssd_d, ssd_norm),
        gdn_mixer(u_gdn, gdn_conv_w, gdn_conv_b, gdn_dt_bias, gdn_a_log, gdn_norm),
    ], axis=-1)
    return o @ w_out


def swiglu(h, w_gate, w_up, w_down):
    return (jax.nn.silu(h @ w_gate) * (h @ w_up)) @ w_down


def moe_ffn(h, w_router, w_gate, w_up, w_down):
    Bsz, S, D = h.shape
    T = Bsz * S
    TK = T * TOP_K
    ht = h.reshape(T, D)
    logits = (ht @ w_router).astype(jnp.float32)
    top_v, top_i = lax.top_k(logits, TOP_K)
    top_w = jax.nn.softmax(top_v, axis=-1)
    e_flat = top_i.reshape(-1).astype(jnp.int32)
    w_flat = top_w.reshape(-1)
    tok_flat = jnp.arange(TK, dtype=jnp.int32) // TOP_K
    order = jnp.argsort(e_flat)
    e_s, tok_s, w_s = e_flat[order], tok_flat[order], w_flat[order]
    counts = jnp.bincount(e_flat, length=N_EXPERTS)
    padded = ((counts + MOE_BLOCK - 1) // MOE_BLOCK) * MOE_BLOCK
    start = jnp.cumsum(counts) - counts
    pend = jnp.cumsum(padded)
    pstart = pend - padded
    dest = pstart[e_s] + (jnp.arange(TK, dtype=jnp.int32) - start[e_s])
    n_blocks = (TK + MOE_BLOCK - 1) // MOE_BLOCK + N_EXPERTS
    n_rows = n_blocks * MOE_BLOCK
    row_tok = jnp.zeros((n_rows,), jnp.int32).at[dest].set(tok_s)
    row_w = jnp.zeros((n_rows,), jnp.float32).at[dest].set(w_s)
    block_expert = jnp.minimum(
        jnp.searchsorted(pend, jnp.arange(n_blocks) * MOE_BLOCK, side='right'), N_EXPERTS - 1)
    xb = ht[row_tok].reshape(n_blocks, MOE_BLOCK, D)

    def expert_block(args):
        xblk, e = args
        return swiglu(xblk, w_gate[e], w_up[e], w_down[e])

    yb = lax.map(expert_block, (xb, block_expert)).reshape(n_rows, D)
    out = jnp.zeros((T, D), yb.dtype).at[row_tok].add(yb * row_w[:, None].astype(yb.dtype))
    return out.reshape(Bsz, S, D).astype(h.dtype)


def modulate(x, shift, scale):
    return rms_norm(x) * (1.0 + scale[:, None, :]) + shift[:, None, :]


def setup_inputs(seed: int = 0) -> dict:
    key = jax.random.key(seed)
    ks = iter(jax.random.split(key, 32))
    f32 = jnp.float32

    def nrm(shape, scale):
        return jax.random.normal(next(ks), shape, f32) * scale

    def gain(shape):
        return 1.0 + 0.02 * jax.random.normal(next(ks), shape, f32)

    def dt_bias(shape):
        dt = jnp.exp(jax.random.uniform(next(ks), shape, f32, math.log(1e-3), math.log(1e-1)))
        return dt + jnp.log(-jnp.expm1(-dt))

    def a_log(shape):
        return jnp.log(jax.random.uniform(next(ks), shape, f32, 1.0, 16.0))

    L = DEPTH
    return {
        "x": nrm((BATCH, SEQ, D_MODEL), 1.0),
        "c": nrm((BATCH, D_MODEL), 1.0),
        "ada_w": nrm((L, D_MODEL, 6 * D_MODEL), 0.5 * D_MODEL ** -0.5),
        "ada_b": nrm((L, 6 * D_MODEL), 0.02),
        "w_in": nrm((L, D_MODEL, D_IN), D_MODEL ** -0.5),
        "mla_q_norm": gain((L, MLA_Q_RANK)),
        "mla_w_uq": nrm((L, MLA_Q_RANK, MLA_HEADS * (MLA_NOPE + MLA_ROPE)), MLA_Q_RANK ** -0.5),
        "mla_kv_norm": gain((L, MLA_KV_RANK)),
        "mla_w_ukv": nrm((L, MLA_KV_RANK, MLA_HEADS * (MLA_NOPE + MLA_V)), MLA_KV_RANK ** -0.5),
        "ssd_conv_w": nrm((L, CONV_WIDTH, SSD_CONV_DIM), CONV_WIDTH ** -0.5),
        "ssd_conv_b": nrm((L, SSD_CONV_DIM), 0.02),
        "ssd_dt_bias": dt_bias((L, SSD_HEADS)),
        "ssd_a_log": a_log((L, SSD_HEADS)),
        "ssd_d": gain((L, SSD_HEADS)),
        "ssd_norm": gain((L, SSD_D_INNER)),
        "gdn_conv_w": nrm((L, CONV_WIDTH, GDN_QKV), CONV_WIDTH ** -0.5),
        "gdn_conv_b": nrm((L, GDN_QKV), 0.02),
        "gdn_dt_bias": dt_bias((L, GDN_HEADS)),
        "gdn_a_log": a_log((L, GDN_HEADS)),
        "gdn_norm": gain((L, GDN_DV)),
        "w_out": nrm((L, D_MIX, D_MODEL), D_MIX ** -0.5),
        "ffn_w_gate": nrm((N_DENSE, D_MODEL, D_FF_DENSE), D_MODEL ** -0.5),
        "ffn_w_up": nrm((N_DENSE, D_MODEL, D_FF_DENSE), D_MODEL ** -0.5),
        "ffn_w_down": nrm((N_DENSE, D_FF_DENSE, D_MODEL), D_FF_DENSE ** -0.5),
        "moe_router": nrm((N_MOE, D_MODEL, N_EXPERTS), D_MODEL ** -0.5),
        "moe_w_gate": nrm((N_MOE, N_EXPERTS, D_MODEL, D_FF_EXPERT), D_MODEL ** -0.5),
        "moe_w_up": nrm((N_MOE, N_EXPERTS, D_MODEL, D_FF_EXPERT), D_MODEL ** -0.5),
        "moe_w_down": nrm((N_MOE, N_EXPERTS, D_FF_EXPERT, D_MODEL), D_FF_EXPERT ** -0.5),
        "final_norm": gain((D_MODEL,)),
    }


def reference(x, c, ada_w, ada_b, w_in, mla_q_norm, mla_w_uq, mla_kv_norm, mla_w_ukv,
              ssd_conv_w, ssd_conv_b, ssd_dt_bias, ssd_a_log, ssd_d, ssd_norm,
              gdn_conv_w, gdn_conv_b, gdn_dt_bias, gdn_a_log, gdn_norm, w_out,
              ffn_w_gate, ffn_w_up, ffn_w_down, moe_router, moe_w_gate, moe_w_up, moe_w_down,
              final_norm):
    c_act = jax.nn.silu(c)
    for l in range(DEPTH):
        mod = jnp.einsum('bd,de->be', c_act, ada_w[l]) + ada_b[l]
        sh1, sc1, g1, sh2, sc2, g2 = jnp.split(mod, 6, axis=-1)
        h = modulate(x, sh1, sc1)
        mix = hybrid_mixer(h, w_in[l], mla_q_norm[l], mla_w_uq[l], mla_kv_norm[l], mla_w_ukv[l],
                           ssd_conv_w[l], ssd_conv_b[l], ssd_dt_bias[l], ssd_a_log[l], ssd_d[l],
                           ssd_norm[l], gdn_conv_w[l], gdn_conv_b[l], gdn_dt_bias[l],
                           gdn_a_log[l], gdn_norm[l], w_out[l])
        x = x + g1[:, None, :] * mix
        h = modulate(x, sh2, sc2)
        if l % 2 == 0:
            f = swiglu(h, ffn_w_gate[l // 2], ffn_w_up[l // 2], ffn_w_down[l // 2])
        else:
            f = moe_ffn(h, moe_router[l // 2], moe_w_gate[l // 2], moe_w_up[l // 2],
                        moe_w_down[l // 2])
        x = x + g2[:, None, :] * f
    return rms_norm(x) * final_norm
```

```python
import functools
import math

import jax
import jax.numpy as jnp
import numpy as np
from jax import lax
from jax.experimental import pallas as pl
from jax.experimental.pallas import tpu as pltpu

F32 = jnp.float32
BF16 = jnp.bfloat16
I32 = jnp.int32
HIGHEST = lax.Precision.HIGHEST

D_MODEL = 1024
EPS = 1e-6
CONV_WIDTH = 4
MLA_HEADS, MLA_Q_RANK, MLA_KV_RANK = 4, 256, 128
MLA_NOPE, MLA_ROPE, MLA_V = 64, 32, 64
ROPE_THETA = 10000.0
SSD_HEADS, SSD_HEAD_DIM, SSD_GROUPS, SSD_STATE, SSD_CHUNK = 8, 64, 2, 64, 128
SSD_D_INNER = SSD_HEADS * SSD_HEAD_DIM
SSD_CONV_DIM = SSD_D_INNER + 2 * SSD_GROUPS * SSD_STATE
GDN_HEADS, GDN_DK, GDN_DV, GDN_CHUNK = 4, 64, 64, 64
GDN_QKV = GDN_HEADS * (2 * GDN_DK + GDN_DV)
GDN_OUT = GDN_HEADS * GDN_DV
MLA_IN = MLA_Q_RANK + MLA_KV_RANK + MLA_ROPE
MLA_OUT = MLA_HEADS * MLA_V
SSD_IN = SSD_D_INNER + SSD_CONV_DIM + SSD_HEADS
GDN_IN = GDN_QKV + GDN_HEADS * GDN_DV + 2 * GDN_HEADS
N_EXPERTS, TOP_K = 8, 2

LANES = 128
SUBLANES = 8
HALF = LANES // 2
VMEM_LIMIT_BYTES = 56 * 1024 * 1024

MISC_DT = 0
MISC_B = 8
MISC_A = 12
MISC_KR = 64
MISC_KR_SW = 96
HEAD_PAD = 128

TM_PROJ = 512
TM_FFN = 512
ATT_BLOCK = 1024
ATT_SUB = 512
SSD_STEP_CHUNKS = 2
GDN_STEP_CHUNKS = 4
MOE_ROWS = 512
MOE_FF_TILE = 512
MOE_TOK_TILE = 512
ROUTE_TILE = 512


def _cparams(sem):
    return pltpu.CompilerParams(dimension_semantics=sem, vmem_limit_bytes=VMEM_LIMIT_BYTES)


def _lane_iota(shape):
    return lax.broadcasted_iota(I32, shape, len(shape) - 1)


def _row_iota(shape):
    return lax.broadcasted_iota(I32, shape, len(shape) - 2)


def _softplus(x):
    return jnp.maximum(x, 0.0) + jnp.log1p(jnp.exp(-jnp.abs(x)))


def _silu(x):
    return x * jax.nn.sigmoid(x)


def _rms(x):
    return x * lax.rsqrt(jnp.mean(x * x, axis=-1, keepdims=True) + EPS)


def _dot(a, b):
    return jnp.dot(a.astype(BF16), b.astype(BF16), preferred_element_type=F32)


def _dot_nt(a, b):
    return lax.dot_general(a.astype(BF16), b.astype(BF16), (((1,), (1,)), ((), ())),
                           preferred_element_type=F32)


def _dot_tn(a, b):
    return lax.dot_general(a.astype(BF16), b.astype(BF16), (((0,), (0,)), ((), ())),
                           preferred_element_type=F32)


def _dot_f32(a, b):
    return jnp.dot(a, b, precision=HIGHEST, preferred_element_type=F32)


def _ada_kernel(c_ref, w_ref, b_ref, o_ref):
    c_act = _silu(c_ref[...])
    o_ref[0] = _dot_f32(c_act, w_ref[0]) + b_ref[0]


def _ada_modulation(c, ada_w, ada_b):
    depth, d, n = ada_w.shape
    bsz = c.shape[0]
    rows = max(SUBLANES, bsz)
    c_pad = jnp.zeros((rows, d), F32).at[:bsz].set(c)
    tn = 1536
    out = pl.pallas_call(
        _ada_kernel,
        grid=(depth, n // tn),
        in_specs=[
            pl.BlockSpec((rows, d), lambda l, j: (0, 0)),
            pl.BlockSpec((1, d, tn), lambda l, j: (l, 0, j)),
            pl.BlockSpec((1, 1, tn), lambda l, j: (l, 0, j)),
        ],
        out_specs=pl.BlockSpec((1, rows, tn), lambda l, j: (l, 0, j)),
        out_shape=jax.ShapeDtypeStruct((depth, rows, n), F32),
        compiler_params=_cparams(("parallel", "parallel")),
        name="ada_modulation",
    )(c_pad, ada_w, ada_b.reshape(depth, 1, n))
    return out[:, :bsz]


IN_SLABS = (("mla_c", 384), ("misc", 128), ("ssd_z", 512), ("ssd_xbc", 768),
            ("gdn_qkv", 768), ("gdn_z", 256))
IN_WIDTH = sum(w for _, w in IN_SLABS)


def _prep_w_in(w):
    d = w.shape[0]
    o_ssd = MLA_IN
    o_gdn = MLA_IN + SSD_IN
    w_kr = w[:, MLA_Q_RANK + MLA_KV_RANK:MLA_IN]
    half = MLA_ROPE // 2
    w_kr_sw = jnp.concatenate([-w_kr[:, half:], w_kr[:, :half]], axis=1)
    ssd_dt = w[:, o_ssd + SSD_D_INNER + SSD_CONV_DIM:o_ssd + SSD_IN]
    gdn_ba = w[:, o_gdn + GDN_QKV + GDN_OUT:o_gdn + GDN_IN]
    misc = jnp.concatenate(
        [ssd_dt, gdn_ba, jnp.zeros((d, MISC_KR - MISC_A - GDN_HEADS), w.dtype), w_kr, w_kr_sw], axis=1)
    cols = [
        w[:, :MLA_Q_RANK + MLA_KV_RANK], misc,
        w[:, o_ssd:o_ssd + SSD_D_INNER],
        w[:, o_ssd + SSD_D_INNER:o_ssd + SSD_D_INNER + SSD_CONV_DIM],
        w[:, o_gdn:o_gdn + GDN_QKV],
        w[:, o_gdn + GDN_QKV:o_gdn + GDN_QKV + GDN_OUT],
    ]
    return jnp.concatenate(cols, axis=1).astype(BF16)


def _in_proj_kernel(x_ref, sh_ref, sc_ref, w_ref, *o_refs):
    h = _rms(x_ref[...]) * (1.0 + sc_ref[0]) + sh_ref[0]
    hb = h.astype(BF16)
    off = 0
    for o_ref, (_, width) in zip(o_refs, IN_SLABS):
        o_ref[...] = jnp.dot(hb, w_ref[:, off:off + width], preferred_element_type=F32)
        off += width


def _in_proj(x2d, shift, scale, w_perm, seq):
    t, d = x2d.shape
    tm = min(TM_PROJ, seq)
    per_b = seq // tm
    return pl.pallas_call(
        _in_proj_kernel,
        grid=(t // tm,),
        in_specs=[
            pl.BlockSpec((tm, d), lambda i: (i, 0)),
            pl.BlockSpec((1, 1, d), lambda i: (i // per_b, 0, 0)),
            pl.BlockSpec((1, 1, d), lambda i: (i // per_b, 0, 0)),
            pl.BlockSpec((d, IN_WIDTH), lambda i: (0, 0)),
        ],
        out_specs=[pl.BlockSpec((tm, w), lambda i: (i, 0)) for _, w in IN_SLABS],
        out_shape=[jax.ShapeDtypeStruct((t, w), F32) for _, w in IN_SLABS],
        compiler_params=_cparams(("parallel",)),
        name="in_proj",
    )(x2d, shift, scale, w_perm)


def _rope_tables(seq):
    pos = jnp.arange(seq, dtype=F32)
    inv_freq = ROPE_THETA ** (-jnp.arange(0, MLA_ROPE, 2, dtype=F32) / MLA_ROPE)
    ang = pos[:, None] * inv_freq[None, :]
    cos, sin = jnp.cos(ang), jnp.sin(ang)
    zeros_l = jnp.zeros((seq, MLA_NOPE), F32)
    zeros_r = jnp.zeros((seq, HEAD_PAD - MLA_NOPE - MLA_ROPE), F32)
    cos_k = jnp.concatenate([zeros_l, cos, cos, zeros_r], axis=1)
    sin_k = jnp.concatenate([zeros_l, sin, sin, zeros_r], axis=1)
    scale = (MLA_NOPE + MLA_ROPE) ** -0.5
    cos_q = scale * jnp.concatenate([jnp.ones((seq, MLA_NOPE), F32), cos, cos, zeros_r], axis=1)
    sin_q = scale * sin_k
    return cos_q, sin_q, cos_k, sin_k


def _prep_mla_weights(w_uq, w_ukv):
    r = w_uq.shape[0]
    hq = MLA_NOPE + MLA_ROPE
    half = MLA_ROPE // 2
    qa, qb = [], []
    for h in range(MLA_HEADS):
        nope = w_uq[:, h * hq:h * hq + MLA_NOPE]
        rope = w_uq[:, h * hq + MLA_NOPE:(h + 1) * hq]
        rope_sw = jnp.concatenate([-rope[:, half:], rope[:, :half]], axis=1)
        pad = jnp.zeros((r, HEAD_PAD - hq), w_uq.dtype)
        qa += [nope, rope, pad]
        qb += [jnp.zeros((r, MLA_NOPE), w_uq.dtype), rope_sw, pad]
    rk = w_ukv.shape[0]
    hk = MLA_NOPE + MLA_V
    kcols, vcols = [], []
    for h in range(MLA_HEADS):
        kcols += [w_ukv[:, h * hk:h * hk + MLA_NOPE], jnp.zeros((rk, HEAD_PAD - MLA_NOPE), w_ukv.dtype)]
        vcols += [w_ukv[:, h * hk + MLA_NOPE:(h + 1) * hk]]
    return (jnp.concatenate(qa, axis=1).astype(BF16), jnp.concatenate(qb, axis=1).astype(BF16),
            jnp.concatenate(kcols + vcols, axis=1).astype(BF16))


def _mla_proj_kernel(c_ref, misc_ref, cq_ref, sq_ref, ck_ref, sk_ref, qn_ref, kvn_ref,
                     wqa_ref, wqb_ref, wkv_ref, q_ref, k_ref, v_ref):
    c = c_ref[...]
    cq = (_rms(c[:, :MLA_Q_RANK]) * qn_ref[...]).astype(BF16)
    ckv = (_rms(c[:, MLA_Q_RANK:]) * kvn_ref[...]).astype(BF16)
    qa = jnp.dot(cq, wqa_ref[...], preferred_element_type=F32)
    qb = jnp.dot(cq, wqb_ref[...], preferred_element_type=F32)
    cos_q = jnp.concatenate([cq_ref[...]] * MLA_HEADS, axis=1)
    sin_q = jnp.concatenate([sq_ref[...]] * MLA_HEADS, axis=1)
    q_ref[...] = (qa * cos_q + qb * sin_q).astype(BF16)
    kv = jnp.dot(ckv, wkv_ref[...], preferred_element_type=F32)
    misc = misc_ref[...]
    kr = misc * ck_ref[...] + pltpu.roll(misc, LANES - (MISC_KR_SW - MISC_KR), 1) * sk_ref[...]
    kw = MLA_HEADS * HEAD_PAD
    k_ref[...] = (kv[:, :kw] + jnp.concatenate([kr] * MLA_HEADS, axis=1)).astype(BF16)
    v_ref[...] = kv[:, kw:].astype(BF16)


def _mla_proj(mla_c, misc, tables, q_norm, kv_norm, wqa, wqb, wkv, seq):
    t = mla_c.shape[0]
    tm = min(TM_PROJ, seq)
    per_b = seq // tm
    kw = MLA_HEADS * HEAD_PAD
    tab_spec = pl.BlockSpec((tm, HEAD_PAD), lambda i: (i % per_b, 0))
    full = lambda a: pl.BlockSpec(a.shape, lambda i: (0,) * a.ndim)
    qn = q_norm.reshape(1, -1)
    kvn = kv_norm.reshape(1, -1)
    return pl.pallas_call(
        _mla_proj_kernel,
        grid=(t // tm,),
        in_specs=[
            pl.BlockSpec((tm, mla_c.shape[1]), lambda i: (i, 0)),
            pl.BlockSpec((tm, LANES), lambda i: (i, 0)),
            tab_spec, tab_spec, tab_spec, tab_spec,
            full(qn), full(kvn), full(wqa), full(wqb), full(wkv),
        ],
        out_specs=[pl.BlockSpec((tm, kw), lambda i: (i, 0)),
                   pl.BlockSpec((tm, kw), lambda i: (i, 0)),
                   pl.BlockSpec((tm, MLA_OUT), lambda i: (i, 0))],
        out_shape=[jax.ShapeDtypeStruct((t, kw), BF16), jax.ShapeDtypeStruct((t, kw), BF16),
                   jax.ShapeDtypeStruct((t, MLA_OUT), BF16)],
        compiler_params=_cparams(("parallel",)),
        name="mla_proj",
    )(mla_c, misc, *tables, qn, kvn, wqa, wqb, wkv)


def _flash_kernel(qi_tab, ki_tab, q_ref, k_ref, v_ref, o_ref, m_ref, l_ref, acc_ref, *, blk, sub):
    p = pl.program_id(2)
    qi = qi_tab[p]
    ki = ki_tab[p]

    @pl.when(ki == 0)
    def _init():
        m_ref[...] = jnp.full(m_ref.shape, -jnp.inf, F32)
        l_ref[...] = jnp.zeros(l_ref.shape, F32)
        acc_ref[...] = jnp.zeros(acc_ref.shape, F32)

    def step(masked):
        for h in range(2):
            q = q_ref[0, :, h * HEAD_PAD:(h + 1) * HEAD_PAD]
            for j in range(blk // sub):
                k = k_ref[0, j * sub:(j + 1) * sub, h * HEAD_PAD:(h + 1) * HEAD_PAD]
                v = v_ref[0, j * sub:(j + 1) * sub, :]
                s = lax.dot_general(q, k, (((1,), (1,)), ((), ())), preferred_element_type=F32)
                if masked:
                    row = _row_iota(s.shape)
                    col = _lane_iota(s.shape) + j * sub
                    s = jnp.where(col <= row, s, -jnp.inf)
                m_prev = m_ref[h]
                m_next = jnp.maximum(m_prev, jnp.max(s, axis=1, keepdims=True))
                alpha = jnp.exp(m_prev - m_next)
                pmat = jnp.exp(s - m_next[:, :1])
                l_ref[h] = alpha * l_ref[h] + jnp.sum(pmat, axis=1, keepdims=True)
                acc_ref[h] = acc_ref[h] * alpha + jnp.dot(pmat.astype(BF16), v, preferred_element_type=F32)
                m_ref[h] = m_next

    @pl.when(ki < qi)
    def _off_diagonal():
        step(False)

    @pl.when(ki == qi)
    def _diagonal():
        step(True)
        lane = _lane_iota(acc_ref.shape[1:])
        o_ref[0] = jnp.where(lane < MLA_V, acc_ref[0] / l_ref[0], acc_ref[1] / l_ref[1]).astype(o_ref.dtype)


def _flash_attention(q, k, v, bsz, seq):
    blk = min(ATT_BLOCK, seq)
    sub = min(ATT_SUB, blk)
    nb = seq // blk
    pairs = [(a, b) for a in range(nb) for b in range(a + 1)]
    qi_tab = jnp.asarray([a for a, _ in pairs], I32)
    ki_tab = jnp.asarray([b for _, b in pairs], I32)
    q3 = q.reshape(bsz, seq, -1)
    k3 = k.reshape(bsz, seq, -1)
    v3 = v.reshape(bsz, seq, -1)
    pw = 2 * HEAD_PAD
    out = pl.pallas_call(
        functools.partial(_flash_kernel, blk=blk, sub=sub),
        grid_spec=pltpu.PrefetchScalarGridSpec(
            num_scalar_prefetch=2,
            grid=(bsz, MLA_HEADS // 2, len(pairs)),
            in_specs=[
                pl.BlockSpec((1, blk, pw), lambda b, hp, p, qt, kt: (b, qt[p], hp)),
                pl.BlockSpec((1, blk, pw), lambda b, hp, p, qt, kt: (b, kt[p], hp)),
                pl.BlockSpec((1, blk, LANES), lambda b, hp, p, qt, kt: (b, kt[p], hp)),
            ],
            out_specs=pl.BlockSpec((1, blk, LANES), lambda b, hp, p, qt, kt: (b, qt[p], hp)),
            scratch_shapes=[pltpu.VMEM((2, blk, LANES), F32)] * 3,
        ),
        out_shape=jax.ShapeDtypeStruct((bsz, seq, MLA_OUT), BF16),
        compiler_params=_cparams(("parallel", "parallel", "arbitrary")),
        name="mla_flash",
    )(qi_tab, ki_tab, q3, k3, v3)
    return out.reshape(bsz * seq, MLA_OUT)


def _causal_conv_silu(cur, ext_ref, halo_ref, cw_ref, cb_ref, first):
    rows = cur.shape[0]

    @pl.when(first)
    def _zero_halo():
        halo_ref[...] = jnp.zeros(halo_ref.shape, F32)

    ext_ref[0:SUBLANES, :] = halo_ref[...]
    ext_ref[SUBLANES:SUBLANES + rows, :] = cur
    halo_ref[...] = cur[rows - SUBLANES:, :]
    acc = cb_ref[...] + cw_ref[CONV_WIDTH - 1:CONV_WIDTH, :] * cur
    for j in range(CONV_WIDTH - 1):
        start = SUBLANES - (CONV_WIDTH - 1) + j
        acc = acc + cw_ref[j:j + 1, :] * ext_ref[start:start + rows, :]
    return _silu(acc)


def _ssd_kernel(z_ref, xbc_ref, misc_ref, cw_ref, cb_ref, dtb_ref, aneg_ref, dsk_ref, nw_ref,
                y_ref, ext_ref, halo_ref, st_ref, *, nchunk):
    first = pl.program_id(1) == 0
    length = SSD_CHUNK
    xbc = _causal_conv_silu(xbc_ref[0], ext_ref, halo_ref, cw_ref, cb_ref, first)

    @pl.when(first)
    def _zero_state():
        st_ref[...] = jnp.zeros(st_ref.shape, F32)

    dt_all = _softplus(misc_ref[0] + dtb_ref[...])
    a_all = dt_all * aneg_ref[...]
    tri = (_lane_iota((length, length)) <= _row_iota((length, length))).astype(F32)
    lower = _lane_iota((length, length)) <= _row_iota((length, length))
    lane = _lane_iota((length, LANES))
    lo = lane < HALF
    bw = SSD_GROUPS * SSD_STATE

    for c in range(nchunk):
        r0 = c * length
        dt = dt_all[r0:r0 + length]
        acum = _dot_f32(tri, a_all[r0:r0 + length])
        acum_t = acum.T
        x_c = xbc[r0:r0 + length, :SSD_D_INNER]
        b_c = xbc[r0:r0 + length, SSD_D_INNER:SSD_D_INNER + bw]
        c_c = xbc[r0:r0 + length, SSD_D_INNER + bw:]
        y_parts = []
        for g in range(SSD_GROUPS):
            b_g = b_c[:, g * SSD_STATE:(g + 1) * SSD_STATE]
            c_g = c_c[:, g * SSD_STATE:(g + 1) * SSD_STATE].astype(BF16)
            b_gt = b_g.T.astype(BF16)
            cb = jnp.dot(c_g, b_gt, preferred_element_type=F32)
            pairs_per_group = SSD_HEADS // SSD_GROUPS // 2
            for pp in range(pairs_per_group):
                pr = g * pairs_per_group + pp
                h0, h1 = 2 * pr, 2 * pr + 1
                col0, col1 = acum[:, h0:h0 + 1], acum[:, h1:h1 + 1]
                l0 = jnp.exp(jnp.where(lower, col0 - acum_t[h0:h0 + 1, :], -jnp.inf))
                l1 = jnp.exp(jnp.where(lower, col1 - acum_t[h1:h1 + 1, :], -jnp.inf))
                mmat = jnp.concatenate([cb * l0, cb * l1], axis=1).astype(BF16)
                xp = x_c[:, pr * LANES:(pr + 1) * LANES]
                xdt = xp * jnp.where(lo, dt[:, h0:h0 + 1], dt[:, h1:h1 + 1])
                rhs = jnp.concatenate([jnp.where(lo, xdt, 0.0), jnp.where(lo, 0.0, xdt)], axis=0)
                y_diag = jnp.dot(mmat, rhs.astype(BF16), preferred_element_type=F32)
                col_pair = jnp.where(lo, col0, col1)
                last_pair = jnp.where(lo[:1], acum[length - 1:length, h0:h0 + 1],
                                      acum[length - 1:length, h1:h1 + 1])
                st = st_ref[pr]
                y_off = jnp.dot(c_g, st.astype(BF16), preferred_element_type=F32) * jnp.exp(col_pair)
                xdec = (xdt * jnp.exp(last_pair - col_pair)).astype(BF16)
                st_ref[pr] = st * jnp.exp(last_pair) + jnp.dot(b_gt, xdec, preferred_element_type=F32)
                y_parts.append(y_diag + y_off + xp * dsk_ref[:, pr * LANES:(pr + 1) * LANES])
        y = jnp.concatenate(y_parts, axis=1) * _silu(z_ref[0, r0:r0 + length, :])
        gw = SSD_D_INNER // SSD_GROUPS
        y = jnp.concatenate([_rms(y[:, g * gw:(g + 1) * gw]) for g in range(SSD_GROUPS)], axis=1)
        y_ref[0, r0:r0 + length, :] = (y * nw_ref[...]).astype(y_ref.dtype)


def _ssd_mixer(z, xbc, misc, conv_w, conv_b, dt_bias, a_log, d_skip, norm_w, bsz, seq):
    nchunk = min(SSD_STEP_CHUNKS, seq // SSD_CHUNK)
    rows = nchunk * SSD_CHUNK
    pad = jnp.zeros((LANES - SSD_HEADS,), F32)
    dtb = jnp.concatenate([dt_bias.astype(F32), pad]).reshape(1, LANES)
    aneg = jnp.concatenate([-jnp.exp(a_log.astype(F32)), pad]).reshape(1, LANES)
    dsk = jnp.repeat(d_skip.astype(F32), SSD_HEAD_DIM).reshape(1, SSD_D_INNER)
    full = lambda a: pl.BlockSpec(a.shape, lambda b, i: (0,) * a.ndim)
    cb2 = conv_b.reshape(1, -1)
    nw2 = norm_w.reshape(1, -1)
    blk = lambda w: pl.BlockSpec((1, rows, w), lambda b, i: (b, i, 0))
    out = pl.pallas_call(
        functools.partial(_ssd_kernel, nchunk=nchunk),
        grid=(bsz, seq // rows),
        in_specs=[blk(SSD_D_INNER), blk(SSD_CONV_DIM), blk(LANES),
                  full(conv_w), full(cb2), full(dtb), full(aneg), full(dsk), full(nw2)],
        out_specs=blk(SSD_D_INNER),
        out_shape=jax.ShapeDtypeStruct((bsz, seq, SSD_D_INNER), BF16),
        scratch_shapes=[pltpu.VMEM((rows + SUBLANES, SSD_CONV_DIM), F32),
                        pltpu.VMEM((SUBLANES, SSD_CONV_DIM), F32),
                        pltpu.VMEM((SSD_HEADS // 2, SSD_STATE, LANES), F32)],
        compiler_params=_cparams(("parallel", "arbitrary")),
        name="ssd_mixer",
    )(z.reshape(bsz, seq, -1), xbc.reshape(bsz, seq, -1), misc.reshape(bsz, seq, -1),
      conv_w, cb2, dtb, aneg, dsk, nw2)
    return out.reshape(bsz * seq, SSD_D_INNER)


def _half_sum(x, lo):
    s_lo = jnp.sum(jnp.where(lo, x, 0.0), axis=1, keepdims=True)
    s_hi = jnp.sum(jnp.where(lo, 0.0, x), axis=1, keepdims=True)
    return jnp.where(lo, s_lo, s_hi)


def _gdn_kernel(qkv_ref, z_ref, misc_ref, cw_ref, cb_ref, dtb_ref, aneg_ref, nw_ref,
                o_ref, ext_ref, halo_ref, st_ref, *, nchunk):
    first = pl.program_id(1) == 0
    length = GDN_CHUNK
    two = 2 * length
    qkv = _causal_conv_silu(qkv_ref[0], ext_ref, halo_ref, cw_ref, cb_ref, first)

    @pl.when(first)
    def _zero_state():
        st_ref[...] = jnp.zeros(st_ref.shape, F32)

    misc = misc_ref[0]
    beta_all = jax.nn.sigmoid(misc)
    g_all = aneg_ref[...] * _softplus(misc + dtb_ref[...])
    tri = (_lane_iota((length, length)) <= _row_iota((length, length))).astype(F32)
    lane = _lane_iota((length, LANES))
    lo = lane < HALF
    r2 = _row_iota((two, two))
    c2 = _lane_iota((two, two))
    same_blk = (r2 < length) == (c2 < length)
    low_incl = same_blk & (c2 <= r2)
    low_strict = same_blk & (c2 < r2)
    eye = (r2 == c2).astype(F32)
    top_lo = (r2 < length) == (c2 < HALF)
    hk = GDN_HEADS * GDN_DK
    qscale = GDN_DK ** -0.5

    def blockdiag(slab):
        return jnp.where(top_lo, jnp.concatenate([slab, slab], axis=0), 0.0)

    def fold(bd):
        return bd[:length] + bd[length:]

    for c in range(nchunk):
        r0 = c * length
        gcum = _dot_f32(tri, g_all[r0:r0 + length])
        beta = beta_all[r0:r0 + length]
        for pr in range(GDN_HEADS // 2):
            h0, h1 = 2 * pr, 2 * pr + 1
            q = qkv[r0:r0 + length, pr * LANES:(pr + 1) * LANES]
            k = qkv[r0:r0 + length, hk + pr * LANES:hk + (pr + 1) * LANES]
            v = qkv[r0:r0 + length, 2 * hk + pr * LANES:2 * hk + (pr + 1) * LANES]
            qn = q * lax.rsqrt(_half_sum(q * q, lo) + EPS) * qscale
            kn = k * lax.rsqrt(_half_sum(k * k, lo) + EPS)
            g0 = gcum[:, MISC_A + h0:MISC_A + h0 + 1]
            g1 = gcum[:, MISC_A + h1:MISC_A + h1 + 1]
            gexp = jnp.where(lo, g0, g1)
            bexp = jnp.where(lo, beta[:, MISC_B + h0:MISC_B + h0 + 1], beta[:, MISC_B + h1:MISC_B + h1 + 1])
            gcol = jnp.concatenate([jnp.broadcast_to(g0, (length, two)),
                                    jnp.broadcast_to(g1, (length, two))], axis=0)
            decay = jnp.exp(jnp.where(low_incl, gcol - gcol.T, -jnp.inf))
            kb = kn * bexp
            kk2 = jnp.concatenate([kn, kn], axis=0)
            a_mat = jnp.where(low_strict, _dot_nt(blockdiag(kb), kk2) * decay, 0.0)
            t_mat = eye - a_mat
            pw = a_mat
            for _ in range(int(math.log2(length)) - 1):
                pw = _dot(pw, pw)
                t_mat = t_mat + _dot(t_mat, pw)
            u = fold(_dot(t_mat, blockdiag(v * bexp)))
            k_cum = fold(_dot(t_mat, blockdiag(kb * jnp.exp(gexp))))
            qk = jnp.where(low_incl, _dot_nt(blockdiag(qn), kk2) * decay, 0.0)
            g_last = gexp[length - 1:length, :]
            st = st_ref[pr]
            v_new = u - _dot(k_cum, st)
            o = _dot(qn * jnp.exp(gexp), st) + fold(_dot(qk, blockdiag(v_new)))
            k_tail = kn * jnp.exp(g_last - gexp)
            st_new = st * jnp.exp(g_last) + _dot_tn(k_tail, v_new)
            st_ref[pr] = jnp.where(top_lo[:, :LANES] if two == LANES else top_lo, st_new, 0.0)
            ms = _half_sum(o * o, lo) * (1.0 / GDN_DV)
            zz = z_ref[0, r0:r0 + length, pr * LANES:(pr + 1) * LANES]
            out = o * lax.rsqrt(ms + EPS) * nw_ref[...] * _silu(zz)
            o_ref[0, r0:r0 + length, pr * LANES:(pr + 1) * LANES] = out.astype(o_ref.dtype)


def _gdn_mixer(qkv, z, misc, conv_w, conv_b, dt_bias, a_log, norm_w, bsz, seq):
    nchunk = min(GDN_STEP_CHUNKS, seq // GDN_CHUNK)
    rows = nchunk * GDN_CHUNK
    dtb = jnp.zeros((1, LANES), F32).at[0, MISC_A:MISC_A + GDN_HEADS].set(dt_bias.astype(F32))
    aneg = jnp.zeros((1, LANES), F32).at[0, MISC_A:MISC_A + GDN_HEADS].set(-jnp.exp(a_log.astype(F32)))
    nw2 = jnp.concatenate([norm_w.astype(F32)] * 2).reshape(1, LANES)
    cb2 = conv_b.reshape(1, -1)
    full = lambda a: pl.BlockSpec(a.shape, lambda b, i: (0,) * a.ndim)
    blk = lambda w: pl.BlockSpec((1, rows, w), lambda b, i: (b, i, 0))
    out = pl.pallas_call(
        functools.partial(_gdn_kernel, nchunk=nchunk),
        grid=(bsz, seq // rows),
        in_specs=[blk(GDN_QKV), blk(GDN_OUT), blk(LANES),
                  full(conv_w), full(cb2), full(dtb), full(aneg), full(nw2)],
        out_specs=blk(GDN_OUT),
        out_shape=jax.ShapeDtypeStruct((bsz, seq, GDN_OUT), BF16),
        scratch_shapes=[pltpu.VMEM((rows + SUBLANES, GDN_QKV), F32),
                        pltpu.VMEM((SUBLANES, GDN_QKV), F32),
                        pltpu.VMEM((GDN_HEADS // 2, 2 * GDN_DK, LANES), F32)],
        compiler_params=_cparams(("parallel", "arbitrary")),
        name="gdn_mixer",
    )(qkv.reshape(bsz, seq, -1), z.reshape(bsz, seq, -1), misc.reshape(bsz, seq, -1),
      conv_w, cb2, dtb, aneg, nw2)
    return out.reshape(bsz * seq, GDN_OUT)


ROUTE_E0, ROUTE_E1, ROUTE_R0, ROUTE_R1, ROUTE_W0, ROUTE_W1 = range(6)


def _post_kernel(*refs, route):
    if route:
        (x_ref, a_ref, s_ref, g_ref, wa_ref, ws_ref, wg_ref, gate_ref, sh_ref, sc_ref, wr_ref,
         x1_ref, h_ref, route_ref, cnt_ref, carry_ref) = refs
    else:
        (x_ref, a_ref, s_ref, g_ref, wa_ref, ws_ref, wg_ref, gate_ref, sh_ref, sc_ref,
         x1_ref, h_ref) = refs
    mix = (jnp.dot(a_ref[...], wa_ref[...], preferred_element_type=F32)
           + jnp.dot(s_ref[...], ws_ref[...], preferred_element_type=F32)
           + jnp.dot(g_ref[...], wg_ref[...], preferred_element_type=F32))
    x1 = x_ref[...] + gate_ref[0] * mix
    x1_ref[...] = x1
    h = _rms(x1) * (1.0 + sc_ref[0]) + sh_ref[0]
    h_ref[...] = h.astype(h_ref.dtype)
    if not route:
        return

    @pl.when(pl.program_id(0) == 0)
    def _zero_carry():
        carry_ref[...] = jnp.zeros(carry_ref.shape, F32)

    tm = h.shape[0]
    lane = _lane_iota((tm, LANES))
    logits = jnp.where(lane < N_EXPERTS, _dot_f32(h, wr_ref[...]), -jnp.inf)
    lane_f = lane.astype(F32)
    m0 = jnp.max(logits, axis=1, keepdims=True)
    e0 = jnp.min(jnp.where(logits == m0, lane_f, float(LANES)), axis=1, keepdims=True)
    rest = jnp.where(lane_f == e0, -jnp.inf, logits)
    m1 = jnp.max(rest, axis=1, keepdims=True)
    e1 = jnp.min(jnp.where(rest == m1, lane_f, float(LANES)), axis=1, keepdims=True)
    ex = jnp.exp(m1 - m0)
    w0 = 1.0 / (1.0 + ex)
    w1 = ex / (1.0 + ex)
    oh0 = (lane_f == e0).astype(F32)
    oh1 = (lane_f == e1).astype(F32)
    both = oh0 + oh1
    strict = (_lane_iota((tm, tm)) < _row_iota((tm, tm))).astype(BF16)
    before = jnp.dot(strict, both.astype(BF16), preferred_element_type=F32) + carry_ref[0:1, :]
    r0 = jnp.sum(before * oh0, axis=1, keepdims=True)
    r1 = jnp.sum(before * oh1, axis=1, keepdims=True)
    carry_ref[0:1, :] = carry_ref[0:1, :] + jnp.sum(both, axis=0, keepdims=True)
    slab = jnp.zeros((tm, LANES), F32)
    for pos, val in ((ROUTE_E0, e0), (ROUTE_E1, e1), (ROUTE_R0, r0), (ROUTE_R1, r1),
                     (ROUTE_W0, w0), (ROUTE_W1, w1)):
        slab = jnp.where(lane == pos, val, slab)
    route_ref[...] = slab
    cnt_ref[...] = carry_ref[...]


def _post_mixer(x2d, mla_o, ssd_y, gdn_o, w_out, gate, shift, scale, seq, w_router=None):
    t, d = x2d.shape
    route = w_router is not None
    tm = min(ROUTE_TILE if route else TM_PROJ, seq)
    per_b = seq // tm
    wa = w_out[:MLA_OUT].astype(BF16)
    ws = w_out[MLA_OUT:MLA_OUT + SSD_D_INNER].astype(BF16)
    wg = w_out[MLA_OUT + SSD_D_INNER:].astype(BF16)
    row = lambda w: pl.BlockSpec((tm, w), lambda i: (i, 0))
    full = lambda a: pl.BlockSpec(a.shape, lambda i: (0,) * a.ndim)
    mod = pl.BlockSpec((1, 1, d), lambda i: (i // per_b, 0, 0))
    in_specs = [row(d), row(MLA_OUT), row(SSD_D_INNER), row(GDN_OUT), full(wa), full(ws), full(wg),
                mod, mod, mod]
    args = [x2d, mla_o, ssd_y, gdn_o, wa, ws, wg, gate, shift, scale]
    out_specs = [row(d), row(d)]
    out_shape = [jax.ShapeDtypeStruct((t, d), F32), jax.ShapeDtypeStruct((t, d), BF16)]
    scratch = []
    if route:
        wr = jnp.zeros((d, LANES), F32).at[:, :N_EXPERTS].set(w_router.astype(F32))
        in_specs.append(full(wr))
        args.append(wr)
        out_specs += [row(LANES), pl.BlockSpec((SUBLANES, LANES), lambda i: (0, 0))]
        out_shape += [jax.ShapeDtypeStruct((t, LANES), F32), jax.ShapeDtypeStruct((SUBLANES, LANES), F32)]
        scratch = [pltpu.VMEM((SUBLANES, LANES), F32)]
    return pl.pallas_call(
        functools.partial(_post_kernel, route=route),
        grid=(t // tm,),
        in_specs=in_specs, out_specs=out_specs, out_shape=out_shape, scratch_shapes=scratch,
        compiler_params=_cparams(("arbitrary",) if route else ("parallel",)),
        name="post_mixer_route" if route else "post_mixer",
    )(*args)


def _finish(x, final, fn_ref):
    return _rms(x) * fn_ref[...] if final else x


def _ffn_kernel(h_ref, x1_ref, gate_ref, fn_ref, wg_ref, wu_ref, wd_ref, o_ref, *, final):
    f = pl.program_id(1)
    h = h_ref[...]
    act = _silu(jnp.dot(h, wg_ref[...], preferred_element_type=F32)) * jnp.dot(
        h, wu_ref[...], preferred_element_type=F32)
    part = jnp.dot(act.astype(BF16), wd_ref[...], preferred_element_type=F32)

    @pl.when(f == 0)
    def _set():
        o_ref[...] = part

    @pl.when(f > 0)
    def _add():
        o_ref[...] += part

    @pl.when(f == pl.num_programs(1) - 1)
    def _residual():
        o_ref[...] = _finish(x1_ref[...] + gate_ref[0] * o_ref[...], final, fn_ref)


def _ffn_tile(dff):
    for cand in (1408, 1024, 512, 256, 128):
        if dff % cand == 0:
            return cand
    raise ValueError(f"unsupported d_ff {dff}")


def _dense_ffn(h, x1, gate, w_gate, w_up, w_down, final_norm, final, seq):
    t, d = x1.shape
    dff = w_gate.shape[1]
    tm = min(TM_FFN, seq)
    tf = _ffn_tile(dff)
    per_b = seq // tm
    fn = final_norm.reshape(1, d).astype(F32)
    return pl.pallas_call(
        functools.partial(_ffn_kernel, final=final),
        grid=(t // tm, dff // tf),
        in_specs=[
            pl.BlockSpec((tm, d), lambda i, f: (i, 0)),
            pl.BlockSpec((tm, d), lambda i, f: (i, 0)),
            pl.BlockSpec((1, 1, d), lambda i, f: (i // per_b, 0, 0)),
            pl.BlockSpec((1, d), lambda i, f: (0, 0)),
            pl.BlockSpec((d, tf), lambda i, f: (0, f)),
            pl.BlockSpec((d, tf), lambda i, f: (0, f)),
            pl.BlockSpec((tf, d), lambda i, f: (f, 0)),
        ],
        out_specs=pl.BlockSpec((tm, d), lambda i, f: (i, 0)),
        out_shape=jax.ShapeDtypeStruct((t, d), F32),
        compiler_params=_cparams(("parallel", "arbitrary")),
        name="dense_ffn",
    )(h, x1, gate, fn, w_gate.astype(BF16), w_up.astype(BF16), w_down.astype(BF16))


def _row_copy(src, dst, sem):
    return pltpu.make_async_copy(src, dst, sem)


def _dispatch_kernel(dest_ref, h_ref, zeros_hbm, xs_hbm, sem, *, tm):
    del zeros_hbm

    def issue(t, carry):
        for kk in range(TOP_K):
            d = dest_ref[0, 0, TOP_K * t + kk]
            _row_copy(h_ref.at[pl.ds(t, 1), :], xs_hbm.at[pl.ds(d, 1), :], sem).start()
        return carry

    lax.fori_loop(0, tm, issue, 0)

    def drain(t, carry):
        for kk in range(TOP_K):
            _row_copy(h_ref.at[pl.ds(0, 1), :], xs_hbm.at[pl.ds(0, 1), :], sem).wait()
        return carry

    lax.fori_loop(0, tm, drain, 0)


def _moe_dispatch(h_f32, dest, n_rows):
    t, d = h_f32.shape
    tm = min(MOE_TOK_TILE, t)
    dest3 = dest.reshape(t // tm, 1, TOP_K * tm)
    return pl.pallas_call(
        functools.partial(_dispatch_kernel, tm=tm),
        grid=(t // tm,),
        in_specs=[
            pl.BlockSpec((1, 1, TOP_K * tm), lambda i: (i, 0, 0), memory_space=pltpu.SMEM),
            pl.BlockSpec((tm, d), lambda i: (i, 0)),
            pl.BlockSpec(memory_space=pl.ANY),
        ],
        out_specs=pl.BlockSpec(memory_space=pl.ANY),
        out_shape=jax.ShapeDtypeStruct((n_rows, d), F32),
        scratch_shapes=[pltpu.SemaphoreType.DMA(())],
        input_output_aliases={2: 0},
        compiler_params=_cparams(("arbitrary",)),
        name="moe_dispatch",
    )(dest3, h_f32, jnp.zeros((n_rows, d), F32))


def _experts_kernel(be_ref, nused_ref, x_ref, wg_ref, wu_ref, wd_ref, y_ref):
    i = pl.program_id(0)
    f = pl.program_id(1)

    @pl.when(f == 0)
    def _zero():
        y_ref[...] = jnp.zeros(y_ref.shape, F32)

    @pl.when(i < nused_ref[0])
    def _compute():
        xb = x_ref[...].astype(BF16)
        act = _silu(jnp.dot(xb, wg_ref[0], preferred_element_type=F32)) * jnp.dot(
            xb, wu_ref[0], preferred_element_type=F32)
        y_ref[...] += jnp.dot(act.astype(BF16), wd_ref[0], preferred_element_type=F32)


def _moe_experts(xs, block_expert, n_used, w_gate, w_up, w_down):
    n_rows, d = xs.shape
    dff = w_gate.shape[2]
    bm = MOE_ROWS
    tf = MOE_FF_TILE if dff % MOE_FF_TILE == 0 else dff
    return pl.pallas_call(
        _experts_kernel,
        grid_spec=pltpu.PrefetchScalarGridSpec(
            num_scalar_prefetch=2,
            grid=(n_rows // bm, dff // tf),
            in_specs=[
                pl.BlockSpec((bm, d), lambda i, f, be, nu: (i, 0)),
                pl.BlockSpec((1, d, tf), lambda i, f, be, nu: (be[i], 0, f)),
                pl.BlockSpec((1, d, tf), lambda i, f, be, nu: (be[i], 0, f)),
                pl.BlockSpec((1, tf, d), lambda i, f, be, nu: (be[i], f, 0)),
            ],
            out_specs=pl.BlockSpec((bm, d), lambda i, f, be, nu: (i, 0)),
        ),
        out_shape=jax.ShapeDtypeStruct((n_rows, d), F32),
        compiler_params=_cparams(("parallel", "arbitrary")),
        name="moe_experts",
    )(block_expert, n_used, xs, w_gate.astype(BF16), w_up.astype(BF16), w_down.astype(BF16))


def _combine_kernel(dest_ref, x1_ref, route_ref, gate_ref, fn_ref, ys_hbm, o_ref, buf_ref, sem,
                    *, tm, final):
    def issue(t, carry):
        for kk in range(TOP_K):
            d = dest_ref[0, 0, TOP_K * t + kk]
            _row_copy(ys_hbm.at[pl.ds(d, 1), :], buf_ref.at[kk, pl.ds(t, 1), :], sem).start()
        return carry

    lax.fori_loop(0, tm, issue, 0)

    def drain(t, carry):
        for kk in range(TOP_K):
            _row_copy(ys_hbm.at[pl.ds(0, 1), :], buf_ref.at[kk, pl.ds(0, 1), :], sem).wait()
        return carry

    lax.fori_loop(0, tm, drain, 0)
    route = route_ref[...]
    w0 = route[:, ROUTE_W0:ROUTE_W0 + 1]
    w1 = route[:, ROUTE_W1:ROUTE_W1 + 1]
    f = w0 * buf_ref[0] + w1 * buf_ref[1]
    o_ref[...] = _finish(x1_ref[...] + gate_ref[0] * f, final, fn_ref)


def _moe_combine(ys, dest, x1, route, gate, final_norm, final, seq):
    t, d = x1.shape
    tm = min(MOE_TOK_TILE, seq)
    per_b = seq // tm
    dest3 = dest.reshape(t // tm, 1, TOP_K * tm)
    fn = final_norm.reshape(1, d).astype(F32)
    return pl.pallas_call(
        functools.partial(_combine_kernel, tm=tm, final=final),
        grid=(t // tm,),
        in_specs=[
            pl.BlockSpec((1, 1, TOP_K * tm), lambda i: (i, 0, 0), memory_space=pltpu.SMEM),
            pl.BlockSpec((tm, d), lambda i: (i, 0)),
            pl.BlockSpec((tm, LANES), lambda i: (i, 0)),
            pl.BlockSpec((1, 1, d), lambda i: (i // per_b, 0, 0)),
            pl.BlockSpec((1, d), lambda i: (0, 0)),
            pl.BlockSpec(memory_space=pl.ANY),
        ],
        out_specs=pl.BlockSpec((tm, d), lambda i: (i, 0)),
        out_shape=jax.ShapeDtypeStruct((t, d), F32),
        scratch_shapes=[pltpu.VMEM((TOP_K, tm, d), F32), pltpu.SemaphoreType.DMA(())],
        compiler_params=_cparams(("arbitrary",)),
        name="moe_combine",
    )(dest3, x1, route, gate, fn, ys)


def _moe_ffn(h_bf16, x1, route, counts_slab, gate, w_gate, w_up, w_down, final_norm, final, seq):
    t, d = x1.shape
    bm = MOE_ROWS
    n_blocks = (t * TOP_K + bm - 1) // bm + N_EXPERTS
    n_rows = n_blocks * bm
    counts = counts_slab[0, :N_EXPERTS].astype(I32)
    padded = ((counts + bm - 1) // bm) * bm
    pend = jnp.cumsum(padded)
    pstart = pend - padded
    e = route[:, ROUTE_E0:ROUTE_E1 + 1].astype(I32)
    rank = route[:, ROUTE_R0:ROUTE_R1 + 1].astype(I32)
    dest = (jnp.sum(jnp.where(e[..., None] == jnp.arange(N_EXPERTS), pstart, 0), axis=-1) + rank)
    blk_start = jnp.arange(n_blocks, dtype=I32) * bm
    block_expert = jnp.minimum(jnp.sum(blk_start[:, None] >= pend[None, :], axis=1), N_EXPERTS - 1).astype(I32)
    n_used = (pend[-1:] // bm).astype(I32)
    xs = _moe_dispatch(h_bf16.astype(F32), dest, n_rows)
    ys = _moe_experts(xs, block_expert, n_used, w_gate, w_up, w_down)
    return _moe_combine(ys, dest, x1, route, gate, final_norm, final, seq)


def kernel(x, c, ada_w, ada_b, w_in, mla_q_norm, mla_w_uq, mla_kv_norm, mla_w_ukv, ssd_conv_w, ssd_conv_b, ssd_dt_bias, ssd_a_log, ssd_d, ssd_norm, gdn_conv_w, gdn_conv_b, gdn_dt_bias, gdn_a_log, gdn_norm, w_out, ffn_w_gate, ffn_w_up, ffn_w_down, moe_router, moe_w_gate, moe_w_up, moe_w_down, final_norm):
    bsz, seq, d = x.shape
    depth = w_in.shape[0]
    t = bsz * seq
    mod = _ada_modulation(c, ada_w, ada_b)
    tables = _rope_tables(seq)
    xf = x.reshape(t, d)
    for l in range(depth):
        sh1, sc1, g1, sh2, sc2, g2 = [m.reshape(bsz, 1, d) for m in jnp.split(mod[l], 6, axis=-1)]
        mla_c, misc, ssd_z, ssd_xbc, gdn_qkv, gdn_z = _in_proj(xf, sh1, sc1, _prep_w_in(w_in[l]), seq)
        wqa, wqb, wkv = _prep_mla_weights(mla_w_uq[l], mla_w_ukv[l])
        q, k, v = _mla_proj(mla_c, misc, tables, mla_q_norm[l], mla_kv_norm[l], wqa, wqb, wkv, seq)
        mla_o = _flash_attention(q, k, v, bsz, seq)
        ssd_y = _ssd_mixer(ssd_z, ssd_xbc, misc, ssd_conv_w[l], ssd_conv_b[l], ssd_dt_bias[l],
                           ssd_a_log[l], ssd_d[l], ssd_norm[l], bsz, seq)
        gdn_o = _gdn_mixer(gdn_qkv, gdn_z, misc, gdn_conv_w[l], gdn_conv_b[l], gdn_dt_bias[l],
                           gdn_a_log[l], gdn_norm[l], bsz, seq)
        final = l == depth - 1
        if l % 2 == 0:
            x1, h2 = _post_mixer(xf, mla_o, ssd_y, gdn_o, w_out[l], g1, sh2, sc2, seq)
            xf = _dense_ffn(h2, x1, g2, ffn_w_gate[l // 2], ffn_w_up[l // 2], ffn_w_down[l // 2],
                            final_norm, final, seq)
        else:
            x1, h2, route, counts = _post_mixer(xf, mla_o, ssd_y, gdn_o, w_out[l], g1, sh2, sc2, seq,
                                                w_router=moe_router[l // 2])
            xf = _moe_ffn(h2, x1, route, counts, g2, moe_w_gate[l // 2], moe_w_up[l // 2],
                          moe_w_down[l // 2], final_norm, final, seq)
    return xf.reshape(bsz, seq, d)
```

```python
import functools
import math

import jax
import jax.numpy as jnp
import numpy as np
from jax import lax
from jax.experimental import pallas as pl
from jax.experimental.pallas import tpu as pltpu

F32 = jnp.float32
BF16 = jnp.bfloat16
I32 = jnp.int32
HIGHEST = lax.Precision.HIGHEST

D_MODEL = 1024
EPS = 1e-6
CONV_WIDTH = 4
MLA_HEADS, MLA_Q_RANK, MLA_KV_RANK = 4, 256, 128
MLA_NOPE, MLA_ROPE, MLA_V = 64, 32, 64
ROPE_THETA = 10000.0
SSD_HEADS, SSD_HEAD_DIM, SSD_GROUPS, SSD_STATE, SSD_CHUNK = 8, 64, 2, 64, 128
SSD_D_INNER = SSD_HEADS * SSD_HEAD_DIM
SSD_CONV_DIM = SSD_D_INNER + 2 * SSD_GROUPS * SSD_STATE
GDN_HEADS, GDN_DK, GDN_DV, GDN_CHUNK = 4, 64, 64, 64
GDN_QKV = GDN_HEADS * (2 * GDN_DK + GDN_DV)
GDN_OUT = GDN_HEADS * GDN_DV
MLA_IN = MLA_Q_RANK + MLA_KV_RANK + MLA_ROPE
MLA_OUT = MLA_HEADS * MLA_V
SSD_IN = SSD_D_INNER + SSD_CONV_DIM + SSD_HEADS
GDN_IN = GDN_QKV + GDN_HEADS * GDN_DV + 2 * GDN_HEADS
N_EXPERTS, TOP_K = 8, 2

LANES = 128
SUBLANES = 8
HALF = LANES // 2
VMEM_LIMIT_BYTES = 56 * 1024 * 1024

MISC_DT = 0
MISC_B = 8
MISC_A = 12
MISC_KR = 64
MISC_KR_SW = 96
HEAD_PAD = 128

TM_PROJ = 512
TM_FFN = 512
ATT_BLOCK = 1024
ATT_Q_COLS = 256
ATT_K_ROWS = 512
ATT_PIPELINE = 3
SSD_STEP_CHUNKS = 2
GDN_STEP_CHUNKS = 4
MOE_ROWS = 512
MOE_FF_TILE = 512
MOE_TOK_TILE = 512
ROUTE_TILE = 512


def _cparams(sem):
    return pltpu.CompilerParams(dimension_semantics=sem, vmem_limit_bytes=VMEM_LIMIT_BYTES)


def _lane_iota(shape):
    return lax.broadcasted_iota(I32, shape, len(shape) - 1)


def _row_iota(shape):
    return lax.broadcasted_iota(I32, shape, len(shape) - 2)


def _softplus(x):
    return jnp.maximum(x, 0.0) + jnp.log1p(jnp.exp(-jnp.abs(x)))


def _silu(x):
    return x * jax.nn.sigmoid(x)


def _rms(x):
    return x * lax.rsqrt(jnp.mean(x * x, axis=-1, keepdims=True) + EPS)


def _dot(a, b):
    return jnp.dot(a.astype(BF16), b.astype(BF16), preferred_element_type=F32)


def _dot_nt(a, b):
    return lax.dot_general(a.astype(BF16), b.astype(BF16), (((1,), (1,)), ((), ())),
                           preferred_element_type=F32)


def _dot_tn(a, b):
    return lax.dot_general(a.astype(BF16), b.astype(BF16), (((0,), (0,)), ((), ())),
                           preferred_element_type=F32)


def _dot_f32(a, b):
    return jnp.dot(a, b, precision=HIGHEST, preferred_element_type=F32)


def _ada_kernel(c_ref, w_ref, b_ref, o_ref):
    c_act = _silu(c_ref[...])
    o_ref[0] = _dot_f32(c_act, w_ref[0]) + b_ref[0]


def _ada_modulation(c, ada_w, ada_b):
    depth, d, n = ada_w.shape
    bsz = c.shape[0]
    rows = max(SUBLANES, bsz)
    c_pad = jnp.zeros((rows, d), F32).at[:bsz].set(c)
    tn = 1536
    out = pl.pallas_call(
        _ada_kernel,
        grid=(depth, n // tn),
        in_specs=[
            pl.BlockSpec((rows, d), lambda l, j: (0, 0)),
            pl.BlockSpec((1, d, tn), lambda l, j: (l, 0, j)),
            pl.BlockSpec((1, 1, tn), lambda l, j: (l, 0, j)),
        ],
        out_specs=pl.BlockSpec((1, rows, tn), lambda l, j: (l, 0, j)),
        out_shape=jax.ShapeDtypeStruct((depth, rows, n), F32),
        compiler_params=_cparams(("parallel", "parallel")),
        name="ada_modulation",
    )(c_pad, ada_w, ada_b.reshape(depth, 1, n))
    return out[:, :bsz]


IN_SLABS = (("mla_c", 384), ("misc", 128), ("ssd_z", 512), ("ssd_xbc", 768),
            ("gdn_qkv", 768), ("gdn_z", 256))
IN_WIDTH = sum(w for _, w in IN_SLABS)


def _prep_w_in(w):
    d = w.shape[0]
    o_ssd = MLA_IN
    o_gdn = MLA_IN + SSD_IN
    w_kr = w[:, MLA_Q_RANK + MLA_KV_RANK:MLA_IN]
    half = MLA_ROPE // 2
    w_kr_sw = jnp.concatenate([-w_kr[:, half:], w_kr[:, :half]], axis=1)
    ssd_dt = w[:, o_ssd + SSD_D_INNER + SSD_CONV_DIM:o_ssd + SSD_IN]
    gdn_ba = w[:, o_gdn + GDN_QKV + GDN_OUT:o_gdn + GDN_IN]
    misc = jnp.concatenate(
        [ssd_dt, gdn_ba, jnp.zeros((d, MISC_KR - MISC_A - GDN_HEADS), w.dtype), w_kr, w_kr_sw], axis=1)
    cols = [
        w[:, :MLA_Q_RANK + MLA_KV_RANK], misc,
        w[:, o_ssd:o_ssd + SSD_D_INNER],
        w[:, o_ssd + SSD_D_INNER:o_ssd + SSD_D_INNER + SSD_CONV_DIM],
        w[:, o_gdn:o_gdn + GDN_QKV],
        w[:, o_gdn + GDN_QKV:o_gdn + GDN_QKV + GDN_OUT],
    ]
    return jnp.concatenate(cols, axis=1).astype(BF16)


def _in_proj_kernel(x_ref, sh_ref, sc_ref, w_ref, *o_refs):
    h = _rms(x_ref[...]) * (1.0 + sc_ref[0]) + sh_ref[0]
    hb = h.astype(BF16)
    off = 0
    for o_ref, (_, width) in zip(o_refs, IN_SLABS):
        o_ref[...] = jnp.dot(hb, w_ref[:, off:off + width], preferred_element_type=F32)
        off += width


def _in_proj(x2d, shift, scale, w_perm, seq):
    t, d = x2d.shape
    tm = min(TM_PROJ, seq)
    per_b = seq // tm
    return pl.pallas_call(
        _in_proj_kernel,
        grid=(t // tm,),
        in_specs=[
            pl.BlockSpec((tm, d), lambda i: (i, 0)),
            pl.BlockSpec((1, 1, d), lambda i: (i // per_b, 0, 0)),
            pl.BlockSpec((1, 1, d), lambda i: (i // per_b, 0, 0)),
            pl.BlockSpec((d, IN_WIDTH), lambda i: (0, 0)),
        ],
        out_specs=[pl.BlockSpec((tm, w), lambda i: (i, 0)) for _, w in IN_SLABS],
        out_shape=[jax.ShapeDtypeStruct((t, w), F32) for _, w in IN_SLABS],
        compiler_params=_cparams(("parallel",)),
        name="in_proj",
    )(x2d, shift, scale, w_perm)


def _rope_tables(seq):
    pos = jnp.arange(seq, dtype=F32)
    inv_freq = ROPE_THETA ** (-jnp.arange(0, MLA_ROPE, 2, dtype=F32) / MLA_ROPE)
    ang = pos[:, None] * inv_freq[None, :]
    cos, sin = jnp.cos(ang), jnp.sin(ang)
    zeros_l = jnp.zeros((seq, MLA_NOPE), F32)
    zeros_r = jnp.zeros((seq, HEAD_PAD - MLA_NOPE - MLA_ROPE), F32)
    cos_k = jnp.concatenate([zeros_l, cos, cos, zeros_r], axis=1)
    sin_k = jnp.concatenate([zeros_l, sin, sin, zeros_r], axis=1)
    scale = (MLA_NOPE + MLA_ROPE) ** -0.5 * math.log2(math.e)
    cos_q = scale * jnp.concatenate([jnp.ones((seq, MLA_NOPE), F32), cos, cos, zeros_r], axis=1)
    sin_q = scale * sin_k
    return cos_q, sin_q, cos_k, sin_k


def _prep_mla_weights(w_uq, w_ukv):
    r = w_uq.shape[0]
    hq = MLA_NOPE + MLA_ROPE
    half = MLA_ROPE // 2
    qa, qb = [], []
    for h in range(MLA_HEADS):
        nope = w_uq[:, h * hq:h * hq + MLA_NOPE]
        rope = w_uq[:, h * hq + MLA_NOPE:(h + 1) * hq]
        rope_sw = jnp.concatenate([-rope[:, half:], rope[:, :half]], axis=1)
        pad = jnp.zeros((r, HEAD_PAD - hq), w_uq.dtype)
        qa += [nope, rope, pad]
        qb += [jnp.zeros((r, MLA_NOPE), w_uq.dtype), rope_sw, pad]
    rk = w_ukv.shape[0]
    hk = MLA_NOPE + MLA_V
    kcols, vcols = [], []
    for h in range(MLA_HEADS):
        kcols += [w_ukv[:, h * hk:h * hk + MLA_NOPE], jnp.zeros((rk, HEAD_PAD - MLA_NOPE), w_ukv.dtype)]
        vcols += [w_ukv[:, h * hk + MLA_NOPE:(h + 1) * hk], jnp.zeros((rk, HEAD_PAD - MLA_V), w_ukv.dtype)]
    return (jnp.concatenate(qa, axis=1).astype(BF16), jnp.concatenate(qb, axis=1).astype(BF16),
            jnp.concatenate(kcols + vcols, axis=1).astype(BF16))


def _mla_proj_kernel(c_ref, misc_ref, cq_ref, sq_ref, ck_ref, sk_ref, qn_ref, kvn_ref,
                     wqa_ref, wqb_ref, wkv_ref, qt_ref, k_ref, vt_ref):
    c = c_ref[...]
    cq = (_rms(c[:, :MLA_Q_RANK]) * qn_ref[...]).astype(BF16)
    ckv = (_rms(c[:, MLA_Q_RANK:]) * kvn_ref[...]).astype(BF16)
    qa = jnp.dot(cq, wqa_ref[...], preferred_element_type=F32)
    qb = jnp.dot(cq, wqb_ref[...], preferred_element_type=F32)
    cos_q = jnp.concatenate([cq_ref[...]] * MLA_HEADS, axis=1)
    sin_q = jnp.concatenate([sq_ref[...]] * MLA_HEADS, axis=1)
    qt_ref[0] = (qa * cos_q + qb * sin_q).T.astype(BF16)
    kv = jnp.dot(ckv, wkv_ref[...], preferred_element_type=F32)
    misc = misc_ref[...]
    kr = misc * ck_ref[...] + pltpu.roll(misc, LANES - (MISC_KR_SW - MISC_KR), 1) * sk_ref[...]
    kw = MLA_HEADS * HEAD_PAD
    k_ref[0] = (kv[:, :kw] + jnp.concatenate([kr] * MLA_HEADS, axis=1)).astype(BF16)
    v = kv[:, kw:]
    v = jnp.where(_lane_iota(v.shape) % HEAD_PAD == MLA_V, 1.0, v)
    vt_ref[0] = v.T.astype(BF16)


def _mla_proj(mla_c, misc, tables, q_norm, kv_norm, wqa, wqb, wkv, bsz, seq):
    t = mla_c.shape[0]
    tm = min(TM_PROJ, seq)
    per_b = seq // tm
    kw = MLA_HEADS * HEAD_PAD
    tab_spec = pl.BlockSpec((tm, HEAD_PAD), lambda i: (i % per_b, 0))
    full = lambda a: pl.BlockSpec(a.shape, lambda i: (0,) * a.ndim)
    qn = q_norm.reshape(1, -1)
    kvn = kv_norm.reshape(1, -1)
    t_spec = pl.BlockSpec((1, kw, tm), lambda i: (i // per_b, 0, i % per_b))
    return pl.pallas_call(
        _mla_proj_kernel,
        grid=(t // tm,),
        in_specs=[
            pl.BlockSpec((tm, mla_c.shape[1]), lambda i: (i, 0)),
            pl.BlockSpec((tm, LANES), lambda i: (i, 0)),
            tab_spec, tab_spec, tab_spec, tab_spec,
            full(qn), full(kvn), full(wqa), full(wqb), full(wkv),
        ],
        out_specs=[t_spec, pl.BlockSpec((1, tm, kw), lambda i: (i // per_b, i % per_b, 0)), t_spec],
        out_shape=[jax.ShapeDtypeStruct((bsz, kw, seq), BF16), jax.ShapeDtypeStruct((bsz, seq, kw), BF16),
                   jax.ShapeDtypeStruct((bsz, kw, seq), BF16)],
        compiler_params=_cparams(("parallel",)),
        name="mla_proj",
    )(mla_c, misc, *tables, qn, kvn, wqa, wqb, wkv)


def _flash_kernel(qi_tab, ki_tab, qt_ref, k_ref, vt_ref, o_ref, m_ref, acc_ref, *, blk, rq, ck):
    p = pl.program_id(1)
    qi = qi_tab[p]
    ki = ki_tab[p]

    @pl.when(ki == 0)
    def _init():
        m_ref[...] = jnp.full(m_ref.shape, -jnp.inf, F32)
        acc_ref[...] = jnp.zeros(acc_ref.shape, F32)

    def step(diag):
        units = []
        for rc in range(blk // rq):
            for kc in range(blk // ck):
                if diag and kc * ck > rc * rq + rq - 1:
                    continue
                for h in range(MLA_HEADS):
                    units.append((h, rc * rq, kc * ck))

        def scores(unit):
            h, c0, k0 = unit
            kk = k_ref[0, k0:k0 + ck, h * HEAD_PAD:(h + 1) * HEAD_PAD]
            qt = qt_ref[0, h * HEAD_PAD:(h + 1) * HEAD_PAD, c0:c0 + rq]
            s = jnp.dot(kk, qt, preferred_element_type=F32)
            if diag and k0 + ck - 1 > c0:
                key = lax.broadcasted_iota(I32, s.shape, 0) + k0
                qry = lax.broadcasted_iota(I32, s.shape, 1) + c0
                s = jnp.where(key <= qry, s, -jnp.inf)
            return s

        def update(unit, s):
            h, c0, k0 = unit
            m_prev = m_ref[h, :, c0:c0 + rq]
            m_new = jnp.maximum(m_prev, jnp.max(s, axis=0, keepdims=True))
            alpha = jnp.exp2(m_prev - m_new)
            pt = jnp.exp2(s - m_new).astype(BF16)
            m_ref[h, :, c0:c0 + rq] = m_new
            acc_ref[h, :, c0:c0 + rq] = acc_ref[h, :, c0:c0 + rq] * alpha + jnp.dot(
                vt_ref[0, h * HEAD_PAD:(h + 1) * HEAD_PAD, k0:k0 + ck], pt, preferred_element_type=F32)

        pending = [scores(u) for u in units[:ATT_PIPELINE]]
        for n, unit in enumerate(units):
            s = pending.pop(0)
            if n + ATT_PIPELINE < len(units):
                pending.append(scores(units[n + ATT_PIPELINE]))
            update(unit, s)

    @pl.when(ki < qi)
    def _off_diagonal():
        step(False)

    @pl.when(ki == qi)
    def _diagonal():
        step(True)
        for h in range(MLA_HEADS):
            acc = acc_ref[h]
            o_ref[0, h * MLA_V:(h + 1) * MLA_V, :] = (acc[:MLA_V] / acc[MLA_V:MLA_V + 1]).astype(o_ref.dtype)


def _flash_attention(qt, k, vt, bsz, seq):
    blk = min(ATT_BLOCK, seq)
    rq = min(ATT_Q_COLS, blk)
    ck = min(ATT_K_ROWS, blk)
    nb = seq // blk
    pairs = [(a, b) for a in range(nb) for b in range(a + 1)]
    qi_tab = jnp.asarray([a for a, _ in pairs], I32)
    ki_tab = jnp.asarray([b for _, b in pairs], I32)
    kw = MLA_HEADS * HEAD_PAD
    return pl.pallas_call(
        functools.partial(_flash_kernel, blk=blk, rq=rq, ck=ck),
        grid_spec=pltpu.PrefetchScalarGridSpec(
            num_scalar_prefetch=2,
            grid=(bsz, len(pairs)),
            in_specs=[
                pl.BlockSpec((1, kw, blk), lambda b, p, qt_, kt_: (b, 0, qt_[p])),
                pl.BlockSpec((1, blk, kw), lambda b, p, qt_, kt_: (b, kt_[p], 0)),
                pl.BlockSpec((1, kw, blk), lambda b, p, qt_, kt_: (b, 0, kt_[p])),
            ],
            out_specs=pl.BlockSpec((1, MLA_OUT, blk), lambda b, p, qt_, kt_: (b, 0, qt_[p])),
            scratch_shapes=[pltpu.VMEM((MLA_HEADS, 1, blk), F32),
                            pltpu.VMEM((MLA_HEADS, HEAD_PAD, blk), F32)],
        ),
        out_shape=jax.ShapeDtypeStruct((bsz, MLA_OUT, seq), BF16),
        compiler_params=_cparams(("parallel", "arbitrary")),
        name="mla_flash",
    )(qi_tab, ki_tab, qt, k, vt)


def _causal_conv_silu(cur, ext_ref, halo_ref, cw_ref, cb_ref, first):
    rows = cur.shape[0]

    @pl.when(first)
    def _zero_halo():
        halo_ref[...] = jnp.zeros(halo_ref.shape, F32)

    ext_ref[0:SUBLANES, :] = halo_ref[...]
    ext_ref[SUBLANES:SUBLANES + rows, :] = cur
    halo_ref[...] = cur[rows - SUBLANES:, :]
    acc = cb_ref[...] + cw_ref[CONV_WIDTH - 1:CONV_WIDTH, :] * cur
    for j in range(CONV_WIDTH - 1):
        start = SUBLANES - (CONV_WIDTH - 1) + j
        acc = acc + cw_ref[j:j + 1, :] * ext_ref[start:start + rows, :]
    return _silu(acc)


def _ssd_kernel(z_ref, xbc_ref, misc_ref, cw_ref, cb_ref, dtb_ref, aneg_ref, dsk_ref, nw_ref,
                y_ref, ext_ref, halo_ref, st_ref, *, nchunk):
    first = pl.program_id(1) == 0
    length = SSD_CHUNK
    xbc = _causal_conv_silu(xbc_ref[0], ext_ref, halo_ref, cw_ref, cb_ref, first)

    @pl.when(first)
    def _zero_state():
        st_ref[...] = jnp.zeros(st_ref.shape, F32)

    dt_all = _softplus(misc_ref[0] + dtb_ref[...])
    a_all = dt_all * aneg_ref[...]
    tri = (_lane_iota((length, length)) <= _row_iota((length, length))).astype(F32)
    lower = _lane_iota((length, length)) <= _row_iota((length, length))
    lane = _lane_iota((length, LANES))
    lo = lane < HALF
    bw = SSD_GROUPS * SSD_STATE

    for c in range(nchunk):
        r0 = c * length
        dt = dt_all[r0:r0 + length]
        acum = _dot_f32(tri, a_all[r0:r0 + length])
        acum_t = acum.T
        x_c = xbc[r0:r0 + length, :SSD_D_INNER]
        b_c = xbc[r0:r0 + length, SSD_D_INNER:SSD_D_INNER + bw]
        c_c = xbc[r0:r0 + length, SSD_D_INNER + bw:]
        y_parts = []
        for g in range(SSD_GROUPS):
            b_g = b_c[:, g * SSD_STATE:(g + 1) * SSD_STATE]
            c_g = c_c[:, g * SSD_STATE:(g + 1) * SSD_STATE].astype(BF16)
            b_gt = b_g.T.astype(BF16)
            cb = jnp.dot(c_g, b_gt, preferred_element_type=F32)
            pairs_per_group = SSD_HEADS // SSD_GROUPS // 2
            for pp in range(pairs_per_group):
                pr = g * pairs_per_group + pp
                h0, h1 = 2 * pr, 2 * pr + 1
                col0, col1 = acum[:, h0:h0 + 1], acum[:, h1:h1 + 1]
                l0 = jnp.exp(jnp.where(lower, col0 - acum_t[h0:h0 + 1, :], -jnp.inf))
                l1 = jnp.exp(jnp.where(lower, col1 - acum_t[h1:h1 + 1, :], -jnp.inf))
                mmat = jnp.concatenate([cb * l0, cb * l1], axis=1).astype(BF16)
                xp = x_c[:, pr * LANES:(pr + 1) * LANES]
                xdt = xp * jnp.where(lo, dt[:, h0:h0 + 1], dt[:, h1:h1 + 1])
                rhs = jnp.concatenate([jnp.where(lo, xdt, 0.0), jnp.where(lo, 0.0, xdt)], axis=0)
                y_diag = jnp.dot(mmat, rhs.astype(BF16), preferred_element_type=F32)
                col_pair = jnp.where(lo, col0, col1)
                last_pair = jnp.where(lo[:1], acum[length - 1:length, h0:h0 + 1],
                                      acum[length - 1:length, h1:h1 + 1])
                st = st_ref[pr]
                y_off = jnp.dot(c_g, st.astype(BF16), preferred_element_type=F32) * jnp.exp(col_pair)
                xdec = (xdt * jnp.exp(last_pair - col_pair)).astype(BF16)
                st_ref[pr] = st * jnp.exp(last_pair) + jnp.dot(b_gt, xdec, preferred_element_type=F32)
                y_parts.append(y_diag + y_off + xp * dsk_ref[:, pr * LANES:(pr + 1) * LANES])
        y = jnp.concatenate(y_parts, axis=1) * _silu(z_ref[0, r0:r0 + length, :])
        gw = SSD_D_INNER // SSD_GROUPS
        y = jnp.concatenate([_rms(y[:, g * gw:(g + 1) * gw]) for g in range(SSD_GROUPS)], axis=1)
        y_ref[0, r0:r0 + length, :] = (y * nw_ref[...]).astype(y_ref.dtype)


def _ssd_mixer(z, xbc, misc, conv_w, conv_b, dt_bias, a_log, d_skip, norm_w, bsz, seq):
    nchunk = min(SSD_STEP_CHUNKS, seq // SSD_CHUNK)
    rows = nchunk * SSD_CHUNK
    pad = jnp.zeros((LANES - SSD_HEADS,), F32)
    dtb = jnp.concatenate([dt_bias.astype(F32), pad]).reshape(1, LANES)
    aneg = jnp.concatenate([-jnp.exp(a_log.astype(F32)), pad]).reshape(1, LANES)
    dsk = jnp.repeat(d_skip.astype(F32), SSD_HEAD_DIM).reshape(1, SSD_D_INNER)
    full = lambda a: pl.BlockSpec(a.shape, lambda b, i: (0,) * a.ndim)
    cb2 = conv_b.reshape(1, -1)
    nw2 = norm_w.reshape(1, -1)
    blk = lambda w: pl.BlockSpec((1, rows, w), lambda b, i: (b, i, 0))
    out = pl.pallas_call(
        functools.partial(_ssd_kernel, nchunk=nchunk),
        grid=(bsz, seq // rows),
        in_specs=[blk(SSD_D_INNER), blk(SSD_CONV_DIM), blk(LANES),
                  full(conv_w), full(cb2), full(dtb), full(aneg), full(dsk), full(nw2)],
        out_specs=blk(SSD_D_INNER),
        out_shape=jax.ShapeDtypeStruct((bsz, seq, SSD_D_INNER), BF16),
        scratch_shapes=[pltpu.VMEM((rows + SUBLANES, SSD_CONV_DIM), F32),
                        pltpu.VMEM((SUBLANES, SSD_CONV_DIM), F32),
                        pltpu.VMEM((SSD_HEADS // 2, SSD_STATE, LANES), F32)],
        compiler_params=_cparams(("parallel", "arbitrary")),
        name="ssd_mixer",
    )(z.reshape(bsz, seq, -1), xbc.reshape(bsz, seq, -1), misc.reshape(bsz, seq, -1),
      conv_w, cb2, dtb, aneg, dsk, nw2)
    return out.reshape(bsz * seq, SSD_D_INNER)


def _half_sum(x, lo):
    s_lo = jnp.sum(jnp.where(lo, x, 0.0), axis=1, keepdims=True)
    s_hi = jnp.sum(jnp.where(lo, 0.0, x), axis=1, keepdims=True)
    return jnp.where(lo, s_lo, s_hi)


def _gdn_kernel(qkv_ref, z_ref, misc_ref, cw_ref, cb_ref, dtb_ref, aneg_ref, nw_ref,
                o_ref, ext_ref, halo_ref, st_ref, *, nchunk, bsz):
    first = pl.program_id(0) == 0
    length = GDN_CHUNK
    two = 2 * length

    @pl.when(first)
    def _zero_state():
        st_ref[...] = jnp.zeros(st_ref.shape, F32)

    tri = (_lane_iota((length, length)) <= _row_iota((length, length))).astype(F32)
    lane = _lane_iota((length, LANES))
    lo = lane < HALF
    r2 = _row_iota((two, two))
    c2 = _lane_iota((two, two))
    same_blk = (r2 < length) == (c2 < length)
    low_incl = same_blk & (c2 <= r2)
    low_strict = same_blk & (c2 < r2)
    eye = (r2 == c2).astype(F32)
    top_lo = (r2 < length) == (c2 < HALF)
    hk = GDN_HEADS * GDN_DK
    qscale = GDN_DK ** -0.5
    npair = GDN_HEADS // 2

    def blockdiag(slab):
        return jnp.where(top_lo, jnp.concatenate([slab, slab], axis=0), 0.0)

    def fold(bd):
        return bd[:length] + bd[length:]

    chains = []
    for b in range(bsz):
        qkv = _causal_conv_silu(qkv_ref[b], ext_ref.at[b], halo_ref.at[b], cw_ref, cb_ref, first)
        misc = misc_ref[b]
        beta_all = jax.nn.sigmoid(misc)
        g_all = aneg_ref[...] * _softplus(misc + dtb_ref[...])
        for c in range(nchunk):
            r0 = c * length
            gcum = _dot_f32(tri, g_all[r0:r0 + length])
            beta = beta_all[r0:r0 + length]
            for pr in range(npair):
                h0, h1 = 2 * pr, 2 * pr + 1
                q = qkv[r0:r0 + length, pr * LANES:(pr + 1) * LANES]
                k = qkv[r0:r0 + length, hk + pr * LANES:hk + (pr + 1) * LANES]
                v = qkv[r0:r0 + length, 2 * hk + pr * LANES:2 * hk + (pr + 1) * LANES]
                qn = q * lax.rsqrt(_half_sum(q * q, lo) + EPS) * qscale
                kn = k * lax.rsqrt(_half_sum(k * k, lo) + EPS)
                g0 = gcum[:, MISC_A + h0:MISC_A + h0 + 1]
                g1 = gcum[:, MISC_A + h1:MISC_A + h1 + 1]
                gexp = jnp.where(lo, g0, g1)
                bexp = jnp.where(lo, beta[:, MISC_B + h0:MISC_B + h0 + 1],
                                 beta[:, MISC_B + h1:MISC_B + h1 + 1])
                gcol = jnp.concatenate([jnp.broadcast_to(g0, (length, two)),
                                        jnp.broadcast_to(g1, (length, two))], axis=0)
                decay = jnp.exp(jnp.where(low_incl, gcol - gcol.T, -jnp.inf))
                kb = kn * bexp
                kk2 = jnp.concatenate([kn, kn], axis=0)
                a_mat = jnp.where(low_strict, _dot_nt(blockdiag(kb), kk2) * decay, 0.0)
                qk = jnp.where(low_incl, _dot_nt(blockdiag(qn), kk2) * decay, 0.0)
                g_last = gexp[length - 1:length, :]
                chains.append(dict(
                    b=b, c=c, pr=pr, pw=a_mat, t=eye - a_mat, qk=qk,
                    vb=v * bexp, kbg=kb * jnp.exp(gexp), q_dec=qn * jnp.exp(gexp),
                    k_tail=kn * jnp.exp(g_last - gexp), gl=jnp.exp(g_last)))

    for _ in range(int(math.log2(length)) - 1):
        for ch in chains:
            ch["pw"] = _dot(ch["pw"], ch["pw"])
        for ch in chains:
            ch["t"] = ch["t"] + _dot(ch["t"], ch["pw"])

    for ch in chains:
        ch["u"] = fold(_dot(ch["t"], blockdiag(ch["vb"])))
        ch["k_cum"] = fold(_dot(ch["t"], blockdiag(ch["kbg"])))
    for ch in chains:
        ch["m"] = jnp.where(top_lo, _dot_tn(ch["k_tail"], ch["k_cum"]), 0.0)
        ch["n"] = jnp.where(top_lo, _dot_tn(ch["k_tail"], ch["u"]), 0.0)
        ch["q_eff"] = ch["q_dec"] - fold(_dot(ch["qk"], blockdiag(ch["k_cum"])))
        ch["o_loc"] = fold(_dot(ch["qk"], blockdiag(ch["u"])))

    states = {(b, pr): st_ref[b * npair + pr] for b in range(bsz) for pr in range(npair)}
    for c in range(nchunk):
        for ch in chains:
            if ch["c"] != c:
                continue
            b, pr = ch["b"], ch["pr"]
            st = states[(b, pr)]
            o = _dot(ch["q_eff"], st) + ch["o_loc"]
            states[(b, pr)] = st * ch["gl"] - _dot(ch["m"], st) + ch["n"]
            r0 = c * length
            ms = _half_sum(o * o, lo) * (1.0 / GDN_DV)
            zz = z_ref[b, r0:r0 + length, pr * LANES:(pr + 1) * LANES]
            out = o * lax.rsqrt(ms + EPS) * nw_ref[...] * _silu(zz)
            o_ref[b, r0:r0 + length, pr * LANES:(pr + 1) * LANES] = out.astype(o_ref.dtype)
    for (b, pr), st in states.items():
        st_ref[b * npair + pr] = st


def _gdn_mixer(qkv, z, misc, conv_w, conv_b, dt_bias, a_log, norm_w, bsz, seq):
    nchunk = min(GDN_STEP_CHUNKS, seq // GDN_CHUNK)
    rows = nchunk * GDN_CHUNK
    dtb = jnp.zeros((1, LANES), F32).at[0, MISC_A:MISC_A + GDN_HEADS].set(dt_bias.astype(F32))
    aneg = jnp.zeros((1, LANES), F32).at[0, MISC_A:MISC_A + GDN_HEADS].set(-jnp.exp(a_log.astype(F32)))
    nw2 = jnp.concatenate([norm_w.astype(F32)] * 2).reshape(1, LANES)
    cb2 = conv_b.reshape(1, -1)
    full = lambda a: pl.BlockSpec(a.shape, lambda i: (0,) * a.ndim)
    blk = lambda w: pl.BlockSpec((bsz, rows, w), lambda i: (0, i, 0))
    out = pl.pallas_call(
        functools.partial(_gdn_kernel, nchunk=nchunk, bsz=bsz),
        grid=(seq // rows,),
        in_specs=[blk(GDN_QKV), blk(GDN_OUT), blk(LANES),
                  full(conv_w), full(cb2), full(dtb), full(aneg), full(nw2)],
        out_specs=blk(GDN_OUT),
        out_shape=jax.ShapeDtypeStruct((bsz, seq, GDN_OUT), BF16),
        scratch_shapes=[pltpu.VMEM((bsz, rows + SUBLANES, GDN_QKV), F32),
                        pltpu.VMEM((bsz, SUBLANES, GDN_QKV), F32),
                        pltpu.VMEM((bsz * (GDN_HEADS // 2), 2 * GDN_DK, LANES), F32)],
        compiler_params=_cparams(("arbitrary",)),
        name="gdn_mixer",
    )(qkv.reshape(bsz, seq, -1), z.reshape(bsz, seq, -1), misc.reshape(bsz, seq, -1),
      conv_w, cb2, dtb, aneg, nw2)
    return out.reshape(bsz * seq, GDN_OUT)


ROUTE_E0, ROUTE_E1, ROUTE_R0, ROUTE_R1, ROUTE_W0, ROUTE_W1 = range(6)


def _post_kernel(*refs, route):
    if route:
        (x_ref, a_ref, s_ref, g_ref, wa_ref, ws_ref, wg_ref, gate_ref, sh_ref, sc_ref, wr_ref,
         x1_ref, h_ref, route_ref, cnt_ref, carry_ref) = refs
    else:
        (x_ref, a_ref, s_ref, g_ref, wa_ref, ws_ref, wg_ref, gate_ref, sh_ref, sc_ref,
         x1_ref, h_ref) = refs
    mix = (lax.dot_general(a_ref[0], wa_ref[...], (((0,), (0,)), ((), ())), preferred_element_type=F32)
           + jnp.dot(s_ref[...], ws_ref[...], preferred_element_type=F32)
           + jnp.dot(g_ref[...], wg_ref[...], preferred_element_type=F32))
    x1 = x_ref[...] + gate_ref[0] * mix
    x1_ref[...] = x1
    h = _rms(x1) * (1.0 + sc_ref[0]) + sh_ref[0]
    h_ref[...] = h.astype(h_ref.dtype)
    if not route:
        return

    @pl.when(pl.program_id(0) == 0)
    def _zero_carry():
        carry_ref[...] = jnp.zeros(carry_ref.shape, F32)

    tm = h.shape[0]
    lane = _lane_iota((tm, LANES))
    logits = jnp.where(lane < N_EXPERTS, _dot_f32(h, wr_ref[...]), -jnp.inf)
    lane_f = lane.astype(F32)
    m0 = jnp.max(logits, axis=1, keepdims=True)
    e0 = jnp.min(jnp.where(logits == m0, lane_f, float(LANES)), axis=1, keepdims=True)
    rest = jnp.where(lane_f == e0, -jnp.inf, logits)
    m1 = jnp.max(rest, axis=1, keepdims=True)
    e1 = jnp.min(jnp.where(rest == m1, lane_f, float(LANES)), axis=1, keepdims=True)
    ex = jnp.exp(m1 - m0)
    w0 = 1.0 / (1.0 + ex)
    w1 = ex / (1.0 + ex)
    oh0 = (lane_f == e0).astype(F32)
    oh1 = (lane_f == e1).astype(F32)
    both = oh0 + oh1
    strict = (_lane_iota((tm, tm)) < _row_iota((tm, tm))).astype(BF16)
    before = jnp.dot(strict, both.astype(BF16), preferred_element_type=F32) + carry_ref[0:1, :]
    r0 = jnp.sum(before * oh0, axis=1, keepdims=True)
    r1 = jnp.sum(before * oh1, axis=1, keepdims=True)
    carry_ref[0:1, :] = carry_ref[0:1, :] + jnp.sum(both, axis=0, keepdims=True)
    slab = jnp.zeros((tm, LANES), F32)
    for pos, val in ((ROUTE_E0, e0), (ROUTE_E1, e1), (ROUTE_R0, r0), (ROUTE_R1, r1),
                     (ROUTE_W0, w0), (ROUTE_W1, w1)):
        slab = jnp.where(lane == pos, val, slab)
    route_ref[...] = slab
    cnt_ref[...] = carry_ref[...]


def _post_mixer(x2d, mla_o, ssd_y, gdn_o, w_out, gate, shift, scale, seq, w_router=None):
    t, d = x2d.shape
    route = w_router is not None
    tm = min(ROUTE_TILE if route else TM_PROJ, seq)
    per_b = seq // tm
    wa = w_out[:MLA_OUT].astype(BF16)
    ws = w_out[MLA_OUT:MLA_OUT + SSD_D_INNER].astype(BF16)
    wg = w_out[MLA_OUT + SSD_D_INNER:].astype(BF16)
    row = lambda w: pl.BlockSpec((tm, w), lambda i: (i, 0))
    full = lambda a: pl.BlockSpec(a.shape, lambda i: (0,) * a.ndim)
    mod = pl.BlockSpec((1, 1, d), lambda i: (i // per_b, 0, 0))
    att = pl.BlockSpec((1, MLA_OUT, tm), lambda i: (i // per_b, 0, i % per_b))
    in_specs = [row(d), att, row(SSD_D_INNER), row(GDN_OUT), full(wa), full(ws), full(wg),
                mod, mod, mod]
    args = [x2d, mla_o, ssd_y, gdn_o, wa, ws, wg, gate, shift, scale]
    out_specs = [row(d), row(d)]
    out_shape = [jax.ShapeDtypeStruct((t, d), F32), jax.ShapeDtypeStruct((t, d), F32 if route else BF16)]
    scratch = []
    if route:
        wr = jnp.zeros((d, LANES), F32).at[:, :N_EXPERTS].set(w_router.astype(F32))
        in_specs.append(full(wr))
        args.append(wr)
        out_specs += [row(LANES), pl.BlockSpec((SUBLANES, LANES), lambda i: (0, 0))]
        out_shape += [jax.ShapeDtypeStruct((t, LANES), F32), jax.ShapeDtypeStruct((SUBLANES, LANES), F32)]
        scratch = [pltpu.VMEM((SUBLANES, LANES), F32)]
    return pl.pallas_call(
        functools.partial(_post_kernel, route=route),
        grid=(t // tm,),
        in_specs=in_specs, out_specs=out_specs, out_shape=out_shape, scratch_shapes=scratch,
        compiler_params=_cparams(("arbitrary",) if route else ("parallel",)),
        name="post_mixer_route" if route else "post_mixer",
    )(*args)


def _finish(x, final, fn_ref):
    return _rms(x) * fn_ref[...] if final else x


def _ffn_kernel(h_ref, x1_ref, gate_ref, fn_ref, wg_ref, wu_ref, wd_ref, o_ref, *, final):
    f = pl.program_id(1)
    h = h_ref[...]
    act = _silu(jnp.dot(h, wg_ref[...], preferred_element_type=F32)) * jnp.dot(
        h, wu_ref[...], preferred_element_type=F32)
    part = jnp.dot(act.astype(BF16), wd_ref[...], preferred_element_type=F32)

    @pl.when(f == 0)
    def _set():
        o_ref[...] = part

    @pl.when(f > 0)
    def _add():
        o_ref[...] += part

    @pl.when(f == pl.num_programs(1) - 1)
    def _residual():
        o_ref[...] = _finish(x1_ref[...] + gate_ref[0] * o_ref[...], final, fn_ref)


def _ffn_tile(dff):
    for cand in (1408, 1024, 512, 256, 128):
        if dff % cand == 0:
            return cand
    raise ValueError(f"unsupported d_ff {dff}")


def _dense_ffn(h, x1, gate, w_gate, w_up, w_down, final_norm, final, seq):
    t, d = x1.shape
    dff = w_gate.shape[1]
    tm = min(TM_FFN, seq)
    tf = _ffn_tile(dff)
    per_b = seq // tm
    fn = final_norm.reshape(1, d).astype(F32)
    return pl.pallas_call(
        functools.partial(_ffn_kernel, final=final),
        grid=(t // tm, dff // tf),
        in_specs=[
            pl.BlockSpec((tm, d), lambda i, f: (i, 0)),
            pl.BlockSpec((tm, d), lambda i, f: (i, 0)),
            pl.BlockSpec((1, 1, d), lambda i, f: (i // per_b, 0, 0)),
            pl.BlockSpec((1, d), lambda i, f: (0, 0)),
            pl.BlockSpec((d, tf), lambda i, f: (0, f)),
            pl.BlockSpec((d, tf), lambda i, f: (0, f)),
            pl.BlockSpec((tf, d), lambda i, f: (f, 0)),
        ],
        out_specs=pl.BlockSpec((tm, d), lambda i, f: (i, 0)),
        out_shape=jax.ShapeDtypeStruct((t, d), F32),
        compiler_params=_cparams(("parallel", "arbitrary")),
        name="dense_ffn",
    )(h, x1, gate, fn, w_gate.astype(BF16), w_up.astype(BF16), w_down.astype(BF16))


def _row_copy(src, dst, sem):
    return pltpu.make_async_copy(src, dst, sem)


def _dispatch_kernel(dest_ref, h_ref, zeros_hbm, xs_hbm, sem, *, tm):
    del zeros_hbm

    def issue(t, carry):
        for kk in range(TOP_K):
            d = dest_ref[0, 0, TOP_K * t + kk]
            _row_copy(h_ref.at[pl.ds(t, 1), :], xs_hbm.at[pl.ds(d, 1), :], sem).start(priority=kk)
        return carry

    lax.fori_loop(0, tm, issue, 0)

    def drain(t, carry):
        for kk in range(TOP_K):
            _row_copy(h_ref.at[pl.ds(0, 1), :], xs_hbm.at[pl.ds(0, 1), :], sem).wait()
        return carry

    lax.fori_loop(0, tm, drain, 0)


def _moe_dispatch(h_f32, dest, n_rows):
    t, d = h_f32.shape
    tm = min(MOE_TOK_TILE, t)
    dest3 = dest.reshape(t // tm, 1, TOP_K * tm)
    return pl.pallas_call(
        functools.partial(_dispatch_kernel, tm=tm),
        grid=(t // tm,),
        in_specs=[
            pl.BlockSpec((1, 1, TOP_K * tm), lambda i: (i, 0, 0), memory_space=pltpu.SMEM),
            pl.BlockSpec((tm, d), lambda i: (i, 0)),
            pl.BlockSpec(memory_space=pl.ANY),
        ],
        out_specs=pl.BlockSpec(memory_space=pl.ANY),
        out_shape=jax.ShapeDtypeStruct((n_rows, d), F32),
        scratch_shapes=[pltpu.SemaphoreType.DMA(())],
        input_output_aliases={2: 0},
        compiler_params=_cparams(("arbitrary",)),
        name="moe_dispatch",
    )(dest3, h_f32, jnp.zeros((n_rows, d), F32))


def _experts_kernel(be_ref, nused_ref, x_ref, wg_ref, wu_ref, wd_ref, y_ref):
    i = pl.program_id(0)
    f = pl.program_id(1)

    @pl.when(f == 0)
    def _zero():
        y_ref[...] = jnp.zeros(y_ref.shape, F32)

    @pl.when(i < nused_ref[0])
    def _compute():
        xb = x_ref[...].astype(BF16)
        act = _silu(jnp.dot(xb, wg_ref[0], preferred_element_type=F32)) * jnp.dot(
            xb, wu_ref[0], preferred_element_type=F32)
        y_ref[...] += jnp.dot(act.astype(BF16), wd_ref[0], preferred_element_type=F32)


def _moe_experts(xs, block_expert, n_used, w_gate, w_up, w_down):
    n_rows, d = xs.shape
    dff = w_gate.shape[2]
    bm = MOE_ROWS
    tf = MOE_FF_TILE if dff % MOE_FF_TILE == 0 else dff
    return pl.pallas_call(
        _experts_kernel,
        grid_spec=pltpu.PrefetchScalarGridSpec(
            num_scalar_prefetch=2,
            grid=(n_rows // bm, dff // tf),
            in_specs=[
                pl.BlockSpec((bm, d), lambda i, f, be, nu: (i, 0)),
                pl.BlockSpec((1, d, tf), lambda i, f, be, nu: (be[i], 0, f)),
                pl.BlockSpec((1, d, tf), lambda i, f, be, nu: (be[i], 0, f)),
                pl.BlockSpec((1, tf, d), lambda i, f, be, nu: (be[i], f, 0)),
            ],
            out_specs=pl.BlockSpec((bm, d), lambda i, f, be, nu: (i, 0)),
        ),
        out_shape=jax.ShapeDtypeStruct((n_rows, d), F32),
        compiler_params=_cparams(("parallel", "arbitrary")),
        name="moe_experts",
    )(block_expert, n_used, xs, w_gate.astype(BF16), w_up.astype(BF16), w_down.astype(BF16))


def _combine_kernel(dest_ref, x1_ref, route_ref, gate_ref, fn_ref, ys_hbm, o_ref, buf_ref, sem,
                    *, tm, final):
    def issue(t, carry):
        for kk in range(TOP_K):
            d = dest_ref[0, 0, TOP_K * t + kk]
            _row_copy(ys_hbm.at[pl.ds(d, 1), :], buf_ref.at[kk, pl.ds(t, 1), :], sem).start(priority=kk)
        return carry

    lax.fori_loop(0, tm, issue, 0)

    def drain(t, carry):
        for kk in range(TOP_K):
            _row_copy(ys_hbm.at[pl.ds(0, 1), :], buf_ref.at[kk, pl.ds(0, 1), :], sem).wait()
        return carry

    lax.fori_loop(0, tm, drain, 0)
    route = route_ref[...]
    w0 = route[:, ROUTE_W0:ROUTE_W0 + 1]
    w1 = route[:, ROUTE_W1:ROUTE_W1 + 1]
    f = w0 * buf_ref[0] + w1 * buf_ref[1]
    o_ref[...] = _finish(x1_ref[...] + gate_ref[0] * f, final, fn_ref)


def _moe_combine(ys, dest, x1, route, gate, final_norm, final, seq):
    t, d = x1.shape
    tm = min(MOE_TOK_TILE, seq)
    per_b = seq // tm
    dest3 = dest.reshape(t // tm, 1, TOP_K * tm)
    fn = final_norm.reshape(1, d).astype(F32)
    return pl.pallas_call(
        functools.partial(_combine_kernel, tm=tm, final=final),
        grid=(t // tm,),
        in_specs=[
            pl.BlockSpec((1, 1, TOP_K * tm), lambda i: (i, 0, 0), memory_space=pltpu.SMEM),
            pl.BlockSpec((tm, d), lambda i: (i, 0)),
            pl.BlockSpec((tm, LANES), lambda i: (i, 0)),
            pl.BlockSpec((1, 1, d), lambda i: (i // per_b, 0, 0)),
            pl.BlockSpec((1, d), lambda i: (0, 0)),
            pl.BlockSpec(memory_space=pl.ANY),
        ],
        out_specs=pl.BlockSpec((tm, d), lambda i: (i, 0)),
        out_shape=jax.ShapeDtypeStruct((t, d), F32),
        scratch_shapes=[pltpu.VMEM((TOP_K, tm, d), F32), pltpu.SemaphoreType.DMA(())],
        compiler_params=_cparams(("arbitrary",)),
        name="moe_combine",
    )(dest3, x1, route, gate, fn, ys)


def _moe_ffn(h_f32, x1, route, counts_slab, gate, w_gate, w_up, w_down, final_norm, final, seq):
    t, d = x1.shape
    bm = MOE_ROWS
    n_blocks = (t * TOP_K + bm - 1) // bm + N_EXPERTS
    n_rows = n_blocks * bm
    counts = counts_slab[0, :N_EXPERTS].astype(I32)
    padded = ((counts + bm - 1) // bm) * bm
    pend = jnp.cumsum(padded)
    pstart = pend - padded
    e = route[:, ROUTE_E0:ROUTE_E1 + 1].astype(I32)
    rank = route[:, ROUTE_R0:ROUTE_R1 + 1].astype(I32)
    dest = (jnp.sum(jnp.where(e[..., None] == jnp.arange(N_EXPERTS), pstart, 0), axis=-1) + rank)
    blk_start = jnp.arange(n_blocks, dtype=I32) * bm
    block_expert = jnp.minimum(jnp.sum(blk_start[:, None] >= pend[None, :], axis=1), N_EXPERTS - 1).astype(I32)
    n_used = (pend[-1:] // bm).astype(I32)
    xs = _moe_dispatch(h_f32, dest, n_rows)
    ys = _moe_experts(xs, block_expert, n_used, w_gate, w_up, w_down)
    return _moe_combine(ys, dest, x1, route, gate, final_norm, final, seq)


def kernel(x, c, ada_w, ada_b, w_in, mla_q_norm, mla_w_uq, mla_kv_norm, mla_w_ukv, ssd_conv_w, ssd_conv_b, ssd_dt_bias, ssd_a_log, ssd_d, ssd_norm, gdn_conv_w, gdn_conv_b, gdn_dt_bias, gdn_a_log, gdn_norm, w_out, ffn_w_gate, ffn_w_up, ffn_w_down, moe_router, moe_w_gate, moe_w_up, moe_w_down, final_norm):
    bsz, seq, d = x.shape
    depth = w_in.shape[0]
    t = bsz * seq
    mod = _ada_modulation(c, ada_w, ada_b)
    tables = _rope_tables(seq)
    xf = x.reshape(t, d)
    for l in range(depth):
        sh1, sc1, g1, sh2, sc2, g2 = [m.reshape(bsz, 1, d) for m in jnp.split(mod[l], 6, axis=-1)]
        mla_c, misc, ssd_z, ssd_xbc, gdn_qkv, gdn_z = _in_proj(xf, sh1, sc1, _prep_w_in(w_in[l]), seq)
        wqa, wqb, wkv = _prep_mla_weights(mla_w_uq[l], mla_w_ukv[l])
        qt, k, vt = _mla_proj(mla_c, misc, tables, mla_q_norm[l], mla_kv_norm[l], wqa, wqb, wkv, bsz, seq)
        mla_o = _flash_attention(qt, k, vt, bsz, seq)
        ssd_y = _ssd_mixer(ssd_z, ssd_xbc, misc, ssd_conv_w[l], ssd_conv_b[l], ssd_dt_bias[l],
                           ssd_a_log[l], ssd_d[l], ssd_norm[l], bsz, seq)
        gdn_o = _gdn_mixer(gdn_qkv, gdn_z, misc, gdn_conv_w[l], gdn_conv_b[l], gdn_dt_bias[l],
                           gdn_a_log[l], gdn_norm[l], bsz, seq)
        final = l == depth - 1
        if l % 2 == 0:
            x1, h2 = _post_mixer(xf, mla_o, ssd_y, gdn_o, w_out[l], g1, sh2, sc2, seq)
            xf = _dense_ffn(h2, x1, g2, ffn_w_gate[l // 2], ffn_w_up[l // 2], ffn_w_down[l // 2],
                            final_norm, final, seq)
        else:
            x1, h2, route, counts = _post_mixer(xf, mla_o, ssd_y, gdn_o, w_out[l], g1, sh2, sc2, seq,
                                                w_router=moe_router[l // 2])
            xf = _moe_ffn(h2, x1, route, counts, g2, moe_w_gate[l // 2], moe_w_up[l // 2],
                          moe_w_down[l // 2], final_norm, final, seq)
    return xf.reshape(bsz, seq, d)
```

```python
import functools
import math

import jax
import jax.numpy as jnp
import numpy as np
from jax import lax
from jax.experimental import pallas as pl
from jax.experimental.pallas import tpu as pltpu

F32 = jnp.float32
BF16 = jnp.bfloat16
I32 = jnp.int32
HIGHEST = lax.Precision.HIGHEST

D_MODEL = 1024
EPS = 1e-6
CONV_WIDTH = 4
MLA_HEADS, MLA_Q_RANK, MLA_KV_RANK = 4, 256, 128
MLA_NOPE, MLA_ROPE, MLA_V = 64, 32, 64
ROPE_THETA = 10000.0
SSD_HEADS, SSD_HEAD_DIM, SSD_GROUPS, SSD_STATE, SSD_CHUNK = 8, 64, 2, 64, 128
SSD_D_INNER = SSD_HEADS * SSD_HEAD_DIM
SSD_CONV_DIM = SSD_D_INNER + 2 * SSD_GROUPS * SSD_STATE
GDN_HEADS, GDN_DK, GDN_DV, GDN_CHUNK = 4, 64, 64, 64
GDN_QKV = GDN_HEADS * (2 * GDN_DK + GDN_DV)
GDN_OUT = GDN_HEADS * GDN_DV
MLA_IN = MLA_Q_RANK + MLA_KV_RANK + MLA_ROPE
MLA_OUT = MLA_HEADS * MLA_V
SSD_IN = SSD_D_INNER + SSD_CONV_DIM + SSD_HEADS
GDN_IN = GDN_QKV + GDN_HEADS * GDN_DV + 2 * GDN_HEADS
N_EXPERTS, TOP_K = 8, 2

LANES = 128
SUBLANES = 8
HALF = LANES // 2
VMEM_LIMIT_BYTES = 56 * 1024 * 1024

MISC_DT = 0
MISC_B = 8
MISC_A = 12
MISC_KR = 64
MISC_KR_SW = 96
HEAD_PAD = 128

TM_PROJ = 512
TM_FFN = 512
ATT_BLOCK = 1024
ATT_Q_COLS = 256
ATT_K_ROWS = 512
ATT_PIPELINE = 3
SSD_STEP_CHUNKS = 2
GDN_STEP_CHUNKS = 8
MOE_ROWS = 512
MOE_FF_TILE = 1792
MOE_TOK_TILE = 512
ROUTE_TILE = 512
ROW_DMA_UNROLL = 8


def _cparams(sem):
    return pltpu.CompilerParams(dimension_semantics=sem, vmem_limit_bytes=VMEM_LIMIT_BYTES)


def _lane_iota(shape):
    return lax.broadcasted_iota(I32, shape, len(shape) - 1)


def _row_iota(shape):
    return lax.broadcasted_iota(I32, shape, len(shape) - 2)


def _softplus(x):
    return jnp.maximum(x, 0.0) + jnp.log1p(jnp.exp(-jnp.abs(x)))


def _silu(x):
    return x * jax.nn.sigmoid(x)


def _rms(x):
    return x * lax.rsqrt(jnp.mean(x * x, axis=-1, keepdims=True) + EPS)


def _dot(a, b):
    return jnp.dot(a.astype(BF16), b.astype(BF16), preferred_element_type=F32)


def _dot_nt(a, b):
    return lax.dot_general(a.astype(BF16), b.astype(BF16), (((1,), (1,)), ((), ())),
                           preferred_element_type=F32)


def _dot_tn(a, b):
    return lax.dot_general(a.astype(BF16), b.astype(BF16), (((0,), (0,)), ((), ())),
                           preferred_element_type=F32)


def _dot_f32(a, b):
    return jnp.dot(a, b, precision=HIGHEST, preferred_element_type=F32)


def _ada_kernel(c_ref, w_ref, b_ref, o_ref):
    c_act = _silu(c_ref[...])
    o_ref[0] = _dot_f32(c_act, w_ref[0]) + b_ref[0]


def _ada_modulation(c, ada_w, ada_b):
    depth, d, n = ada_w.shape
    bsz = c.shape[0]
    rows = max(SUBLANES, bsz)
    c_pad = jnp.zeros((rows, d), F32).at[:bsz].set(c)
    tn = 1536
    out = pl.pallas_call(
        _ada_kernel,
        grid=(depth, n // tn),
        in_specs=[
            pl.BlockSpec((rows, d), lambda l, j: (0, 0)),
            pl.BlockSpec((1, d, tn), lambda l, j: (l, 0, j)),
            pl.BlockSpec((1, 1, tn), lambda l, j: (l, 0, j)),
        ],
        out_specs=pl.BlockSpec((1, rows, tn), lambda l, j: (l, 0, j)),
        out_shape=jax.ShapeDtypeStruct((depth, rows, n), F32),
        compiler_params=_cparams(("parallel", "parallel")),
        name="ada_modulation",
    )(c_pad, ada_w, ada_b.reshape(depth, 1, n))
    return out[:, :bsz]


IN_SLABS = (("mla_c", 384), ("misc", 128), ("ssd_z", 512), ("ssd_xbc", 768),
            ("gdn_qkv", 768), ("gdn_z", 256))
IN_WIDTH = sum(w for _, w in IN_SLABS)


def _prep_w_in(w):
    d = w.shape[0]
    o_ssd = MLA_IN
    o_gdn = MLA_IN + SSD_IN
    w_kr = w[:, MLA_Q_RANK + MLA_KV_RANK:MLA_IN]
    half = MLA_ROPE // 2
    w_kr_sw = jnp.concatenate([-w_kr[:, half:], w_kr[:, :half]], axis=1)
    ssd_dt = w[:, o_ssd + SSD_D_INNER + SSD_CONV_DIM:o_ssd + SSD_IN]
    gdn_ba = w[:, o_gdn + GDN_QKV + GDN_OUT:o_gdn + GDN_IN]
    misc = jnp.concatenate(
        [ssd_dt, gdn_ba, jnp.zeros((d, MISC_KR - MISC_A - GDN_HEADS), w.dtype), w_kr, w_kr_sw], axis=1)
    cols = [
        w[:, :MLA_Q_RANK + MLA_KV_RANK], misc,
        w[:, o_ssd:o_ssd + SSD_D_INNER],
        w[:, o_ssd + SSD_D_INNER:o_ssd + SSD_D_INNER + SSD_CONV_DIM],
        w[:, o_gdn:o_gdn + GDN_QKV],
        w[:, o_gdn + GDN_QKV:o_gdn + GDN_QKV + GDN_OUT],
    ]
    return jnp.concatenate(cols, axis=1).astype(BF16)


def _in_proj_kernel(x_ref, sh_ref, sc_ref, w_ref, *o_refs):
    h = _rms(x_ref[...]) * (1.0 + sc_ref[0]) + sh_ref[0]
    hb = h.astype(BF16)
    off = 0
    for o_ref, (_, width) in zip(o_refs, IN_SLABS):
        o_ref[...] = jnp.dot(hb, w_ref[:, off:off + width], preferred_element_type=F32)
        off += width


def _in_proj(x2d, shift, scale, w_perm, seq):
    t, d = x2d.shape
    tm = min(TM_PROJ, seq)
    per_b = seq // tm
    return pl.pallas_call(
        _in_proj_kernel,
        grid=(t // tm,),
        in_specs=[
            pl.BlockSpec((tm, d), lambda i: (i, 0)),
            pl.BlockSpec((1, 1, d), lambda i: (i // per_b, 0, 0)),
            pl.BlockSpec((1, 1, d), lambda i: (i // per_b, 0, 0)),
            pl.BlockSpec((d, IN_WIDTH), lambda i: (0, 0)),
        ],
        out_specs=[pl.BlockSpec((tm, w), lambda i: (i, 0)) for _, w in IN_SLABS],
        out_shape=[jax.ShapeDtypeStruct((t, w), F32) for _, w in IN_SLABS],
        compiler_params=_cparams(("parallel",)),
        name="in_proj",
    )(x2d, shift, scale, w_perm)


def _rope_tables(seq):
    pos = jnp.arange(seq, dtype=F32)
    inv_freq = ROPE_THETA ** (-jnp.arange(0, MLA_ROPE, 2, dtype=F32) / MLA_ROPE)
    ang = pos[:, None] * inv_freq[None, :]
    cos, sin = jnp.cos(ang), jnp.sin(ang)
    zeros_l = jnp.zeros((seq, MLA_NOPE), F32)
    zeros_r = jnp.zeros((seq, HEAD_PAD - MLA_NOPE - MLA_ROPE), F32)
    cos_k = jnp.concatenate([zeros_l, cos, cos, zeros_r], axis=1)
    sin_k = jnp.concatenate([zeros_l, sin, sin, zeros_r], axis=1)
    scale = (MLA_NOPE + MLA_ROPE) ** -0.5 * math.log2(math.e)
    cos_q = scale * jnp.concatenate([jnp.ones((seq, MLA_NOPE), F32), cos, cos, zeros_r], axis=1)
    sin_q = scale * sin_k
    return cos_q, sin_q, cos_k, sin_k


def _prep_mla_weights(w_uq, w_ukv):
    r = w_uq.shape[0]
    hq = MLA_NOPE + MLA_ROPE
    half = MLA_ROPE // 2
    qa, qb = [], []
    for h in range(MLA_HEADS):
        nope = w_uq[:, h * hq:h * hq + MLA_NOPE]
        rope = w_uq[:, h * hq + MLA_NOPE:(h + 1) * hq]
        rope_sw = jnp.concatenate([-rope[:, half:], rope[:, :half]], axis=1)
        pad = jnp.zeros((r, HEAD_PAD - hq), w_uq.dtype)
        qa += [nope, rope, pad]
        qb += [jnp.zeros((r, MLA_NOPE), w_uq.dtype), rope_sw, pad]
    rk = w_ukv.shape[0]
    hk = MLA_NOPE + MLA_V
    kcols, vcols = [], []
    for h in range(MLA_HEADS):
        kcols += [w_ukv[:, h * hk:h * hk + MLA_NOPE], jnp.zeros((rk, HEAD_PAD - MLA_NOPE), w_ukv.dtype)]
        vcols += [w_ukv[:, h * hk + MLA_NOPE:(h + 1) * hk], jnp.zeros((rk, HEAD_PAD - MLA_V), w_ukv.dtype)]
    return (jnp.concatenate(qa, axis=1).astype(BF16), jnp.concatenate(qb, axis=1).astype(BF16),
            jnp.concatenate(kcols + vcols, axis=1).astype(BF16))


def _mla_proj_kernel(c_ref, misc_ref, cq_ref, sq_ref, ck_ref, sk_ref, qn_ref, kvn_ref,
                     wqa_ref, wqb_ref, wkv_ref, qt_ref, k_ref, vt_ref):
    c = c_ref[...]
    cq = (_rms(c[:, :MLA_Q_RANK]) * qn_ref[...]).astype(BF16)
    ckv = (_rms(c[:, MLA_Q_RANK:]) * kvn_ref[...]).astype(BF16)
    qa = jnp.dot(cq, wqa_ref[...], preferred_element_type=F32)
    qb = jnp.dot(cq, wqb_ref[...], preferred_element_type=F32)
    cos_q = jnp.concatenate([cq_ref[...]] * MLA_HEADS, axis=1)
    sin_q = jnp.concatenate([sq_ref[...]] * MLA_HEADS, axis=1)
    qt_ref[0] = (qa * cos_q + qb * sin_q).T.astype(BF16)
    kv = jnp.dot(ckv, wkv_ref[...], preferred_element_type=F32)
    misc = misc_ref[...]
    kr = misc * ck_ref[...] + pltpu.roll(misc, LANES - (MISC_KR_SW - MISC_KR), 1) * sk_ref[...]
    kw = MLA_HEADS * HEAD_PAD
    k_ref[0] = (kv[:, :kw] + jnp.concatenate([kr] * MLA_HEADS, axis=1)).astype(BF16)
    v = kv[:, kw:]
    v = jnp.where(_lane_iota(v.shape) % HEAD_PAD == MLA_V, 1.0, v)
    vt_ref[0] = v.T.astype(BF16)


def _mla_proj(mla_c, misc, tables, q_norm, kv_norm, wqa, wqb, wkv, bsz, seq):
    t = mla_c.shape[0]
    tm = min(TM_PROJ, seq)
    per_b = seq // tm
    kw = MLA_HEADS * HEAD_PAD
    tab_spec = pl.BlockSpec((tm, HEAD_PAD), lambda i: (i % per_b, 0))
    full = lambda a: pl.BlockSpec(a.shape, lambda i: (0,) * a.ndim)
    qn = q_norm.reshape(1, -1)
    kvn = kv_norm.reshape(1, -1)
    t_spec = pl.BlockSpec((1, kw, tm), lambda i: (i // per_b, 0, i % per_b))
    return pl.pallas_call(
        _mla_proj_kernel,
        grid=(t // tm,),
        in_specs=[
            pl.BlockSpec((tm, mla_c.shape[1]), lambda i: (i, 0)),
            pl.BlockSpec((tm, LANES), lambda i: (i, 0)),
            tab_spec, tab_spec, tab_spec, tab_spec,
            full(qn), full(kvn), full(wqa), full(wqb), full(wkv),
        ],
        out_specs=[t_spec, pl.BlockSpec((1, tm, kw), lambda i: (i // per_b, i % per_b, 0)), t_spec],
        out_shape=[jax.ShapeDtypeStruct((bsz, kw, seq), BF16), jax.ShapeDtypeStruct((bsz, seq, kw), BF16),
                   jax.ShapeDtypeStruct((bsz, kw, seq), BF16)],
        compiler_params=_cparams(("parallel",)),
        name="mla_proj",
    )(mla_c, misc, *tables, qn, kvn, wqa, wqb, wkv)


def _flash_kernel(qi_tab, ki_tab, qt_ref, k_ref, vt_ref, o_ref, m_ref, acc_ref, *, blk, rq, ck):
    p = pl.program_id(1)
    qi = qi_tab[p]
    ki = ki_tab[p]

    @pl.when(ki == 0)
    def _init():
        m_ref[...] = jnp.full(m_ref.shape, -jnp.inf, F32)
        acc_ref[...] = jnp.zeros(acc_ref.shape, F32)

    def step(diag):
        units = []
        for rc in range(blk // rq):
            for kc in range(blk // ck):
                if diag and kc * ck > rc * rq + rq - 1:
                    continue
                for h in range(MLA_HEADS):
                    units.append((h, rc * rq, kc * ck))

        def scores(unit):
            h, c0, k0 = unit
            kk = k_ref[0, k0:k0 + ck, h * HEAD_PAD:(h + 1) * HEAD_PAD]
            qt = qt_ref[0, h * HEAD_PAD:(h + 1) * HEAD_PAD, c0:c0 + rq]
            s = jnp.dot(kk, qt, preferred_element_type=F32)
            if diag and k0 + ck - 1 > c0:
                key = lax.broadcasted_iota(I32, s.shape, 0) + k0
                qry = lax.broadcasted_iota(I32, s.shape, 1) + c0
                s = jnp.where(key <= qry, s, -jnp.inf)
            return s

        def update(unit, s):
            h, c0, k0 = unit
            m_prev = m_ref[h, :, c0:c0 + rq]
            m_new = jnp.maximum(m_prev, jnp.max(s, axis=0, keepdims=True))
            alpha = jnp.exp2(m_prev - m_new)
            pt = jnp.exp2(s - m_new).astype(BF16)
            m_ref[h, :, c0:c0 + rq] = m_new
            acc_ref[h, :, c0:c0 + rq] = acc_ref[h, :, c0:c0 + rq] * alpha + jnp.dot(
                vt_ref[0, h * HEAD_PAD:(h + 1) * HEAD_PAD, k0:k0 + ck], pt, preferred_element_type=F32)

        pending = [scores(u) for u in units[:ATT_PIPELINE]]
        for n, unit in enumerate(units):
            s = pending.pop(0)
            if n + ATT_PIPELINE < len(units):
                pending.append(scores(units[n + ATT_PIPELINE]))
            update(unit, s)

    @pl.when(ki < qi)
    def _off_diagonal():
        step(False)

    @pl.when(ki == qi)
    def _diagonal():
        step(True)
        for h in range(MLA_HEADS):
            acc = acc_ref[h]
            o_ref[0, h * MLA_V:(h + 1) * MLA_V, :] = (acc[:MLA_V] / acc[MLA_V:MLA_V + 1]).astype(o_ref.dtype)


def _flash_attention(qt, k, vt, bsz, seq):
    blk = min(ATT_BLOCK, seq)
    rq = min(ATT_Q_COLS, blk)
    ck = min(ATT_K_ROWS, blk)
    nb = seq // blk
    pairs = [(a, b) for a in range(nb) for b in range(a + 1)]
    qi_tab = jnp.asarray([a for a, _ in pairs], I32)
    ki_tab = jnp.asarray([b for _, b in pairs], I32)
    kw = MLA_HEADS * HEAD_PAD
    return pl.pallas_call(
        functools.partial(_flash_kernel, blk=blk, rq=rq, ck=ck),
        grid_spec=pltpu.PrefetchScalarGridSpec(
            num_scalar_prefetch=2,
            grid=(bsz, len(pairs)),
            in_specs=[
                pl.BlockSpec((1, kw, blk), lambda b, p, qt_, kt_: (b, 0, qt_[p])),
                pl.BlockSpec((1, blk, kw), lambda b, p, qt_, kt_: (b, kt_[p], 0)),
                pl.BlockSpec((1, kw, blk), lambda b, p, qt_, kt_: (b, 0, kt_[p])),
            ],
            out_specs=pl.BlockSpec((1, MLA_OUT, blk), lambda b, p, qt_, kt_: (b, 0, qt_[p])),
            scratch_shapes=[pltpu.VMEM((MLA_HEADS, 1, blk), F32),
                            pltpu.VMEM((MLA_HEADS, HEAD_PAD, blk), F32)],
        ),
        out_shape=jax.ShapeDtypeStruct((bsz, MLA_OUT, seq), BF16),
        compiler_params=_cparams(("parallel", "arbitrary")),
        name="mla_flash",
    )(qi_tab, ki_tab, qt, k, vt)


def _causal_conv_silu(cur, ext_ref, halo_ref, cw_ref, cb_ref, first):
    rows = cur.shape[0]

    @pl.when(first)
    def _zero_halo():
        halo_ref[...] = jnp.zeros(halo_ref.shape, F32)

    ext_ref[0:SUBLANES, :] = halo_ref[...]
    ext_ref[SUBLANES:SUBLANES + rows, :] = cur
    halo_ref[...] = cur[rows - SUBLANES:, :]
    acc = cb_ref[...] + cw_ref[CONV_WIDTH - 1:CONV_WIDTH, :] * cur
    for j in range(CONV_WIDTH - 1):
        start = SUBLANES - (CONV_WIDTH - 1) + j
        acc = acc + cw_ref[j:j + 1, :] * ext_ref[start:start + rows, :]
    return _silu(acc)


def _ssd_kernel(z_ref, xbc_ref, misc_ref, cw_ref, cb_ref, dtb_ref, aneg_ref, dsk_ref, nw_ref,
                y_ref, ext_ref, halo_ref, st_ref, *, nchunk):
    first = pl.program_id(1) == 0
    length = SSD_CHUNK
    xbc = _causal_conv_silu(xbc_ref[0], ext_ref, halo_ref, cw_ref, cb_ref, first)

    @pl.when(first)
    def _zero_state():
        st_ref[...] = jnp.zeros(st_ref.shape, F32)

    dt_all = _softplus(misc_ref[0] + dtb_ref[...])
    a_all = dt_all * aneg_ref[...]
    tri = (_lane_iota((length, length)) <= _row_iota((length, length))).astype(F32)
    lower = _lane_iota((length, length)) <= _row_iota((length, length))
    lane = _lane_iota((length, LANES))
    lo = lane < HALF
    bw = SSD_GROUPS * SSD_STATE

    for c in range(nchunk):
        r0 = c * length
        dt = dt_all[r0:r0 + length]
        acum = _dot_f32(tri, a_all[r0:r0 + length])
        acum_t = acum.T
        x_c = xbc[r0:r0 + length, :SSD_D_INNER]
        b_c = xbc[r0:r0 + length, SSD_D_INNER:SSD_D_INNER + bw]
        c_c = xbc[r0:r0 + length, SSD_D_INNER + bw:]
        y_parts = []
        for g in range(SSD_GROUPS):
            b_g = b_c[:, g * SSD_STATE:(g + 1) * SSD_STATE]
            c_g = c_c[:, g * SSD_STATE:(g + 1) * SSD_STATE].astype(BF16)
            b_gt = b_g.T.astype(BF16)
            cb = jnp.dot(c_g, b_gt, preferred_element_type=F32)
            pairs_per_group = SSD_HEADS // SSD_GROUPS // 2
            for pp in range(pairs_per_group):
                pr = g * pairs_per_group + pp
                h0, h1 = 2 * pr, 2 * pr + 1
                col0, col1 = acum[:, h0:h0 + 1], acum[:, h1:h1 + 1]
                l0 = jnp.exp(jnp.where(lower, col0 - acum_t[h0:h0 + 1, :], -jnp.inf))
                l1 = jnp.exp(jnp.where(lower, col1 - acum_t[h1:h1 + 1, :], -jnp.inf))
                mmat = jnp.concatenate([cb * l0, cb * l1], axis=1).astype(BF16)
                xp = x_c[:, pr * LANES:(pr + 1) * LANES]
                xdt = xp * jnp.where(lo, dt[:, h0:h0 + 1], dt[:, h1:h1 + 1])
                rhs = jnp.concatenate([jnp.where(lo, xdt, 0.0), jnp.where(lo, 0.0, xdt)], axis=0)
                y_diag = jnp.dot(mmat, rhs.astype(BF16), preferred_element_type=F32)
                col_pair = jnp.where(lo, col0, col1)
                last_pair = jnp.where(lo[:1], acum[length - 1:length, h0:h0 + 1],
                                      acum[length - 1:length, h1:h1 + 1])
                st = st_ref[pr]
                y_off = jnp.dot(c_g, st.astype(BF16), preferred_element_type=F32) * jnp.exp(col_pair)
                xdec = (xdt * jnp.exp(last_pair - col_pair)).astype(BF16)
                st_ref[pr] = st * jnp.exp(last_pair) + jnp.dot(b_gt, xdec, preferred_element_type=F32)
                y_parts.append(y_diag + y_off + xp * dsk_ref[:, pr * LANES:(pr + 1) * LANES])
        y = jnp.concatenate(y_parts, axis=1) * _silu(z_ref[0, r0:r0 + length, :])
        gw = SSD_D_INNER // SSD_GROUPS
        y = jnp.concatenate([_rms(y[:, g * gw:(g + 1) * gw]) for g in range(SSD_GROUPS)], axis=1)
        y_ref[0, r0:r0 + length, :] = (y * nw_ref[...]).astype(y_ref.dtype)


def _ssd_mixer(z, xbc, misc, conv_w, conv_b, dt_bias, a_log, d_skip, norm_w, bsz, seq):
    nchunk = min(SSD_STEP_CHUNKS, seq // SSD_CHUNK)
    rows = nchunk * SSD_CHUNK
    pad = jnp.zeros((LANES - SSD_HEADS,), F32)
    dtb = jnp.concatenate([dt_bias.astype(F32), pad]).reshape(1, LANES)
    aneg = jnp.concatenate([-jnp.exp(a_log.astype(F32)), pad]).reshape(1, LANES)
    dsk = jnp.repeat(d_skip.astype(F32), SSD_HEAD_DIM).reshape(1, SSD_D_INNER)
    full = lambda a: pl.BlockSpec(a.shape, lambda b, i: (0,) * a.ndim)
    cb2 = conv_b.reshape(1, -1)
    nw2 = norm_w.reshape(1, -1)
    blk = lambda w: pl.BlockSpec((1, rows, w), lambda b, i: (b, i, 0))
    out = pl.pallas_call(
        functools.partial(_ssd_kernel, nchunk=nchunk),
        grid=(bsz, seq // rows),
        in_specs=[blk(SSD_D_INNER), blk(SSD_CONV_DIM), blk(LANES),
                  full(conv_w), full(cb2), full(dtb), full(aneg), full(dsk), full(nw2)],
        out_specs=blk(SSD_D_INNER),
        out_shape=jax.ShapeDtypeStruct((bsz, seq, SSD_D_INNER), BF16),
        scratch_shapes=[pltpu.VMEM((rows + SUBLANES, SSD_CONV_DIM), F32),
                        pltpu.VMEM((SUBLANES, SSD_CONV_DIM), F32),
                        pltpu.VMEM((SSD_HEADS // 2, SSD_STATE, LANES), F32)],
        compiler_params=_cparams(("parallel", "arbitrary")),
        name="ssd_mixer",
    )(z.reshape(bsz, seq, -1), xbc.reshape(bsz, seq, -1), misc.reshape(bsz, seq, -1),
      conv_w, cb2, dtb, aneg, dsk, nw2)
    return out.reshape(bsz * seq, SSD_D_INNER)


def _half_sum(x, lo):
    s_lo = jnp.sum(jnp.where(lo, x, 0.0), axis=1, keepdims=True)
    s_hi = jnp.sum(jnp.where(lo, 0.0, x), axis=1, keepdims=True)
    return jnp.where(lo, s_lo, s_hi)


def _gdn_kernel(qkv_ref, z_ref, misc_ref, cw_ref, cb_ref, dtb_ref, aneg_ref, nw_ref,
                o_ref, ext_ref, halo_ref, st_ref, *, nchunk, bsz):
    first = pl.program_id(0) == 0
    length = GDN_CHUNK
    two = 2 * length

    @pl.when(first)
    def _zero_state():
        st_ref[...] = jnp.zeros(st_ref.shape, F32)

    tri = (_lane_iota((length, length)) <= _row_iota((length, length))).astype(F32)
    lane = _lane_iota((length, LANES))
    lo = lane < HALF
    r2 = _row_iota((two, two))
    c2 = _lane_iota((two, two))
    same_blk = (r2 < length) == (c2 < length)
    low_incl = same_blk & (c2 <= r2)
    low_strict = same_blk & (c2 < r2)
    eye = (r2 == c2).astype(F32)
    top_lo = (r2 < length) == (c2 < HALF)
    hk = GDN_HEADS * GDN_DK
    qscale = GDN_DK ** -0.5
    npair = GDN_HEADS // 2

    def blockdiag(slab):
        return jnp.where(top_lo, jnp.concatenate([slab, slab], axis=0), 0.0)

    def fold(bd):
        return bd[:length] + bd[length:]

    chains = []
    for b in range(bsz):
        qkv = _causal_conv_silu(qkv_ref[b], ext_ref.at[b], halo_ref.at[b], cw_ref, cb_ref, first)
        misc = misc_ref[b]
        beta_all = jax.nn.sigmoid(misc)
        g_all = aneg_ref[...] * _softplus(misc + dtb_ref[...])
        for c in range(nchunk):
            r0 = c * length
            gcum = _dot_f32(tri, g_all[r0:r0 + length])
            beta = beta_all[r0:r0 + length]
            for pr in range(npair):
                h0, h1 = 2 * pr, 2 * pr + 1
                q = qkv[r0:r0 + length, pr * LANES:(pr + 1) * LANES]
                k = qkv[r0:r0 + length, hk + pr * LANES:hk + (pr + 1) * LANES]
                v = qkv[r0:r0 + length, 2 * hk + pr * LANES:2 * hk + (pr + 1) * LANES]
                qn = q * lax.rsqrt(_half_sum(q * q, lo) + EPS) * qscale
                kn = k * lax.rsqrt(_half_sum(k * k, lo) + EPS)
                g0 = gcum[:, MISC_A + h0:MISC_A + h0 + 1]
                g1 = gcum[:, MISC_A + h1:MISC_A + h1 + 1]
                gexp = jnp.where(lo, g0, g1)
                bexp = jnp.where(lo, beta[:, MISC_B + h0:MISC_B + h0 + 1],
                                 beta[:, MISC_B + h1:MISC_B + h1 + 1])
                gcol = jnp.concatenate([jnp.broadcast_to(g0, (length, two)),
                                        jnp.broadcast_to(g1, (length, two))], axis=0)
                decay = jnp.exp(jnp.where(low_incl, gcol - gcol.T, -jnp.inf))
                kb = kn * bexp
                kk2 = jnp.concatenate([kn, kn], axis=0)
                a_mat = jnp.where(low_strict, _dot_nt(blockdiag(kb), kk2) * decay, 0.0)
                qk = jnp.where(low_incl, _dot_nt(blockdiag(qn), kk2) * decay, 0.0)
                g_last = gexp[length - 1:length, :]
                chains.append(dict(
                    b=b, c=c, pr=pr, pw=a_mat, t=eye - a_mat, qk=qk,
                    vb=v * bexp, kbg=kb * jnp.exp(gexp), q_dec=qn * jnp.exp(gexp),
                    k_tail=kn * jnp.exp(g_last - gexp), gl=jnp.exp(g_last)))

    for _ in range(int(math.log2(length)) - 1):
        for ch in chains:
            ch["pw"] = _dot(ch["pw"], ch["pw"])
        for ch in chains:
            ch["t"] = ch["t"] + _dot(ch["t"], ch["pw"])

    for ch in chains:
        ch["u"] = fold(_dot(ch["t"], blockdiag(ch["vb"])))
        ch["k_cum"] = fold(_dot(ch["t"], blockdiag(ch["kbg"])))
    for ch in chains:
        ch["m"] = jnp.where(top_lo, _dot_tn(ch["k_tail"], ch["k_cum"]), 0.0)
        ch["n"] = jnp.where(top_lo, _dot_tn(ch["k_tail"], ch["u"]), 0.0)
        ch["q_eff"] = ch["q_dec"] - fold(_dot(ch["qk"], blockdiag(ch["k_cum"])))
        ch["o_loc"] = fold(_dot(ch["qk"], blockdiag(ch["u"])))

    states = {(b, pr): st_ref[b * npair + pr] for b in range(bsz) for pr in range(npair)}
    for c in range(nchunk):
        for ch in chains:
            if ch["c"] != c:
                continue
            b, pr = ch["b"], ch["pr"]
            st = states[(b, pr)]
            o = _dot(ch["q_eff"], st) + ch["o_loc"]
            states[(b, pr)] = st * ch["gl"] - _dot(ch["m"], st) + ch["n"]
            r0 = c * length
            ms = _half_sum(o * o, lo) * (1.0 / GDN_DV)
            zz = z_ref[b, r0:r0 + length, pr * LANES:(pr + 1) * LANES]
            out = o * lax.rsqrt(ms + EPS) * nw_ref[...] * _silu(zz)
            o_ref[b, r0:r0 + length, pr * LANES:(pr + 1) * LANES] = out.astype(o_ref.dtype)
    for (b, pr), st in states.items():
        st_ref[b * npair + pr] = st


def _gdn_mixer(qkv, z, misc, conv_w, conv_b, dt_bias, a_log, norm_w, bsz, seq):
    nchunk = min(GDN_STEP_CHUNKS, seq // GDN_CHUNK)
    rows = nchunk * GDN_CHUNK
    dtb = jnp.zeros((1, LANES), F32).at[0, MISC_A:MISC_A + GDN_HEADS].set(dt_bias.astype(F32))
    aneg = jnp.zeros((1, LANES), F32).at[0, MISC_A:MISC_A + GDN_HEADS].set(-jnp.exp(a_log.astype(F32)))
    nw2 = jnp.concatenate([norm_w.astype(F32)] * 2).reshape(1, LANES)
    cb2 = conv_b.reshape(1, -1)
    full = lambda a: pl.BlockSpec(a.shape, lambda i: (0,) * a.ndim)
    blk = lambda w: pl.BlockSpec((bsz, rows, w), lambda i: (0, i, 0))
    out = pl.pallas_call(
        functools.partial(_gdn_kernel, nchunk=nchunk, bsz=bsz),
        grid=(seq // rows,),
        in_specs=[blk(GDN_QKV), blk(GDN_OUT), blk(LANES),
                  full(conv_w), full(cb2), full(dtb), full(aneg), full(nw2)],
        out_specs=blk(GDN_OUT),
        out_shape=jax.ShapeDtypeStruct((bsz, seq, GDN_OUT), BF16),
        scratch_shapes=[pltpu.VMEM((bsz, rows + SUBLANES, GDN_QKV), F32),
                        pltpu.VMEM((bsz, SUBLANES, GDN_QKV), F32),
                        pltpu.VMEM((bsz * (GDN_HEADS // 2), 2 * GDN_DK, LANES), F32)],
        compiler_params=_cparams(("arbitrary",)),
        name="gdn_mixer",
    )(qkv.reshape(bsz, seq, -1), z.reshape(bsz, seq, -1), misc.reshape(bsz, seq, -1),
      conv_w, cb2, dtb, aneg, nw2)
    return out.reshape(bsz * seq, GDN_OUT)


ROUTE_E0, ROUTE_E1, ROUTE_R0, ROUTE_R1, ROUTE_W0, ROUTE_W1 = range(6)


def _post_kernel(*refs, route):
    if route:
        (x_ref, a_ref, s_ref, g_ref, wa_ref, ws_ref, wg_ref, gate_ref, sh_ref, sc_ref, wrh_ref, wrl_ref,
         x1_ref, h_ref, route_ref, cnt_ref, carry_ref) = refs
    else:
        (x_ref, a_ref, s_ref, g_ref, wa_ref, ws_ref, wg_ref, gate_ref, sh_ref, sc_ref,
         x1_ref, h_ref) = refs
    mix = (lax.dot_general(a_ref[0], wa_ref[...], (((0,), (0,)), ((), ())), preferred_element_type=F32)
           + jnp.dot(s_ref[...], ws_ref[...], preferred_element_type=F32)
           + jnp.dot(g_ref[...], wg_ref[...], preferred_element_type=F32))
    x1 = x_ref[...] + gate_ref[0] * mix
    x1_ref[...] = x1
    h = _rms(x1) * (1.0 + sc_ref[0]) + sh_ref[0]
    h_ref[...] = h.astype(h_ref.dtype)
    if not route:
        return

    @pl.when(pl.program_id(0) == 0)
    def _zero_carry():
        carry_ref[...] = jnp.zeros(carry_ref.shape, F32)

    tm = h.shape[0]
    lane = _lane_iota((tm, LANES))
    h_hi = h.astype(BF16)
    h_lo = (h - h_hi.astype(F32)).astype(BF16)
    logits = (jnp.dot(h_hi, wrh_ref[...], preferred_element_type=F32)
              + (jnp.dot(h_lo, wrh_ref[...], preferred_element_type=F32)
                 + jnp.dot(h_hi, wrl_ref[...], preferred_element_type=F32)))
    logits = jnp.where(lane < N_EXPERTS, logits, -jnp.inf)
    lane_f = lane.astype(F32)
    m0 = jnp.max(logits, axis=1, keepdims=True)
    e0 = jnp.min(jnp.where(logits == m0, lane_f, float(LANES)), axis=1, keepdims=True)
    rest = jnp.where(lane_f == e0, -jnp.inf, logits)
    m1 = jnp.max(rest, axis=1, keepdims=True)
    e1 = jnp.min(jnp.where(rest == m1, lane_f, float(LANES)), axis=1, keepdims=True)
    ex = jnp.exp(m1 - m0)
    w0 = 1.0 / (1.0 + ex)
    w1 = ex / (1.0 + ex)
    oh0 = (lane_f == e0).astype(F32)
    oh1 = (lane_f == e1).astype(F32)
    both = oh0 + oh1
    strict = (_lane_iota((tm, tm)) < _row_iota((tm, tm))).astype(BF16)
    before = jnp.dot(strict, both.astype(BF16), preferred_element_type=F32) + carry_ref[0:1, :]
    r0 = jnp.sum(before * oh0, axis=1, keepdims=True)
    r1 = jnp.sum(before * oh1, axis=1, keepdims=True)
    carry_ref[0:1, :] = carry_ref[0:1, :] + jnp.sum(both, axis=0, keepdims=True)
    slab = jnp.zeros((tm, LANES), F32)
    for pos, val in ((ROUTE_E0, e0), (ROUTE_E1, e1), (ROUTE_R0, r0), (ROUTE_R1, r1),
                     (ROUTE_W0, w0), (ROUTE_W1, w1)):
        slab = jnp.where(lane == pos, val, slab)
    route_ref[...] = slab
    cnt_ref[...] = carry_ref[...]


def _post_mixer(x2d, mla_o, ssd_y, gdn_o, w_out, gate, shift, scale, seq, w_router=None):
    t, d = x2d.shape
    route = w_router is not None
    tm = min(ROUTE_TILE if route else TM_PROJ, seq)
    per_b = seq // tm
    wa = w_out[:MLA_OUT].astype(BF16)
    ws = w_out[MLA_OUT:MLA_OUT + SSD_D_INNER].astype(BF16)
    wg = w_out[MLA_OUT + SSD_D_INNER:].astype(BF16)
    row = lambda w: pl.BlockSpec((tm, w), lambda i: (i, 0))
    full = lambda a: pl.BlockSpec(a.shape, lambda i: (0,) * a.ndim)
    mod = pl.BlockSpec((1, 1, d), lambda i: (i // per_b, 0, 0))
    att = pl.BlockSpec((1, MLA_OUT, tm), lambda i: (i // per_b, 0, i % per_b))
    in_specs = [row(d), att, row(SSD_D_INNER), row(GDN_OUT), full(wa), full(ws), full(wg),
                mod, mod, mod]
    args = [x2d, mla_o, ssd_y, gdn_o, wa, ws, wg, gate, shift, scale]
    out_specs = [row(d), row(d)]
    out_shape = [jax.ShapeDtypeStruct((t, d), F32), jax.ShapeDtypeStruct((t, d), F32 if route else BF16)]
    scratch = []
    if route:
        wr = jnp.zeros((d, LANES), F32).at[:, :N_EXPERTS].set(w_router.astype(F32))
        wr_hi = wr.astype(BF16)
        wr_lo = (wr - wr_hi.astype(F32)).astype(BF16)
        in_specs += [full(wr_hi), full(wr_lo)]
        args += [wr_hi, wr_lo]
        out_specs += [row(LANES), pl.BlockSpec((SUBLANES, LANES), lambda i: (0, 0))]
        out_shape += [jax.ShapeDtypeStruct((t, LANES), F32), jax.ShapeDtypeStruct((SUBLANES, LANES), F32)]
        scratch = [pltpu.VMEM((SUBLANES, LANES), F32)]
    return pl.pallas_call(
        functools.partial(_post_kernel, route=route),
        grid=(t // tm,),
        in_specs=in_specs, out_specs=out_specs, out_shape=out_shape, scratch_shapes=scratch,
        compiler_params=_cparams(("arbitrary",) if route else ("parallel",)),
        name="post_mixer_route" if route else "post_mixer",
    )(*args)


def _finish(x, final, fn_ref):
    return _rms(x) * fn_ref[...] if final else x


def _ffn_kernel(h_ref, x1_ref, gate_ref, fn_ref, wg_ref, wu_ref, wd_ref, o_ref, *, final):
    f = pl.program_id(1)
    h = h_ref[...]
    act = _silu(jnp.dot(h, wg_ref[...], preferred_element_type=F32)) * jnp.dot(
        h, wu_ref[...], preferred_element_type=F32)
    part = jnp.dot(act.astype(BF16), wd_ref[...], preferred_element_type=F32)

    @pl.when(f == 0)
    def _set():
        o_ref[...] = part

    @pl.when(f > 0)
    def _add():
        o_ref[...] += part

    @pl.when(f == pl.num_programs(1) - 1)
    def _residual():
        o_ref[...] = _finish(x1_ref[...] + gate_ref[0] * o_ref[...], final, fn_ref)


def _ffn_tile(dff):
    for cand in (2816, 1408, 1024, 512, 256, 128):
        if dff % cand == 0:
            return cand
    raise ValueError(f"unsupported d_ff {dff}")


def _dense_ffn(h, x1, gate, w_gate, w_up, w_down, final_norm, final, seq):
    t, d = x1.shape
    dff = w_gate.shape[1]
    tm = min(TM_FFN, seq)
    tf = _ffn_tile(dff)
    per_b = seq // tm
    fn = final_norm.reshape(1, d).astype(F32)
    return pl.pallas_call(
        functools.partial(_ffn_kernel, final=final),
        grid=(t // tm, dff // tf),
        in_specs=[
            pl.BlockSpec((tm, d), lambda i, f: (i, 0)),
            pl.BlockSpec((tm, d), lambda i, f: (i, 0)),
            pl.BlockSpec((1, 1, d), lambda i, f: (i // per_b, 0, 0)),
            pl.BlockSpec((1, d), lambda i, f: (0, 0)),
            pl.BlockSpec((d, tf), lambda i, f: (0, f)),
            pl.BlockSpec((d, tf), lambda i, f: (0, f)),
            pl.BlockSpec((tf, d), lambda i, f: (f, 0)),
        ],
        out_specs=pl.BlockSpec((tm, d), lambda i, f: (i, 0)),
        out_shape=jax.ShapeDtypeStruct((t, d), F32),
        compiler_params=_cparams(("parallel", "arbitrary")),
        name="dense_ffn",
    )(h, x1, gate, fn, w_gate.astype(BF16), w_up.astype(BF16), w_down.astype(BF16))


def _row_copy(src, dst, sem):
    return pltpu.make_async_copy(src, dst, sem)


def _dispatch_kernel(dest_ref, h_ref, zeros_hbm, xs_hbm, sem, *, tm):
    del zeros_hbm

    def issue(g, carry):
        for j in range(ROW_DMA_UNROLL):
            t = g * ROW_DMA_UNROLL + j
            for kk in range(TOP_K):
                d = dest_ref[0, 0, TOP_K * t + kk]
                _row_copy(h_ref.at[pl.ds(t, 1), :], xs_hbm.at[pl.ds(d, 1), :], sem).start(priority=kk)
        return carry

    lax.fori_loop(0, tm // ROW_DMA_UNROLL, issue, 0)

    def drain(g, carry):
        for _ in range(ROW_DMA_UNROLL * TOP_K):
            _row_copy(h_ref.at[pl.ds(0, 1), :], xs_hbm.at[pl.ds(0, 1), :], sem).wait()
        return carry

    lax.fori_loop(0, tm // ROW_DMA_UNROLL, drain, 0)


def _moe_dispatch(h_f32, dest, n_rows):
    t, d = h_f32.shape
    tm = min(MOE_TOK_TILE, t)
    dest3 = dest.reshape(t // tm, 1, TOP_K * tm)
    return pl.pallas_call(
        functools.partial(_dispatch_kernel, tm=tm),
        grid=(t // tm,),
        in_specs=[
            pl.BlockSpec((1, 1, TOP_K * tm), lambda i: (i, 0, 0), memory_space=pltpu.SMEM),
            pl.BlockSpec((tm, d), lambda i: (i, 0)),
            pl.BlockSpec(memory_space=pl.ANY),
        ],
        out_specs=pl.BlockSpec(memory_space=pl.ANY),
        out_shape=jax.ShapeDtypeStruct((n_rows, d), F32),
        scratch_shapes=[pltpu.SemaphoreType.DMA(())],
        input_output_aliases={2: 0},
        compiler_params=_cparams(("arbitrary",)),
        name="moe_dispatch",
    )(dest3, h_f32, jnp.zeros((n_rows, d), F32))


def _experts_kernel(be_ref, nused_ref, x_ref, wg_ref, wu_ref, wd_ref, y_ref):
    i = pl.program_id(0)
    f = pl.program_id(1)

    @pl.when(f == 0)
    def _zero():
        y_ref[...] = jnp.zeros(y_ref.shape, F32)

    @pl.when(i < nused_ref[0])
    def _compute():
        xb = x_ref[...].astype(BF16)
        act = _silu(jnp.dot(xb, wg_ref[0], preferred_element_type=F32)) * jnp.dot(
            xb, wu_ref[0], preferred_element_type=F32)
        y_ref[...] += jnp.dot(act.astype(BF16), wd_ref[0], preferred_element_type=F32)


def _moe_experts(xs, block_expert, n_used, w_gate, w_up, w_down):
    n_rows, d = xs.shape
    dff = w_gate.shape[2]
    bm = MOE_ROWS
    tf = MOE_FF_TILE if dff % MOE_FF_TILE == 0 else dff

    def ff(i, f, nu):
        return jnp.where(i < nu[0], f, 0)

    return pl.pallas_call(
        _experts_kernel,
        grid_spec=pltpu.PrefetchScalarGridSpec(
            num_scalar_prefetch=2,
            grid=(n_rows // bm, dff // tf),
            in_specs=[
                pl.BlockSpec((bm, d), lambda i, f, be, nu: (i, 0)),
                pl.BlockSpec((1, d, tf), lambda i, f, be, nu: (be[i], 0, ff(i, f, nu))),
                pl.BlockSpec((1, d, tf), lambda i, f, be, nu: (be[i], 0, ff(i, f, nu))),
                pl.BlockSpec((1, tf, d), lambda i, f, be, nu: (be[i], ff(i, f, nu), 0)),
            ],
            out_specs=pl.BlockSpec((bm, d), lambda i, f, be, nu: (i, 0)),
        ),
        out_shape=jax.ShapeDtypeStruct((n_rows, d), F32),
        compiler_params=_cparams(("parallel", "arbitrary")),
        name="moe_experts",
    )(block_expert, n_used, xs, w_gate.astype(BF16), w_up.astype(BF16), w_down.astype(BF16))


def _combine_kernel(dest_ref, x1_ref, route_ref, gate_ref, fn_ref, ys_hbm, o_ref, buf_ref, sem,
                    *, tm, final):
    def issue(g, carry):
        for j in range(ROW_DMA_UNROLL):
            t = g * ROW_DMA_UNROLL + j
            for kk in range(TOP_K):
                d = dest_ref[0, 0, TOP_K * t + kk]
                _row_copy(ys_hbm.at[pl.ds(d, 1), :], buf_ref.at[kk, pl.ds(t, 1), :], sem).start(priority=kk)
        return carry

    lax.fori_loop(0, tm // ROW_DMA_UNROLL, issue, 0)

    def drain(g, carry):
        for _ in range(ROW_DMA_UNROLL * TOP_K):
            _row_copy(ys_hbm.at[pl.ds(0, 1), :], buf_ref.at[0, pl.ds(0, 1), :], sem).wait()
        return carry

    lax.fori_loop(0, tm // ROW_DMA_UNROLL, drain, 0)
    route = route_ref[...]
    w0 = route[:, ROUTE_W0:ROUTE_W0 + 1]
    w1 = route[:, ROUTE_W1:ROUTE_W1 + 1]
    f = w0 * buf_ref[0] + w1 * buf_ref[1]
    o_ref[...] = _finish(x1_ref[...] + gate_ref[0] * f, final, fn_ref)


def _moe_combine(ys, dest, x1, route, gate, final_norm, final, seq):
    t, d = x1.shape
    tm = min(MOE_TOK_TILE, seq)
    per_b = seq // tm
    dest3 = dest.reshape(t // tm, 1, TOP_K * tm)
    fn = final_norm.reshape(1, d).astype(F32)
    return pl.pallas_call(
        functools.partial(_combine_kernel, tm=tm, final=final),
        grid=(t // tm,),
        in_specs=[
            pl.BlockSpec((1, 1, TOP_K * tm), lambda i: (i, 0, 0), memory_space=pltpu.SMEM),
            pl.BlockSpec((tm, d), lambda i: (i, 0)),
            pl.BlockSpec((tm, LANES), lambda i: (i, 0)),
            pl.BlockSpec((1, 1, d), lambda i: (i // per_b, 0, 0)),
            pl.BlockSpec((1, d), lambda i: (0, 0)),
            pl.BlockSpec(memory_space=pl.ANY),
        ],
        out_specs=pl.BlockSpec((tm, d), lambda i: (i, 0)),
        out_shape=jax.ShapeDtypeStruct((t, d), F32),
        scratch_shapes=[pltpu.VMEM((TOP_K, tm, d), F32), pltpu.SemaphoreType.DMA(())],
        compiler_params=_cparams(("arbitrary",)),
        name="moe_combine",
    )(dest3, x1, route, gate, fn, ys)


def _moe_ffn(h_f32, x1, route, counts_slab, gate, w_gate, w_up, w_down, final_norm, final, seq):
    t, d = x1.shape
    bm = MOE_ROWS
    n_blocks = (t * TOP_K + bm - 1) // bm + N_EXPERTS
    n_rows = n_blocks * bm
    counts = counts_slab[0, :N_EXPERTS].astype(I32)
    padded = ((counts + bm - 1) // bm) * bm
    pend = jnp.cumsum(padded)
    pstart = pend - padded
    e = route[:, ROUTE_E0:ROUTE_E1 + 1].astype(I32)
    rank = route[:, ROUTE_R0:ROUTE_R1 + 1].astype(I32)
    dest = (jnp.sum(jnp.where(e[..., None] == jnp.arange(N_EXPERTS), pstart, 0), axis=-1) + rank)
    blk_start = jnp.arange(n_blocks, dtype=I32) * bm
    block_expert = jnp.minimum(jnp.sum(blk_start[:, None] >= pend[None, :], axis=1), N_EXPERTS - 1).astype(I32)
    n_used = (pend[-1:] // bm).astype(I32)
    xs = _moe_dispatch(h_f32, dest, n_rows)
    ys = _moe_experts(xs, block_expert, n_used, w_gate, w_up, w_down)
    return _moe_combine(ys, dest, x1, route, gate, final_norm, final, seq)


def kernel(x, c, ada_w, ada_b, w_in, mla_q_norm, mla_w_uq, mla_kv_norm, mla_w_ukv, ssd_conv_w, ssd_conv_b, ssd_dt_bias, ssd_a_log, ssd_d, ssd_norm, gdn_conv_w, gdn_conv_b, gdn_dt_bias, gdn_a_log, gdn_norm, w_out, ffn_w_gate, ffn_w_up, ffn_w_down, moe_router, moe_w_gate, moe_w_up, moe_w_down, final_norm):
    bsz, seq, d = x.shape
    depth = w_in.shape[0]
    t = bsz * seq
    mod = _ada_modulation(c, ada_w, ada_b)
    tables = _rope_tables(seq)
    xf = x.reshape(t, d)
    for l in range(depth):
        sh1, sc1, g1, sh2, sc2, g2 = [m.reshape(bsz, 1, d) for m in jnp.split(mod[l], 6, axis=-1)]
        mla_c, misc, ssd_z, ssd_xbc, gdn_qkv, gdn_z = _in_proj(xf, sh1, sc1, _prep_w_in(w_in[l]), seq)
        wqa, wqb, wkv = _prep_mla_weights(mla_w_uq[l], mla_w_ukv[l])
        qt, k, vt = _mla_proj(mla_c, misc, tables, mla_q_norm[l], mla_kv_norm[l], wqa, wqb, wkv, bsz, seq)
        mla_o = _flash_attention(qt, k, vt, bsz, seq)
        ssd_y = _ssd_mixer(ssd_z, ssd_xbc, misc, ssd_conv_w[l], ssd_conv_b[l], ssd_dt_bias[l],
                           ssd_a_log[l], ssd_d[l], ssd_norm[l], bsz, seq)
        gdn_o = _gdn_mixer(gdn_qkv, gdn_z, misc, gdn_conv_w[l], gdn_conv_b[l], gdn_dt_bias[l],
                           gdn_a_log[l], gdn_norm[l], bsz, seq)
        final = l == depth - 1
        if l % 2 == 0:
            x1, h2 = _post_mixer(xf, mla_o, ssd_y, gdn_o, w_out[l], g1, sh2, sc2, seq)
            xf = _dense_ffn(h2, x1, g2, ffn_w_gate[l // 2], ffn_w_up[l // 2], ffn_w_down[l // 2],
                            final_norm, final, seq)
        else:
            x1, h2, route, counts = _post_mixer(xf, mla_o, ssd_y, gdn_o, w_out[l], g1, sh2, sc2, seq,
                                                w_router=moe_router[l // 2])
            xf = _moe_ffn(h2, x1, route, counts, g2, moe_w_gate[l // 2], moe_w_up[l // 2],
                          moe_w_down[l // 2], final_norm, final, seq)
    return xf.reshape(bsz, seq, d)
```

```python
import functools
import math

import jax
import jax.numpy as jnp
import numpy as np
from jax import lax
from jax.experimental import pallas as pl
from jax.experimental.pallas import tpu as pltpu

F32 = jnp.float32
BF16 = jnp.bfloat16
I32 = jnp.int32
HIGHEST = lax.Precision.HIGHEST

D_MODEL = 1024
EPS = 1e-6
CONV_WIDTH = 4
MLA_HEADS, MLA_Q_RANK, MLA_KV_RANK = 4, 256, 128
MLA_NOPE, MLA_ROPE, MLA_V = 64, 32, 64
ROPE_THETA = 10000.0
SSD_HEADS, SSD_HEAD_DIM, SSD_GROUPS, SSD_STATE, SSD_CHUNK = 8, 64, 2, 64, 128
SSD_D_INNER = SSD_HEADS * SSD_HEAD_DIM
SSD_CONV_DIM = SSD_D_INNER + 2 * SSD_GROUPS * SSD_STATE
GDN_HEADS, GDN_DK, GDN_DV, GDN_CHUNK = 4, 64, 64, 64
GDN_QKV = GDN_HEADS * (2 * GDN_DK + GDN_DV)
GDN_OUT = GDN_HEADS * GDN_DV
MLA_IN = MLA_Q_RANK + MLA_KV_RANK + MLA_ROPE
MLA_OUT = MLA_HEADS * MLA_V
SSD_IN = SSD_D_INNER + SSD_CONV_DIM + SSD_HEADS
GDN_IN = GDN_QKV + GDN_HEADS * GDN_DV + 2 * GDN_HEADS
N_EXPERTS, TOP_K = 8, 2

LANES = 128
SUBLANES = 8
HALF = LANES // 2
VMEM_LIMIT_BYTES = 56 * 1024 * 1024

MISC_DT = 0
MISC_B = 8
MISC_A = 12
MISC_KR = 64
MISC_KR_SW = 96
HEAD_PAD = 128

TM_PROJ = 512
TM_FFN = 512
ATT_BLOCK = 1024
ATT_Q_COLS = 256
ATT_K_ROWS = 512
ATT_PIPELINE = 3
SSD_STEP_CHUNKS = 2
GDN_STEP_CHUNKS = 8
MOE_ROWS = 512
MOE_FF_TILE = 1792
MOE_TOK_TILE = 512
ROUTE_TILE = 512
ROW_DMA_UNROLL = 8


def _cparams(sem):
    return pltpu.CompilerParams(dimension_semantics=sem, vmem_limit_bytes=VMEM_LIMIT_BYTES)


def _lane_iota(shape):
    return lax.broadcasted_iota(I32, shape, len(shape) - 1)


def _row_iota(shape):
    return lax.broadcasted_iota(I32, shape, len(shape) - 2)


def _softplus(x):
    return jnp.maximum(x, 0.0) + jnp.log1p(jnp.exp(-jnp.abs(x)))


def _silu(x):
    return x * jax.nn.sigmoid(x)


def _rms(x):
    return x * lax.rsqrt(jnp.mean(x * x, axis=-1, keepdims=True) + EPS)


def _dot(a, b):
    return jnp.dot(a.astype(BF16), b.astype(BF16), preferred_element_type=F32)


def _dot_nt(a, b):
    return lax.dot_general(a.astype(BF16), b.astype(BF16), (((1,), (1,)), ((), ())),
                           preferred_element_type=F32)


def _dot_tn(a, b):
    return lax.dot_general(a.astype(BF16), b.astype(BF16), (((0,), (0,)), ((), ())),
                           preferred_element_type=F32)


def _dot_f32(a, b):
    return jnp.dot(a, b, precision=HIGHEST, preferred_element_type=F32)


def _ada_kernel(c_ref, w_ref, b_ref, o_ref):
    c_act = _silu(c_ref[...])
    o_ref[0] = _dot_f32(c_act, w_ref[0]) + b_ref[0]


def _ada_modulation(c, ada_w, ada_b):
    depth, d, n = ada_w.shape
    bsz = c.shape[0]
    rows = max(SUBLANES, bsz)
    c_pad = jnp.zeros((rows, d), F32).at[:bsz].set(c)
    tn = 1536
    out = pl.pallas_call(
        _ada_kernel,
        grid=(depth, n // tn),
        in_specs=[
            pl.BlockSpec((rows, d), lambda l, j: (0, 0)),
            pl.BlockSpec((1, d, tn), lambda l, j: (l, 0, j)),
            pl.BlockSpec((1, 1, tn), lambda l, j: (l, 0, j)),
        ],
        out_specs=pl.BlockSpec((1, rows, tn), lambda l, j: (l, 0, j)),
        out_shape=jax.ShapeDtypeStruct((depth, rows, n), F32),
        compiler_params=_cparams(("parallel", "parallel")),
        name="ada_modulation",
    )(c_pad, ada_w, ada_b.reshape(depth, 1, n))
    return out[:, :bsz]


IN_SLABS = (("mla_c", 384, F32), ("misc", 128, F32), ("ssd_z", 512, BF16), ("ssd_xbc", 768, BF16),
            ("gdn_qkv", 768, BF16), ("gdn_z", 256, BF16))
IN_WIDTH = sum(w for _, w, _ in IN_SLABS)


def _prep_w_in(w):
    d = w.shape[0]
    o_ssd = MLA_IN
    o_gdn = MLA_IN + SSD_IN
    w_kr = w[:, MLA_Q_RANK + MLA_KV_RANK:MLA_IN]
    half = MLA_ROPE // 2
    w_kr_sw = jnp.concatenate([-w_kr[:, half:], w_kr[:, :half]], axis=1)
    ssd_dt = w[:, o_ssd + SSD_D_INNER + SSD_CONV_DIM:o_ssd + SSD_IN]
    gdn_ba = w[:, o_gdn + GDN_QKV + GDN_OUT:o_gdn + GDN_IN]
    misc = jnp.concatenate(
        [ssd_dt, gdn_ba, jnp.zeros((d, MISC_KR - MISC_A - GDN_HEADS), w.dtype), w_kr, w_kr_sw], axis=1)
    cols = [
        w[:, :MLA_Q_RANK + MLA_KV_RANK], misc,
        w[:, o_ssd:o_ssd + SSD_D_INNER],
        w[:, o_ssd + SSD_D_INNER:o_ssd + SSD_D_INNER + SSD_CONV_DIM],
        w[:, o_gdn:o_gdn + GDN_QKV],
        w[:, o_gdn + GDN_QKV:o_gdn + GDN_QKV + GDN_OUT],
    ]
    return jnp.concatenate(cols, axis=1).astype(BF16)


def _in_proj_kernel(x_ref, sh_ref, sc_ref, w_ref, *o_refs):
    h = _rms(x_ref[...]) * (1.0 + sc_ref[0]) + sh_ref[0]
    hb = h.astype(BF16)
    off = 0
    for o_ref, (_, width, _) in zip(o_refs, IN_SLABS):
        o_ref[...] = jnp.dot(hb, w_ref[:, off:off + width], preferred_element_type=F32).astype(o_ref.dtype)
        off += width


def _in_proj(x2d, shift, scale, w_perm, seq):
    t, d = x2d.shape
    tm = min(TM_PROJ, seq)
    per_b = seq // tm
    return pl.pallas_call(
        _in_proj_kernel,
        grid=(t // tm,),
        in_specs=[
            pl.BlockSpec((tm, d), lambda i: (i, 0)),
            pl.BlockSpec((1, 1, d), lambda i: (i // per_b, 0, 0)),
            pl.BlockSpec((1, 1, d), lambda i: (i // per_b, 0, 0)),
            pl.BlockSpec((d, IN_WIDTH), lambda i: (0, 0)),
        ],
        out_specs=[pl.BlockSpec((tm, w), lambda i: (i, 0)) for _, w, _ in IN_SLABS],
        out_shape=[jax.ShapeDtypeStruct((t, w), dt) for _, w, dt in IN_SLABS],
        compiler_params=_cparams(("parallel",)),
        name="in_proj",
    )(x2d, shift, scale, w_perm)


def _rope_tables(seq):
    pos = jnp.arange(seq, dtype=F32)
    inv_freq = ROPE_THETA ** (-jnp.arange(0, MLA_ROPE, 2, dtype=F32) / MLA_ROPE)
    ang = pos[:, None] * inv_freq[None, :]
    cos, sin = jnp.cos(ang), jnp.sin(ang)
    zeros_l = jnp.zeros((seq, MLA_NOPE), F32)
    zeros_r = jnp.zeros((seq, HEAD_PAD - MLA_NOPE - MLA_ROPE), F32)
    cos_k = jnp.concatenate([zeros_l, cos, cos, zeros_r], axis=1)
    sin_k = jnp.concatenate([zeros_l, sin, sin, zeros_r], axis=1)
    scale = (MLA_NOPE + MLA_ROPE) ** -0.5 * math.log2(math.e)
    cos_q = scale * jnp.concatenate([jnp.ones((seq, MLA_NOPE), F32), cos, cos, zeros_r], axis=1)
    sin_q = scale * sin_k
    return cos_q, sin_q, cos_k, sin_k


def _prep_mla_weights(w_uq, w_ukv):
    r = w_uq.shape[0]
    hq = MLA_NOPE + MLA_ROPE
    half = MLA_ROPE // 2
    qa, qb = [], []
    for h in range(MLA_HEADS):
        nope = w_uq[:, h * hq:h * hq + MLA_NOPE]
        rope = w_uq[:, h * hq + MLA_NOPE:(h + 1) * hq]
        rope_sw = jnp.concatenate([-rope[:, half:], rope[:, :half]], axis=1)
        pad = jnp.zeros((r, HEAD_PAD - hq), w_uq.dtype)
        qa += [nope, rope, pad]
        qb += [jnp.zeros((r, MLA_NOPE), w_uq.dtype), rope_sw, pad]
    rk = w_ukv.shape[0]
    hk = MLA_NOPE + MLA_V
    kcols, vcols = [], []
    for h in range(MLA_HEADS):
        kcols += [w_ukv[:, h * hk:h * hk + MLA_NOPE], jnp.zeros((rk, HEAD_PAD - MLA_NOPE), w_ukv.dtype)]
        vcols += [w_ukv[:, h * hk + MLA_NOPE:(h + 1) * hk], jnp.zeros((rk, HEAD_PAD - MLA_V), w_ukv.dtype)]
    return (jnp.concatenate(qa, axis=1).astype(BF16), jnp.concatenate(qb, axis=1).astype(BF16),
            jnp.concatenate(kcols + vcols, axis=1).astype(BF16))


def _mla_proj_kernel(c_ref, misc_ref, cq_ref, sq_ref, ck_ref, sk_ref, qn_ref, kvn_ref,
                     wqa_ref, wqb_ref, wkv_ref, qt_ref, k_ref, vt_ref):
    c = c_ref[...]
    cq = (_rms(c[:, :MLA_Q_RANK]) * qn_ref[...]).astype(BF16)
    ckv = (_rms(c[:, MLA_Q_RANK:]) * kvn_ref[...]).astype(BF16)
    qa = jnp.dot(cq, wqa_ref[...], preferred_element_type=F32)
    qb = jnp.dot(cq, wqb_ref[...], preferred_element_type=F32)
    cos_q = jnp.concatenate([cq_ref[...]] * MLA_HEADS, axis=1)
    sin_q = jnp.concatenate([sq_ref[...]] * MLA_HEADS, axis=1)
    qt_ref[0] = (qa * cos_q + qb * sin_q).T.astype(BF16)
    kv = jnp.dot(ckv, wkv_ref[...], preferred_element_type=F32)
    misc = misc_ref[...]
    kr = misc * ck_ref[...] + pltpu.roll(misc, LANES - (MISC_KR_SW - MISC_KR), 1) * sk_ref[...]
    kw = MLA_HEADS * HEAD_PAD
    k_ref[0] = (kv[:, :kw] + jnp.concatenate([kr] * MLA_HEADS, axis=1)).astype(BF16)
    v = kv[:, kw:]
    v = jnp.where(_lane_iota(v.shape) % HEAD_PAD == MLA_V, 1.0, v)
    vt_ref[0] = v.T.astype(BF16)


def _mla_proj(mla_c, misc, tables, q_norm, kv_norm, wqa, wqb, wkv, bsz, seq):
    t = mla_c.shape[0]
    tm = min(TM_PROJ, seq)
    per_b = seq // tm
    kw = MLA_HEADS * HEAD_PAD
    tab_spec = pl.BlockSpec((tm, HEAD_PAD), lambda i: (i % per_b, 0))
    full = lambda a: pl.BlockSpec(a.shape, lambda i: (0,) * a.ndim)
    qn = q_norm.reshape(1, -1)
    kvn = kv_norm.reshape(1, -1)
    t_spec = pl.BlockSpec((1, kw, tm), lambda i: (i // per_b, 0, i % per_b))
    return pl.pallas_call(
        _mla_proj_kernel,
        grid=(t // tm,),
        in_specs=[
            pl.BlockSpec((tm, mla_c.shape[1]), lambda i: (i, 0)),
            pl.BlockSpec((tm, LANES), lambda i: (i, 0)),
            tab_spec, tab_spec, tab_spec, tab_spec,
            full(qn), full(kvn), full(wqa), full(wqb), full(wkv),
        ],
        out_specs=[t_spec, pl.BlockSpec((1, tm, kw), lambda i: (i // per_b, i % per_b, 0)), t_spec],
        out_shape=[jax.ShapeDtypeStruct((bsz, kw, seq), BF16), jax.ShapeDtypeStruct((bsz, seq, kw), BF16),
                   jax.ShapeDtypeStruct((bsz, kw, seq), BF16)],
        compiler_params=_cparams(("parallel",)),
        name="mla_proj",
    )(mla_c, misc, *tables, qn, kvn, wqa, wqb, wkv)


def _flash_kernel(qi_tab, ki_tab, qt_ref, k_ref, vt_ref, o_ref, m_ref, acc_ref, *, blk, rq, ck):
    p = pl.program_id(1)
    qi = qi_tab[p]
    ki = ki_tab[p]

    @pl.when(ki == 0)
    def _init():
        m_ref[...] = jnp.full(m_ref.shape, -jnp.inf, F32)
        acc_ref[...] = jnp.zeros(acc_ref.shape, F32)

    def step(diag):
        units = []
        for rc in range(blk // rq):
            for kc in range(blk // ck):
                if diag and kc * ck > rc * rq + rq - 1:
                    continue
                for h in range(MLA_HEADS):
                    units.append((h, rc * rq, kc * ck))

        def scores(unit):
            h, c0, k0 = unit
            kk = k_ref[0, k0:k0 + ck, h * HEAD_PAD:(h + 1) * HEAD_PAD]
            qt = qt_ref[0, h * HEAD_PAD:(h + 1) * HEAD_PAD, c0:c0 + rq]
            s = jnp.dot(kk, qt, preferred_element_type=F32)
            if diag and k0 + ck - 1 > c0:
                key = lax.broadcasted_iota(I32, s.shape, 0) + k0
                qry = lax.broadcasted_iota(I32, s.shape, 1) + c0
                s = jnp.where(key <= qry, s, -jnp.inf)
            return s

        def update(unit, s):
            h, c0, k0 = unit
            m_prev = m_ref[h, :, c0:c0 + rq]
            m_new = jnp.maximum(m_prev, jnp.max(s, axis=0, keepdims=True))
            alpha = jnp.exp2(m_prev - m_new)
            pt = jnp.exp2(s - m_new).astype(BF16)
            m_ref[h, :, c0:c0 + rq] = m_new
            acc_ref[h, :, c0:c0 + rq] = acc_ref[h, :, c0:c0 + rq] * alpha + jnp.dot(
                vt_ref[0, h * HEAD_PAD:(h + 1) * HEAD_PAD, k0:k0 + ck], pt, preferred_element_type=F32)

        pending = [scores(u) for u in units[:ATT_PIPELINE]]
        for n, unit in enumerate(units):
            s = pending.pop(0)
            if n + ATT_PIPELINE < len(units):
                pending.append(scores(units[n + ATT_PIPELINE]))
            update(unit, s)

    @pl.when(ki < qi)
    def _off_diagonal():
        step(False)

    @pl.when(ki == qi)
    def _diagonal():
        step(True)
        for h in range(MLA_HEADS):
            acc = acc_ref[h]
            o_ref[0, h * MLA_V:(h + 1) * MLA_V, :] = (acc[:MLA_V] / acc[MLA_V:MLA_V + 1]).astype(o_ref.dtype)


def _flash_attention(qt, k, vt, bsz, seq):
    blk = min(ATT_BLOCK, seq)
    rq = min(ATT_Q_COLS, blk)
    ck = min(ATT_K_ROWS, blk)
    nb = seq // blk
    pairs = [(a, b) for a in range(nb) for b in range(a + 1)]
    qi_tab = jnp.asarray([a for a, _ in pairs], I32)
    ki_tab = jnp.asarray([b for _, b in pairs], I32)
    kw = MLA_HEADS * HEAD_PAD
    return pl.pallas_call(
        functools.partial(_flash_kernel, blk=blk, rq=rq, ck=ck),
        grid_spec=pltpu.PrefetchScalarGridSpec(
            num_scalar_prefetch=2,
            grid=(bsz, len(pairs)),
            in_specs=[
                pl.BlockSpec((1, kw, blk), lambda b, p, qt_, kt_: (b, 0, qt_[p])),
                pl.BlockSpec((1, blk, kw), lambda b, p, qt_, kt_: (b, kt_[p], 0)),
                pl.BlockSpec((1, kw, blk), lambda b, p, qt_, kt_: (b, 0, kt_[p])),
            ],
            out_specs=pl.BlockSpec((1, MLA_OUT, blk), lambda b, p, qt_, kt_: (b, 0, qt_[p])),
            scratch_shapes=[pltpu.VMEM((MLA_HEADS, 1, blk), F32),
                            pltpu.VMEM((MLA_HEADS, HEAD_PAD, blk), F32)],
        ),
        out_shape=jax.ShapeDtypeStruct((bsz, MLA_OUT, seq), BF16),
        compiler_params=_cparams(("parallel", "arbitrary")),
        name="mla_flash",
    )(qi_tab, ki_tab, qt, k, vt)


def _causal_conv_silu(cur, ext_ref, halo_ref, cw_ref, cb_ref, first):
    rows = cur.shape[0]

    @pl.when(first)
    def _zero_halo():
        halo_ref[...] = jnp.zeros(halo_ref.shape, F32)

    ext_ref[0:SUBLANES, :] = halo_ref[...]
    ext_ref[SUBLANES:SUBLANES + rows, :] = cur
    halo_ref[...] = cur[rows - SUBLANES:, :]
    acc = cb_ref[...] + cw_ref[CONV_WIDTH - 1:CONV_WIDTH, :] * cur
    for j in range(CONV_WIDTH - 1):
        start = SUBLANES - (CONV_WIDTH - 1) + j
        acc = acc + cw_ref[j:j + 1, :] * ext_ref[start:start + rows, :]
    return _silu(acc)


def _ssd_kernel(z_ref, xbc_ref, misc_ref, cw_ref, cb_ref, dtb_ref, aneg_ref, dsk_ref, nw_ref,
                y_ref, ext_ref, halo_ref, st_ref, *, nchunk):
    first = pl.program_id(1) == 0
    length = SSD_CHUNK
    xbc = _causal_conv_silu(xbc_ref[0].astype(F32), ext_ref, halo_ref, cw_ref, cb_ref, first)

    @pl.when(first)
    def _zero_state():
        st_ref[...] = jnp.zeros(st_ref.shape, F32)

    dt_all = _softplus(misc_ref[0] + dtb_ref[...])
    a_all = dt_all * aneg_ref[...]
    tri = (_lane_iota((length, length)) <= _row_iota((length, length))).astype(F32)
    lower = _lane_iota((length, length)) <= _row_iota((length, length))
    lane = _lane_iota((length, LANES))
    lo = lane < HALF
    bw = SSD_GROUPS * SSD_STATE

    for c in range(nchunk):
        r0 = c * length
        dt = dt_all[r0:r0 + length]
        acum = _dot_f32(tri, a_all[r0:r0 + length])
        acum_t = acum.T
        x_c = xbc[r0:r0 + length, :SSD_D_INNER]
        b_c = xbc[r0:r0 + length, SSD_D_INNER:SSD_D_INNER + bw]
        c_c = xbc[r0:r0 + length, SSD_D_INNER + bw:]
        y_parts = []
        for g in range(SSD_GROUPS):
            b_g = b_c[:, g * SSD_STATE:(g + 1) * SSD_STATE]
            c_g = c_c[:, g * SSD_STATE:(g + 1) * SSD_STATE].astype(BF16)
            b_gt = b_g.T.astype(BF16)
            cb = jnp.dot(c_g, b_gt, preferred_element_type=F32)
            pairs_per_group = SSD_HEADS // SSD_GROUPS // 2
            for pp in range(pairs_per_group):
                pr = g * pairs_per_group + pp
                h0, h1 = 2 * pr, 2 * pr + 1
                col0, col1 = acum[:, h0:h0 + 1], acum[:, h1:h1 + 1]
                l0 = jnp.exp(jnp.where(lower, col0 - acum_t[h0:h0 + 1, :], -jnp.inf))
                l1 = jnp.exp(jnp.where(lower, col1 - acum_t[h1:h1 + 1, :], -jnp.inf))
                mmat = jnp.concatenate([cb * l0, cb * l1], axis=1).astype(BF16)
                xp = x_c[:, pr * LANES:(pr + 1) * LANES]
                xdt = xp * jnp.where(lo, dt[:, h0:h0 + 1], dt[:, h1:h1 + 1])
                rhs = jnp.concatenate([jnp.where(lo, xdt, 0.0), jnp.where(lo, 0.0, xdt)], axis=0)
                y_diag = jnp.dot(mmat, rhs.astype(BF16), preferred_element_type=F32)
                col_pair = jnp.where(lo, col0, col1)
                last_pair = jnp.where(lo[:1], acum[length - 1:length, h0:h0 + 1],
                                      acum[length - 1:length, h1:h1 + 1])
                st = st_ref[pr]
                y_off = jnp.dot(c_g, st.astype(BF16), preferred_element_type=F32) * jnp.exp(col_pair)
                xdec = (xdt * jnp.exp(last_pair - col_pair)).astype(BF16)
                st_ref[pr] = st * jnp.exp(last_pair) + jnp.dot(b_gt, xdec, preferred_element_type=F32)
                y_parts.append(y_diag + y_off + xp * dsk_ref[:, pr * LANES:(pr + 1) * LANES])
        y = jnp.concatenate(y_parts, axis=1) * _silu(z_ref[0, r0:r0 + length, :].astype(F32))
        gw = SSD_D_INNER // SSD_GROUPS
        y = jnp.concatenate([_rms(y[:, g * gw:(g + 1) * gw]) for g in range(SSD_GROUPS)], axis=1)
        y_ref[0, r0:r0 + length, :] = (y * nw_ref[...]).astype(y_ref.dtype)


def _ssd_mixer(z, xbc, misc, conv_w, conv_b, dt_bias, a_log, d_skip, norm_w, bsz, seq):
    nchunk = min(SSD_STEP_CHUNKS, seq // SSD_CHUNK)
    rows = nchunk * SSD_CHUNK
    pad = jnp.zeros((LANES - SSD_HEADS,), F32)
    dtb = jnp.concatenate([dt_bias.astype(F32), pad]).reshape(1, LANES)
    aneg = jnp.concatenate([-jnp.exp(a_log.astype(F32)), pad]).reshape(1, LANES)
    dsk = jnp.repeat(d_skip.astype(F32), SSD_HEAD_DIM).reshape(1, SSD_D_INNER)
    full = lambda a: pl.BlockSpec(a.shape, lambda b, i: (0,) * a.ndim)
    cb2 = conv_b.reshape(1, -1)
    nw2 = norm_w.reshape(1, -1)
    blk = lambda w: pl.BlockSpec((1, rows, w), lambda b, i: (b, i, 0))
    out = pl.pallas_call(
        functools.partial(_ssd_kernel, nchunk=nchunk),
        grid=(bsz, seq // rows),
        in_specs=[blk(SSD_D_INNER), blk(SSD_CONV_DIM), blk(LANES),
                  full(conv_w), full(cb2), full(dtb), full(aneg), full(dsk), full(nw2)],
        out_specs=blk(SSD_D_INNER),
        out_shape=jax.ShapeDtypeStruct((bsz, seq, SSD_D_INNER), BF16),
        scratch_shapes=[pltpu.VMEM((rows + SUBLANES, SSD_CONV_DIM), F32),
                        pltpu.VMEM((SUBLANES, SSD_CONV_DIM), F32),
                        pltpu.VMEM((SSD_HEADS // 2, SSD_STATE, LANES), F32)],
        compiler_params=_cparams(("parallel", "arbitrary")),
        name="ssd_mixer",
    )(z.reshape(bsz, seq, -1), xbc.reshape(bsz, seq, -1), misc.reshape(bsz, seq, -1),
      conv_w, cb2, dtb, aneg, dsk, nw2)
    return out.reshape(bsz * seq, SSD_D_INNER)


def _half_sum(x, lo):
    s_lo = jnp.sum(jnp.where(lo, x, 0.0), axis=1, keepdims=True)
    s_hi = jnp.sum(jnp.where(lo, 0.0, x), axis=1, keepdims=True)
    return jnp.where(lo, s_lo, s_hi)


def _gdn_kernel(qkv_ref, z_ref, misc_ref, cw_ref, cb_ref, dtb_ref, aneg_ref, nw_ref,
                o_ref, ext_ref, halo_ref, st_ref, *, nchunk, bsz):
    first = pl.program_id(0) == 0
    length = GDN_CHUNK
    two = 2 * length

    @pl.when(first)
    def _zero_state():
        st_ref[...] = jnp.zeros(st_ref.shape, F32)

    tri = (_lane_iota((length, length)) <= _row_iota((length, length))).astype(F32)
    lane = _lane_iota((length, LANES))
    lo = lane < HALF
    r2 = _row_iota((two, two))
    c2 = _lane_iota((two, two))
    same_blk = (r2 < length) == (c2 < length)
    low_incl = same_blk & (c2 <= r2)
    low_strict = same_blk & (c2 < r2)
    eye = (r2 == c2).astype(F32)
    top_lo = (r2 < length) == (c2 < HALF)
    hk = GDN_HEADS * GDN_DK
    qscale = GDN_DK ** -0.5
    npair = GDN_HEADS // 2

    def blockdiag(slab):
        return jnp.where(top_lo, jnp.concatenate([slab, slab], axis=0), 0.0)

    def fold(bd):
        return bd[:length] + bd[length:]

    chains = []
    for b in range(bsz):
        qkv = _causal_conv_silu(qkv_ref[b].astype(F32), ext_ref.at[b], halo_ref.at[b], cw_ref, cb_ref, first)
        misc = misc_ref[b]
        beta_all = jax.nn.sigmoid(misc)
        g_all = aneg_ref[...] * _softplus(misc + dtb_ref[...])
        for c in range(nchunk):
            r0 = c * length
            gcum = _dot_f32(tri, g_all[r0:r0 + length])
            beta = beta_all[r0:r0 + length]
            for pr in range(npair):
                h0, h1 = 2 * pr, 2 * pr + 1
                q = qkv[r0:r0 + length, pr * LANES:(pr + 1) * LANES]
                k = qkv[r0:r0 + length, hk + pr * LANES:hk + (pr + 1) * LANES]
                v = qkv[r0:r0 + length, 2 * hk + pr * LANES:2 * hk + (pr + 1) * LANES]
                qn = q * lax.rsqrt(_half_sum(q * q, lo) + EPS) * qscale
                kn = k * lax.rsqrt(_half_sum(k * k, lo) + EPS)
                g0 = gcum[:, MISC_A + h0:MISC_A + h0 + 1]
                g1 = gcum[:, MISC_A + h1:MISC_A + h1 + 1]
                gexp = jnp.where(lo, g0, g1)
                bexp = jnp.where(lo, beta[:, MISC_B + h0:MISC_B + h0 + 1],
                                 beta[:, MISC_B + h1:MISC_B + h1 + 1])
                gcol = jnp.concatenate([jnp.broadcast_to(g0, (length, two)),
                                        jnp.broadcast_to(g1, (length, two))], axis=0)
                decay = jnp.exp(jnp.where(low_incl, gcol - gcol.T, -jnp.inf))
                kb = kn * bexp
                kk2 = jnp.concatenate([kn, kn], axis=0)
                kq = _dot_nt(jnp.concatenate([blockdiag(kb), blockdiag(qn)], axis=0), kk2)
                a_mat = jnp.where(low_strict, kq[:two] * decay, 0.0)
                qk = jnp.where(low_incl, kq[two:] * decay, 0.0)
                g_last = gexp[length - 1:length, :]
                chains.append(dict(
                    b=b, c=c, pr=pr, pw=a_mat, t=eye - a_mat, qk=qk,
                    vb=v * bexp, kbg=kb * jnp.exp(gexp), q_dec=qn * jnp.exp(gexp),
                    k_tail=kn * jnp.exp(g_last - gexp), gl=jnp.exp(g_last)))

    for _ in range(int(math.log2(length)) - 1):
        for ch in chains:
            ch["pw"] = _dot(ch["pw"], ch["pw"])
        for ch in chains:
            ch["t"] = ch["t"] + _dot(ch["t"], ch["pw"])

    for ch in chains:
        ku = _dot(ch["t"], jnp.concatenate([blockdiag(ch["kbg"]), blockdiag(ch["vb"])], axis=1))
        ch["k_cum"] = fold(ku[:, :LANES])
        ch["u"] = fold(ku[:, LANES:])
    for ch in chains:
        rhs = jnp.concatenate([ch["k_cum"], ch["u"]], axis=1)
        mn = _dot_tn(ch["k_tail"], rhs)
        ch["m"] = jnp.where(top_lo, mn[:, :LANES], 0.0)
        ch["n"] = jnp.where(top_lo, mn[:, LANES:], 0.0)
        qo = _dot(ch["qk"], jnp.concatenate([blockdiag(ch["k_cum"]), blockdiag(ch["u"])], axis=1))
        ch["q_eff"] = ch["q_dec"] - fold(qo[:, :LANES])
        ch["o_loc"] = fold(qo[:, LANES:])

    states = {(b, pr): st_ref[b * npair + pr] for b in range(bsz) for pr in range(npair)}
    for c in range(nchunk):
        for ch in chains:
            if ch["c"] != c:
                continue
            b, pr = ch["b"], ch["pr"]
            st = states[(b, pr)]
            both = _dot(jnp.concatenate([ch["q_eff"], ch["m"]], axis=0), st)
            o = both[:length] + ch["o_loc"]
            states[(b, pr)] = st * ch["gl"] - both[length:] + ch["n"]
            r0 = c * length
            ms = _half_sum(o * o, lo) * (1.0 / GDN_DV)
            zz = z_ref[b, r0:r0 + length, pr * LANES:(pr + 1) * LANES].astype(F32)
            out = o * lax.rsqrt(ms + EPS) * nw_ref[...] * _silu(zz)
            o_ref[b, r0:r0 + length, pr * LANES:(pr + 1) * LANES] = out.astype(o_ref.dtype)
    for (b, pr), st in states.items():
        st_ref[b * npair + pr] = st


def _gdn_mixer(qkv, z, misc, conv_w, conv_b, dt_bias, a_log, norm_w, bsz, seq):
    nchunk = min(GDN_STEP_CHUNKS, seq // GDN_CHUNK)
    rows = nchunk * GDN_CHUNK
    dtb = jnp.zeros((1, LANES), F32).at[0, MISC_A:MISC_A + GDN_HEADS].set(dt_bias.astype(F32))
    aneg = jnp.zeros((1, LANES), F32).at[0, MISC_A:MISC_A + GDN_HEADS].set(-jnp.exp(a_log.astype(F32)))
    nw2 = jnp.concatenate([norm_w.astype(F32)] * 2).reshape(1, LANES)
    cb2 = conv_b.reshape(1, -1)
    full = lambda a: pl.BlockSpec(a.shape, lambda i: (0,) * a.ndim)
    blk = lambda w: pl.BlockSpec((bsz, rows, w), lambda i: (0, i, 0))
    out = pl.pallas_call(
        functools.partial(_gdn_kernel, nchunk=nchunk, bsz=bsz),
        grid=(seq // rows,),
        in_specs=[blk(GDN_QKV), blk(GDN_OUT), blk(LANES),
                  full(conv_w), full(cb2), full(dtb), full(aneg), full(nw2)],
        out_specs=blk(GDN_OUT),
        out_shape=jax.ShapeDtypeStruct((bsz, seq, GDN_OUT), BF16),
        scratch_shapes=[pltpu.VMEM((bsz, rows + SUBLANES, GDN_QKV), F32),
                        pltpu.VMEM((bsz, SUBLANES, GDN_QKV), F32),
                        pltpu.VMEM((bsz * (GDN_HEADS // 2), 2 * GDN_DK, LANES), F32)],
        compiler_params=_cparams(("arbitrary",)),
        name="gdn_mixer",
    )(qkv.reshape(bsz, seq, -1), z.reshape(bsz, seq, -1), misc.reshape(bsz, seq, -1),
      conv_w, cb2, dtb, aneg, nw2)
    return out.reshape(bsz * seq, GDN_OUT)


ROUTE_E0, ROUTE_E1, ROUTE_R0, ROUTE_R1, ROUTE_W0, ROUTE_W1 = range(6)


def _post_kernel(*refs, route):
    if route:
        (x_ref, a_ref, s_ref, g_ref, wa_ref, ws_ref, wg_ref, gate_ref, sh_ref, sc_ref, wrh_ref, wrl_ref,
         x1_ref, h_ref, route_ref, cnt_ref, carry_ref) = refs
    else:
        (x_ref, a_ref, s_ref, g_ref, wa_ref, ws_ref, wg_ref, gate_ref, sh_ref, sc_ref,
         x1_ref, h_ref) = refs
    mix = (lax.dot_general(a_ref[0], wa_ref[...], (((0,), (0,)), ((), ())), preferred_element_type=F32)
           + jnp.dot(s_ref[...], ws_ref[...], preferred_element_type=F32)
           + jnp.dot(g_ref[...], wg_ref[...], preferred_element_type=F32))
    x1 = x_ref[...] + gate_ref[0] * mix
    x1_ref[...] = x1
    h = _rms(x1) * (1.0 + sc_ref[0]) + sh_ref[0]
    h_ref[...] = h.astype(h_ref.dtype)
    if not route:
        return

    @pl.when(pl.program_id(0) == 0)
    def _zero_carry():
        carry_ref[...] = jnp.zeros(carry_ref.shape, F32)

    tm = h.shape[0]
    lane = _lane_iota((tm, LANES))
    h_hi = h.astype(BF16)
    h_lo = (h - h_hi.astype(F32)).astype(BF16)
    logits = (jnp.dot(h_hi, wrh_ref[...], preferred_element_type=F32)
              + (jnp.dot(h_lo, wrh_ref[...], preferred_element_type=F32)
                 + jnp.dot(h_hi, wrl_ref[...], preferred_element_type=F32)))
    logits = jnp.where(lane < N_EXPERTS, logits, -jnp.inf)
    lane_f = lane.astype(F32)
    m0 = jnp.max(logits, axis=1, keepdims=True)
    e0 = jnp.min(jnp.where(logits == m0, lane_f, float(LANES)), axis=1, keepdims=True)
    rest = jnp.where(lane_f == e0, -jnp.inf, logits)
    m1 = jnp.max(rest, axis=1, keepdims=True)
    e1 = jnp.min(jnp.where(rest == m1, lane_f, float(LANES)), axis=1, keepdims=True)
    ex = jnp.exp(m1 - m0)
    w0 = 1.0 / (1.0 + ex)
    w1 = ex / (1.0 + ex)
    oh0 = (lane_f == e0).astype(F32)
    oh1 = (lane_f == e1).astype(F32)
    both = oh0 + oh1
    strict = (_lane_iota((tm, tm)) < _row_iota((tm, tm))).astype(BF16)
    before = jnp.dot(strict, both.astype(BF16), preferred_element_type=F32) + carry_ref[0:1, :]
    r0 = jnp.sum(before * oh0, axis=1, keepdims=True)
    r1 = jnp.sum(before * oh1, axis=1, keepdims=True)
    carry_ref[0:1, :] = carry_ref[0:1, :] + jnp.sum(both, axis=0, keepdims=True)
    slab = jnp.zeros((tm, LANES), F32)
    for pos, val in ((ROUTE_E0, e0), (ROUTE_E1, e1), (ROUTE_R0, r0), (ROUTE_R1, r1),
                     (ROUTE_W0, w0), (ROUTE_W1, w1)):
        slab = jnp.where(lane == pos, val, slab)
    route_ref[...] = slab
    cnt_ref[...] = carry_ref[...]


def _post_mixer(x2d, mla_o, ssd_y, gdn_o, w_out, gate, shift, scale, seq, w_router=None):
    t, d = x2d.shape
    route = w_router is not None
    tm = min(ROUTE_TILE if route else TM_PROJ, seq)
    per_b = seq // tm
    wa = w_out[:MLA_OUT].astype(BF16)
    ws = w_out[MLA_OUT:MLA_OUT + SSD_D_INNER].astype(BF16)
    wg = w_out[MLA_OUT + SSD_D_INNER:].astype(BF16)
    row = lambda w: pl.BlockSpec((tm, w), lambda i: (i, 0))
    full = lambda a: pl.BlockSpec(a.shape, lambda i: (0,) * a.ndim)
    mod = pl.BlockSpec((1, 1, d), lambda i: (i // per_b, 0, 0))
    att = pl.BlockSpec((1, MLA_OUT, tm), lambda i: (i // per_b, 0, i % per_b))
    in_specs = [row(d), att, row(SSD_D_INNER), row(GDN_OUT), full(wa), full(ws), full(wg),
                mod, mod, mod]
    args = [x2d, mla_o, ssd_y, gdn_o, wa, ws, wg, gate, shift, scale]
    out_specs = [row(d), row(d)]
    out_shape = [jax.ShapeDtypeStruct((t, d), F32), jax.ShapeDtypeStruct((t, d), F32 if route else BF16)]
    scratch = []
    if route:
        wr = jnp.zeros((d, LANES), F32).at[:, :N_EXPERTS].set(w_router.astype(F32))
        wr_hi = wr.astype(BF16)
        wr_lo = (wr - wr_hi.astype(F32)).astype(BF16)
        in_specs += [full(wr_hi), full(wr_lo)]
        args += [wr_hi, wr_lo]
        out_specs += [row(LANES), pl.BlockSpec((SUBLANES, LANES), lambda i: (0, 0))]
        out_shape += [jax.ShapeDtypeStruct((t, LANES), F32), jax.ShapeDtypeStruct((SUBLANES, LANES), F32)]
        scratch = [pltpu.VMEM((SUBLANES, LANES), F32)]
    return pl.pallas_call(
        functools.partial(_post_kernel, route=route),
        grid=(t // tm,),
        in_specs=in_specs, out_specs=out_specs, out_shape=out_shape, scratch_shapes=scratch,
        compiler_params=_cparams(("arbitrary",) if route else ("parallel",)),
        name="post_mixer_route" if route else "post_mixer",
    )(*args)


def _finish(x, final, fn_ref):
    return _rms(x) * fn_ref[...] if final else x


def _ffn_kernel(h_ref, x1_ref, gate_ref, fn_ref, wg_ref, wu_ref, wd_ref, o_ref, *, final):
    f = pl.program_id(1)
    h = h_ref[...]
    act = _silu(jnp.dot(h, wg_ref[...], preferred_element_type=F32)) * jnp.dot(
        h, wu_ref[...], preferred_element_type=F32)
    part = jnp.dot(act.astype(BF16), wd_ref[...], preferred_element_type=F32)

    @pl.when(f == 0)
    def _set():
        o_ref[...] = part

    @pl.when(f > 0)
    def _add():
        o_ref[...] += part

    @pl.when(f == pl.num_programs(1) - 1)
    def _residual():
        o_ref[...] = _finish(x1_ref[...] + gate_ref[0] * o_ref[...], final, fn_ref)


def _ffn_tile(dff):
    for cand in (2816, 1408, 1024, 512, 256, 128):
        if dff % cand == 0:
            return cand
    raise ValueError(f"unsupported d_ff {dff}")


def _dense_ffn(h, x1, gate, w_gate, w_up, w_down, final_norm, final, seq):
    t, d = x1.shape
    dff = w_gate.shape[1]
    tm = min(TM_FFN, seq)
    tf = _ffn_tile(dff)
    per_b = seq // tm
    fn = final_norm.reshape(1, d).astype(F32)
    return pl.pallas_call(
        functools.partial(_ffn_kernel, final=final),
        grid=(t // tm, dff // tf),
        in_specs=[
            pl.BlockSpec((tm, d), lambda i, f: (i, 0)),
            pl.BlockSpec((tm, d), lambda i, f: (i, 0)),
            pl.BlockSpec((1, 1, d), lambda i, f: (i // per_b, 0, 0)),
            pl.BlockSpec((1, d), lambda i, f: (0, 0)),
            pl.BlockSpec((d, tf), lambda i, f: (0, f)),
            pl.BlockSpec((d, tf), lambda i, f: (0, f)),
            pl.BlockSpec((tf, d), lambda i, f: (f, 0)),
        ],
        out_specs=pl.BlockSpec((tm, d), lambda i, f: (i, 0)),
        out_shape=jax.ShapeDtypeStruct((t, d), F32),
        compiler_params=_cparams(("parallel", "arbitrary")),
        name="dense_ffn",
    )(h, x1, gate, fn, w_gate.astype(BF16), w_up.astype(BF16), w_down.astype(BF16))


def _row_copy(src, dst, sem):
    return pltpu.make_async_copy(src, dst, sem)


def _dispatch_kernel(dest_ref, h_ref, zeros_hbm, xs_hbm, sem, *, tm):
    del zeros_hbm

    def issue(g, carry):
        for j in range(ROW_DMA_UNROLL):
            t = g * ROW_DMA_UNROLL + j
            for kk in range(TOP_K):
                d = dest_ref[0, 0, TOP_K * t + kk]
                _row_copy(h_ref.at[pl.ds(t, 1), :], xs_hbm.at[pl.ds(d, 1), :], sem).start(priority=kk)
        return carry

    lax.fori_loop(0, tm // ROW_DMA_UNROLL, issue, 0)

    def drain(g, carry):
        for _ in range(ROW_DMA_UNROLL * TOP_K):
            _row_copy(h_ref.at[pl.ds(0, 1), :], xs_hbm.at[pl.ds(0, 1), :], sem).wait()
        return carry

    lax.fori_loop(0, tm // ROW_DMA_UNROLL, drain, 0)


def _moe_dispatch(h_f32, dest, n_rows):
    t, d = h_f32.shape
    tm = min(MOE_TOK_TILE, t)
    dest3 = dest.reshape(t // tm, 1, TOP_K * tm)
    return pl.pallas_call(
        functools.partial(_dispatch_kernel, tm=tm),
        grid=(t // tm,),
        in_specs=[
            pl.BlockSpec((1, 1, TOP_K * tm), lambda i: (i, 0, 0), memory_space=pltpu.SMEM),
            pl.BlockSpec((tm, d), lambda i: (i, 0)),
            pl.BlockSpec(memory_space=pl.ANY),
        ],
        out_specs=pl.BlockSpec(memory_space=pl.ANY),
        out_shape=jax.ShapeDtypeStruct((n_rows, d), F32),
        scratch_shapes=[pltpu.SemaphoreType.DMA(())],
        input_output_aliases={2: 0},
        compiler_params=_cparams(("arbitrary",)),
        name="moe_dispatch",
    )(dest3, h_f32, jnp.zeros((n_rows, d), F32))


def _experts_kernel(be_ref, nused_ref, x_ref, wg_ref, wu_ref, wd_ref, y_ref):
    i = pl.program_id(0)
    f = pl.program_id(1)

    @pl.when(f == 0)
    def _zero():
        y_ref[...] = jnp.zeros(y_ref.shape, F32)

    @pl.when(i < nused_ref[0])
    def _compute():
        xb = x_ref[...].astype(BF16)
        act = _silu(jnp.dot(xb, wg_ref[0], preferred_element_type=F32)) * jnp.dot(
            xb, wu_ref[0], preferred_element_type=F32)
        y_ref[...] += jnp.dot(act.astype(BF16), wd_ref[0], preferred_element_type=F32)


def _moe_experts(xs, block_expert, n_used, w_gate, w_up, w_down):
    n_rows, d = xs.shape
    dff = w_gate.shape[2]
    bm = MOE_ROWS
    tf = MOE_FF_TILE if dff % MOE_FF_TILE == 0 else dff

    def ff(i, f, nu):
        return jnp.where(i < nu[0], f, 0)

    return pl.pallas_call(
        _experts_kernel,
        grid_spec=pltpu.PrefetchScalarGridSpec(
            num_scalar_prefetch=2,
            grid=(n_rows // bm, dff // tf),
            in_specs=[
                pl.BlockSpec((bm, d), lambda i, f, be, nu: (i, 0)),
                pl.BlockSpec((1, d, tf), lambda i, f, be, nu: (be[i], 0, ff(i, f, nu))),
                pl.BlockSpec((1, d, tf), lambda i, f, be, nu: (be[i], 0, ff(i, f, nu))),
                pl.BlockSpec((1, tf, d), lambda i, f, be, nu: (be[i], ff(i, f, nu), 0)),
            ],
            out_specs=pl.BlockSpec((bm, d), lambda i, f, be, nu: (i, 0)),
        ),
        out_shape=jax.ShapeDtypeStruct((n_rows, d), F32),
        compiler_params=_cparams(("parallel", "arbitrary")),
        name="moe_experts",
    )(block_expert, n_used, xs, w_gate.astype(BF16), w_up.astype(BF16), w_down.astype(BF16))


def _combine_kernel(dest_ref, dnext_ref, x1_ref, route_ref, gate_ref, fn_ref, ys_hbm, o_ref, buf_ref, sem,
                    *, tm, final):
    i = pl.program_id(0)
    slot = lax.rem(i, 2)

    def gather(d_ref, s):
        def issue(g, carry):
            for j in range(ROW_DMA_UNROLL):
                t = g * ROW_DMA_UNROLL + j
                for kk in range(TOP_K):
                    d = d_ref[0, 0, TOP_K * t + kk]
                    _row_copy(ys_hbm.at[pl.ds(d, 1), :], buf_ref.at[s, kk, pl.ds(t, 1), :],
                              sem.at[s]).start(priority=kk)
            return carry

        lax.fori_loop(0, tm // ROW_DMA_UNROLL, issue, 0)

    @pl.when(i == 0)
    def _first_tile():
        gather(dest_ref, slot)

    @pl.when(i + 1 < pl.num_programs(0))
    def _next_tile():
        gather(dnext_ref, 1 - slot)

    def drain(g, carry):
        for _ in range(ROW_DMA_UNROLL * TOP_K):
            _row_copy(ys_hbm.at[pl.ds(0, 1), :], buf_ref.at[slot, 0, pl.ds(0, 1), :], sem.at[slot]).wait()
        return carry

    lax.fori_loop(0, tm // ROW_DMA_UNROLL, drain, 0)
    route = route_ref[...]
    w0 = route[:, ROUTE_W0:ROUTE_W0 + 1]
    w1 = route[:, ROUTE_W1:ROUTE_W1 + 1]
    f = w0 * buf_ref[slot, 0] + w1 * buf_ref[slot, 1]
    o_ref[...] = _finish(x1_ref[...] + gate_ref[0] * f, final, fn_ref)


def _moe_combine(ys, dest, x1, route, gate, final_norm, final, seq):
    t, d = x1.shape
    tm = min(MOE_TOK_TILE, seq)
    per_b = seq // tm
    dest3 = dest.reshape(t // tm, 1, TOP_K * tm)
    fn = final_norm.reshape(1, d).astype(F32)
    nt = t // tm
    return pl.pallas_call(
        functools.partial(_combine_kernel, tm=tm, final=final),
        grid=(nt,),
        in_specs=[
            pl.BlockSpec((1, 1, TOP_K * tm), lambda i: (i, 0, 0), memory_space=pltpu.SMEM),
            pl.BlockSpec((1, 1, TOP_K * tm), lambda i: (jnp.minimum(i + 1, nt - 1), 0, 0),
                         memory_space=pltpu.SMEM),
            pl.BlockSpec((tm, d), lambda i: (i, 0)),
            pl.BlockSpec((tm, LANES), lambda i: (i, 0)),
            pl.BlockSpec((1, 1, d), lambda i: (i // per_b, 0, 0)),
            pl.BlockSpec((1, d), lambda i: (0, 0)),
            pl.BlockSpec(memory_space=pl.ANY),
        ],
        out_specs=pl.BlockSpec((tm, d), lambda i: (i, 0)),
        out_shape=jax.ShapeDtypeStruct((t, d), F32),
        scratch_shapes=[pltpu.VMEM((2, TOP_K, tm, d), F32), pltpu.SemaphoreType.DMA((2,))],
        compiler_params=_cparams(("arbitrary",)),
        name="moe_combine",
    )(dest3, dest3, x1, route, gate, fn, ys)


def _moe_ffn(h_f32, x1, route, counts_slab, gate, w_gate, w_up, w_down, final_norm, final, seq):
    t, d = x1.shape
    bm = MOE_ROWS
    n_blocks = (t * TOP_K + bm - 1) // bm + N_EXPERTS
    n_rows = n_blocks * bm
    counts = counts_slab[0, :N_EXPERTS].astype(I32)
    padded = ((counts + bm - 1) // bm) * bm
    pend = jnp.cumsum(padded)
    pstart = pend - padded
    e = route[:, ROUTE_E0:ROUTE_E1 + 1].astype(I32)
    rank = route[:, ROUTE_R0:ROUTE_R1 + 1].astype(I32)
    dest = (jnp.sum(jnp.where(e[..., None] == jnp.arange(N_EXPERTS), pstart, 0), axis=-1) + rank)
    blk_start = jnp.arange(n_blocks, dtype=I32) * bm
    block_expert = jnp.minimum(jnp.sum(blk_start[:, None] >= pend[None, :], axis=1), N_EXPERTS - 1).astype(I32)
    n_used = (pend[-1:] // bm).astype(I32)
    xs = _moe_dispatch(h_f32, dest, n_rows)
    ys = _moe_experts(xs, block_expert, n_used, w_gate, w_up, w_down)
    return _moe_combine(ys, dest, x1, route, gate, final_norm, final, seq)


def kernel(x, c, ada_w, ada_b, w_in, mla_q_norm, mla_w_uq, mla_kv_norm, mla_w_ukv, ssd_conv_w, ssd_conv_b, ssd_dt_bias, ssd_a_log, ssd_d, ssd_norm, gdn_conv_w, gdn_conv_b, gdn_dt_bias, gdn_a_log, gdn_norm, w_out, ffn_w_gate, ffn_w_up, ffn_w_down, moe_router, moe_w_gate, moe_w_up, moe_w_down, final_norm):
    bsz, seq, d = x.shape
    depth = w_in.shape[0]
    t = bsz * seq
    mod = _ada_modulation(c, ada_w, ada_b)
    tables = _rope_tables(seq)
    xf = x.reshape(t, d)
    for l in range(depth):
        sh1, sc1, g1, sh2, sc2, g2 = [m.reshape(bsz, 1, d) for m in jnp.split(mod[l], 6, axis=-1)]
        mla_c, misc, ssd_z, ssd_xbc, gdn_qkv, gdn_z = _in_proj(xf, sh1, sc1, _prep_w_in(w_in[l]), seq)
        wqa, wqb, wkv = _prep_mla_weights(mla_w_uq[l], mla_w_ukv[l])
        qt, k, vt = _mla_proj(mla_c, misc, tables, mla_q_norm[l], mla_kv_norm[l], wqa, wqb, wkv, bsz, seq)
        mla_o = _flash_attention(qt, k, vt, bsz, seq)
        ssd_y = _ssd_mixer(ssd_z, ssd_xbc, misc, ssd_conv_w[l], ssd_conv_b[l], ssd_dt_bias[l],
                           ssd_a_log[l], ssd_d[l], ssd_norm[l], bsz, seq)
        gdn_o = _gdn_mixer(gdn_qkv, gdn_z, misc, gdn_conv_w[l], gdn_conv_b[l], gdn_dt_bias[l],
                           gdn_a_log[l], gdn_norm[l], bsz, seq)
        final = l == depth - 1
        if l % 2 == 0:
            x1, h2 = _post_mixer(xf, mla_o, ssd_y, gdn_o, w_out[l], g1, sh2, sc2, seq)
            xf = _dense_ffn(h2, x1, g2, ffn_w_gate[l // 2], ffn_w_up[l // 2], ffn_w_down[l // 2],
                            final_norm, final, seq)
        else:
            x1, h2, route, counts = _post_mixer(xf, mla_o, ssd_y, gdn_o, w_out[l], g1, sh2, sc2, seq,
                                                w_router=moe_router[l // 2])
            xf = _moe_ffn(h2, x1, route, counts, g2, moe_w_gate[l // 2], moe_w_up[l // 2],
                          moe_w_down[l // 2], final_norm, final, seq)
    return xf.reshape(bsz, seq, d)
```

```python
import functools
import math

import jax
import jax.numpy as jnp
import numpy as np
from jax import lax
from jax.experimental import pallas as pl
from jax.experimental.pallas import tpu as pltpu

F32 = jnp.float32
BF16 = jnp.bfloat16
I32 = jnp.int32
HIGHEST = lax.Precision.HIGHEST

D_MODEL = 1024
EPS = 1e-6
CONV_WIDTH = 4
MLA_HEADS, MLA_Q_RANK, MLA_KV_RANK = 4, 256, 128
MLA_NOPE, MLA_ROPE, MLA_V = 64, 32, 64
ROPE_THETA = 10000.0
SSD_HEADS, SSD_HEAD_DIM, SSD_GROUPS, SSD_STATE, SSD_CHUNK = 8, 64, 2, 64, 128
SSD_D_INNER = SSD_HEADS * SSD_HEAD_DIM
SSD_CONV_DIM = SSD_D_INNER + 2 * SSD_GROUPS * SSD_STATE
GDN_HEADS, GDN_DK, GDN_DV, GDN_CHUNK = 4, 64, 64, 64
GDN_QKV = GDN_HEADS * (2 * GDN_DK + GDN_DV)
GDN_OUT = GDN_HEADS * GDN_DV
MLA_IN = MLA_Q_RANK + MLA_KV_RANK + MLA_ROPE
MLA_OUT = MLA_HEADS * MLA_V
SSD_IN = SSD_D_INNER + SSD_CONV_DIM + SSD_HEADS
GDN_IN = GDN_QKV + GDN_HEADS * GDN_DV + 2 * GDN_HEADS
N_EXPERTS, TOP_K = 8, 2

LANES = 128
SUBLANES = 8
HALF = LANES // 2
VMEM_LIMIT_BYTES = 56 * 1024 * 1024

MISC_DT = 0
MISC_B = 8
MISC_A = 12
MISC_KR = 64
MISC_KR_SW = 96
HEAD_PAD = 128

TM_PROJ = 1024
IN_PROJ_SUB = 512
TM_FFN = 512
ATT_Q_BLOCK = 1024
ATT_K_BLOCK = 2048
ATT_Q_COLS = 256
ATT_K_ROWS = 512
ATT_PIPELINE = 3
SSD_STEP_CHUNKS = 2
GDN_STEP_CHUNKS = 8
MOE_ROWS = 512
MOE_FF_TILE = 1792
MOE_TOK_TILE = 512
ROUTE_TILE = 512
ROW_DMA_UNROLL = 8


def _cparams(sem):
    return pltpu.CompilerParams(dimension_semantics=sem, vmem_limit_bytes=VMEM_LIMIT_BYTES)


def _lane_iota(shape):
    return lax.broadcasted_iota(I32, shape, len(shape) - 1)


def _row_iota(shape):
    return lax.broadcasted_iota(I32, shape, len(shape) - 2)


def _softplus(x):
    return jnp.maximum(x, 0.0) + jnp.log1p(jnp.exp(-jnp.abs(x)))


def _silu(x):
    return x * jax.nn.sigmoid(x)


def _rms(x):
    return x * lax.rsqrt(jnp.mean(x * x, axis=-1, keepdims=True) + EPS)


def _dot(a, b):
    return jnp.dot(a.astype(BF16), b.astype(BF16), preferred_element_type=F32)


def _dot_nt(a, b):
    return lax.dot_general(a.astype(BF16), b.astype(BF16), (((1,), (1,)), ((), ())),
                           preferred_element_type=F32)


def _dot_tn(a, b):
    return lax.dot_general(a.astype(BF16), b.astype(BF16), (((0,), (0,)), ((), ())),
                           preferred_element_type=F32)


def _dot_f32(a, b):
    return jnp.dot(a, b, precision=HIGHEST, preferred_element_type=F32)


def _ada_kernel(c_ref, w_ref, b_ref, o_ref):
    c_act = _silu(c_ref[...])
    o_ref[0] = _dot_f32(c_act, w_ref[0]) + b_ref[0]


def _ada_modulation(c, ada_w, ada_b):
    depth, d, n = ada_w.shape
    bsz = c.shape[0]
    rows = max(SUBLANES, bsz)
    c_pad = jnp.zeros((rows, d), F32).at[:bsz].set(c)
    tn = 1536
    out = pl.pallas_call(
        _ada_kernel,
        grid=(depth, n // tn),
        in_specs=[
            pl.BlockSpec((rows, d), lambda l, j: (0, 0)),
            pl.BlockSpec((1, d, tn), lambda l, j: (l, 0, j)),
            pl.BlockSpec((1, 1, tn), lambda l, j: (l, 0, j)),
        ],
        out_specs=pl.BlockSpec((1, rows, tn), lambda l, j: (l, 0, j)),
        out_shape=jax.ShapeDtypeStruct((depth, rows, n), F32),
        compiler_params=_cparams(("parallel", "parallel")),
        name="ada_modulation",
    )(c_pad, ada_w, ada_b.reshape(depth, 1, n))
    return out[:, :bsz]


IN_SLABS = (("mla_c", 384, F32), ("misc", 128, F32), ("ssd_z", 512, BF16), ("ssd_xbc", 768, BF16),
            ("gdn_qkv", 768, BF16), ("gdn_z", 256, BF16))
IN_WIDTH = sum(w for _, w, _ in IN_SLABS)


def _prep_w_in(w):
    d = w.shape[0]
    o_ssd = MLA_IN
    o_gdn = MLA_IN + SSD_IN
    w_kr = w[:, MLA_Q_RANK + MLA_KV_RANK:MLA_IN]
    half = MLA_ROPE // 2
    w_kr_sw = jnp.concatenate([-w_kr[:, half:], w_kr[:, :half]], axis=1)
    ssd_dt = w[:, o_ssd + SSD_D_INNER + SSD_CONV_DIM:o_ssd + SSD_IN]
    gdn_ba = w[:, o_gdn + GDN_QKV + GDN_OUT:o_gdn + GDN_IN]
    misc = jnp.concatenate(
        [ssd_dt, gdn_ba, jnp.zeros((d, MISC_KR - MISC_A - GDN_HEADS), w.dtype), w_kr, w_kr_sw], axis=1)
    cols = [
        w[:, :MLA_Q_RANK + MLA_KV_RANK], misc,
        w[:, o_ssd:o_ssd + SSD_D_INNER],
        w[:, o_ssd + SSD_D_INNER:o_ssd + SSD_D_INNER + SSD_CONV_DIM],
        w[:, o_gdn:o_gdn + GDN_QKV],
        w[:, o_gdn + GDN_QKV:o_gdn + GDN_QKV + GDN_OUT],
    ]
    return jnp.concatenate(cols, axis=1).astype(BF16)


def _in_proj_kernel(x_ref, sh_ref, sc_ref, w_ref, *o_refs):
    rows = x_ref.shape[0]
    sub = min(IN_PROJ_SUB, rows)

    def normed(r0):
        return (_rms(x_ref[r0:r0 + sub, :]) * (1.0 + sc_ref[0]) + sh_ref[0]).astype(BF16)

    nxt = normed(0)
    for r0 in range(0, rows, sub):
        hb = nxt
        if r0 + sub < rows:
            nxt = normed(r0 + sub)
        off = 0
        for o_ref, (_, width, _) in zip(o_refs, IN_SLABS):
            o_ref[r0:r0 + sub, :] = jnp.dot(
                hb, w_ref[:, off:off + width], preferred_element_type=F32).astype(o_ref.dtype)
            off += width


def _in_proj(x2d, shift, scale, w_perm, seq):
    t, d = x2d.shape
    tm = min(TM_PROJ, seq)
    per_b = seq // tm
    return pl.pallas_call(
        _in_proj_kernel,
        grid=(t // tm,),
        in_specs=[
            pl.BlockSpec((tm, d), lambda i: (i, 0)),
            pl.BlockSpec((1, 1, d), lambda i: (i // per_b, 0, 0)),
            pl.BlockSpec((1, 1, d), lambda i: (i // per_b, 0, 0)),
            pl.BlockSpec((d, IN_WIDTH), lambda i: (0, 0)),
        ],
        out_specs=[pl.BlockSpec((tm, w), lambda i: (i, 0)) for _, w, _ in IN_SLABS],
        out_shape=[jax.ShapeDtypeStruct((t, w), dt) for _, w, dt in IN_SLABS],
        compiler_params=_cparams(("parallel",)),
        name="in_proj",
    )(x2d, shift, scale, w_perm)


def _rope_tables(seq):
    pos = jnp.arange(seq, dtype=F32)
    inv_freq = ROPE_THETA ** (-jnp.arange(0, MLA_ROPE, 2, dtype=F32) / MLA_ROPE)
    ang = pos[:, None] * inv_freq[None, :]
    cos, sin = jnp.cos(ang), jnp.sin(ang)
    zeros_l = jnp.zeros((seq, MLA_NOPE), F32)
    zeros_r = jnp.zeros((seq, HEAD_PAD - MLA_NOPE - MLA_ROPE), F32)
    cos_k = jnp.concatenate([zeros_l, cos, cos, zeros_r], axis=1)
    sin_k = jnp.concatenate([zeros_l, sin, sin, zeros_r], axis=1)
    scale = (MLA_NOPE + MLA_ROPE) ** -0.5 * math.log2(math.e)
    cos_q = scale * jnp.concatenate([jnp.ones((seq, MLA_NOPE), F32), cos, cos, zeros_r], axis=1)
    sin_q = scale * sin_k
    return cos_q, sin_q, cos_k, sin_k


def _prep_mla_weights(w_uq, w_ukv):
    r = w_uq.shape[0]
    hq = MLA_NOPE + MLA_ROPE
    half = MLA_ROPE // 2
    qa, qb = [], []
    for h in range(MLA_HEADS):
        nope = w_uq[:, h * hq:h * hq + MLA_NOPE]
        rope = w_uq[:, h * hq + MLA_NOPE:(h + 1) * hq]
        rope_sw = jnp.concatenate([-rope[:, half:], rope[:, :half]], axis=1)
        pad = jnp.zeros((r, HEAD_PAD - hq), w_uq.dtype)
        qa += [nope, rope, pad]
        qb += [jnp.zeros((r, MLA_NOPE), w_uq.dtype), rope_sw, pad]
    rk = w_ukv.shape[0]
    hk = MLA_NOPE + MLA_V
    kcols, vcols = [], []
    for h in range(MLA_HEADS):
        kcols += [w_ukv[:, h * hk:h * hk + MLA_NOPE], jnp.zeros((rk, HEAD_PAD - MLA_NOPE), w_ukv.dtype)]
        vcols += [w_ukv[:, h * hk + MLA_NOPE:(h + 1) * hk], jnp.zeros((rk, HEAD_PAD - MLA_V), w_ukv.dtype)]
    return (jnp.concatenate(qa, axis=1).astype(BF16), jnp.concatenate(qb, axis=1).astype(BF16),
            jnp.concatenate(kcols + vcols, axis=1).astype(BF16))


def _mla_proj_kernel(c_ref, misc_ref, cq_ref, sq_ref, ck_ref, sk_ref, qn_ref, kvn_ref,
                     wqa_ref, wqb_ref, wkv_ref, qt_ref, k_ref, vt_ref):
    c = c_ref[...]
    cq = (_rms(c[:, :MLA_Q_RANK]) * qn_ref[...]).astype(BF16)
    ckv = (_rms(c[:, MLA_Q_RANK:]) * kvn_ref[...]).astype(BF16)
    qa = jnp.dot(cq, wqa_ref[...], preferred_element_type=F32)
    qb = jnp.dot(cq, wqb_ref[...], preferred_element_type=F32)
    cos_q = jnp.concatenate([cq_ref[...]] * MLA_HEADS, axis=1)
    sin_q = jnp.concatenate([sq_ref[...]] * MLA_HEADS, axis=1)
    qt_ref[0] = (qa * cos_q + qb * sin_q).T.astype(BF16)
    kv = jnp.dot(ckv, wkv_ref[...], preferred_element_type=F32)
    misc = misc_ref[...]
    kr = misc * ck_ref[...] + pltpu.roll(misc, LANES - (MISC_KR_SW - MISC_KR), 1) * sk_ref[...]
    kw = MLA_HEADS * HEAD_PAD
    k_ref[0] = (kv[:, :kw] + jnp.concatenate([kr] * MLA_HEADS, axis=1)).astype(BF16)
    v = kv[:, kw:]
    v = jnp.where(_lane_iota(v.shape) % HEAD_PAD == MLA_V, 1.0, v)
    vt_ref[0] = v.T.astype(BF16)


def _mla_proj(mla_c, misc, tables, q_norm, kv_norm, wqa, wqb, wkv, bsz, seq):
    t = mla_c.shape[0]
    tm = min(TM_PROJ, seq)
    per_b = seq // tm
    kw = MLA_HEADS * HEAD_PAD
    tab_spec = pl.BlockSpec((tm, HEAD_PAD), lambda i: (i % per_b, 0))
    full = lambda a: pl.BlockSpec(a.shape, lambda i: (0,) * a.ndim)
    qn = q_norm.reshape(1, -1)
    kvn = kv_norm.reshape(1, -1)
    t_spec = pl.BlockSpec((1, kw, tm), lambda i: (i // per_b, 0, i % per_b))
    return pl.pallas_call(
        _mla_proj_kernel,
        grid=(t // tm,),
        in_specs=[
            pl.BlockSpec((tm, mla_c.shape[1]), lambda i: (i, 0)),
            pl.BlockSpec((tm, LANES), lambda i: (i, 0)),
            tab_spec, tab_spec, tab_spec, tab_spec,
            full(qn), full(kvn), full(wqa), full(wqb), full(wkv),
        ],
        out_specs=[t_spec, pl.BlockSpec((1, tm, kw), lambda i: (i // per_b, i % per_b, 0)), t_spec],
        out_shape=[jax.ShapeDtypeStruct((bsz, kw, seq), BF16), jax.ShapeDtypeStruct((bsz, seq, kw), BF16),
                   jax.ShapeDtypeStruct((bsz, kw, seq), BF16)],
        compiler_params=_cparams(("parallel",)),
        name="mla_proj",
    )(mla_c, misc, *tables, qn, kvn, wqa, wqb, wkv)


def _flash_diag_offsets(qb, kb):
    return sorted({((qi * qb) // kb) * kb - qi * qb for qi in range(max(kb // qb, 1))})


def _flash_kernel(qi_tab, ki_tab, kind_tab, qt_ref, k_ref, vt_ref, o_ref, m_ref, acc_ref, *, qb, kb, rq, ck):
    del qi_tab
    p = pl.program_id(1)
    ki = ki_tab[p]
    kind = kind_tab[p]

    @pl.when(ki == 0)
    def _init():
        m_ref[...] = jnp.full(m_ref.shape, -jnp.inf, F32)
        acc_ref[...] = jnp.zeros(acc_ref.shape, F32)

    def step(rel):
        units = []
        for rc in range(qb // rq):
            for kc in range(kb // ck):
                if rel is not None and rel + kc * ck > rc * rq + rq - 1:
                    continue
                for h in range(MLA_HEADS):
                    units.append((h, rc * rq, kc * ck))

        def scores(unit):
            h, c0, k0 = unit
            kk = k_ref[0, k0:k0 + ck, h * HEAD_PAD:(h + 1) * HEAD_PAD]
            qt = qt_ref[0, h * HEAD_PAD:(h + 1) * HEAD_PAD, c0:c0 + rq]
            s = jnp.dot(kk, qt, preferred_element_type=F32)
            if rel is not None and rel + k0 + ck - 1 > c0:
                key = lax.broadcasted_iota(I32, s.shape, 0) + (k0 + rel)
                qry = lax.broadcasted_iota(I32, s.shape, 1) + c0
                s = jnp.where(key <= qry, s, -jnp.inf)
            return s

        def update(unit, s):
            h, c0, k0 = unit
            m_prev = m_ref[h, :, c0:c0 + rq]
            m_new = jnp.maximum(m_prev, jnp.max(s, axis=0, keepdims=True))
            alpha = jnp.exp2(m_prev - m_new)
            pt = jnp.exp2(s - m_new).astype(BF16)
            m_ref[h, :, c0:c0 + rq] = m_new
            acc_ref[h, :, c0:c0 + rq] = acc_ref[h, :, c0:c0 + rq] * alpha + jnp.dot(
                vt_ref[0, h * HEAD_PAD:(h + 1) * HEAD_PAD, k0:k0 + ck], pt, preferred_element_type=F32)

        pending = [scores(u) for u in units[:ATT_PIPELINE]]
        for n, unit in enumerate(units):
            s = pending.pop(0)
            if n + ATT_PIPELINE < len(units):
                pending.append(scores(units[n + ATT_PIPELINE]))
            update(unit, s)

    @pl.when(kind == 0)
    def _off_diagonal():
        step(None)

    for n, rel in enumerate(_flash_diag_offsets(qb, kb)):
        @pl.when(kind == n + 1)
        def _diagonal(rel=rel):
            step(rel)
            for h in range(MLA_HEADS):
                acc = acc_ref[h]
                o_ref[0, h * MLA_V:(h + 1) * MLA_V, :] = (
                    acc[:MLA_V] / acc[MLA_V:MLA_V + 1]).astype(o_ref.dtype)


def _flash_attention(qt, k, vt, bsz, seq):
    qb = min(ATT_Q_BLOCK, seq)
    kb = min(ATT_K_BLOCK, seq)
    rq = min(ATT_Q_COLS, qb)
    ck = min(ATT_K_ROWS, kb)
    offsets = _flash_diag_offsets(qb, kb)
    qi_l, ki_l, kind_l = [], [], []
    for qi in range(seq // qb):
        for kj in range((qi * qb + qb - 1) // kb + 1):
            qi_l.append(qi)
            ki_l.append(kj)
            visible = (kj + 1) * kb - 1 <= qi * qb
            kind_l.append(0 if visible else 1 + offsets.index(kj * kb - qi * qb))
    tabs = [jnp.asarray(x, I32) for x in (qi_l, ki_l, kind_l)]
    kw = MLA_HEADS * HEAD_PAD
    return pl.pallas_call(
        functools.partial(_flash_kernel, qb=qb, kb=kb, rq=rq, ck=ck),
        grid_spec=pltpu.PrefetchScalarGridSpec(
            num_scalar_prefetch=3,
            grid=(bsz, len(qi_l)),
            in_specs=[
                pl.BlockSpec((1, kw, qb), lambda b, p, qt_, kt_, kd_: (b, 0, qt_[p])),
                pl.BlockSpec((1, kb, kw), lambda b, p, qt_, kt_, kd_: (b, kt_[p], 0)),
                pl.BlockSpec((1, kw, kb), lambda b, p, qt_, kt_, kd_: (b, 0, kt_[p])),
            ],
            out_specs=pl.BlockSpec((1, MLA_OUT, qb), lambda b, p, qt_, kt_, kd_: (b, 0, qt_[p])),
            scratch_shapes=[pltpu.VMEM((MLA_HEADS, 1, qb), F32),
                            pltpu.VMEM((MLA_HEADS, HEAD_PAD, qb), F32)],
        ),
        out_shape=jax.ShapeDtypeStruct((bsz, MLA_OUT, seq), BF16),
        compiler_params=_cparams(("parallel", "arbitrary")),
        name="mla_flash",
    )(*tabs, qt, k, vt)


def _causal_conv_silu(cur, ext_ref, halo_ref, cw_ref, cb_ref, first):
    rows = cur.shape[0]

    @pl.when(first)
    def _zero_halo():
        halo_ref[...] = jnp.zeros(halo_ref.shape, F32)

    ext_ref[0:SUBLANES, :] = halo_ref[...]
    ext_ref[SUBLANES:SUBLANES + rows, :] = cur
    halo_ref[...] = cur[rows - SUBLANES:, :]
    acc = cb_ref[...] + cw_ref[CONV_WIDTH - 1:CONV_WIDTH, :] * cur
    for j in range(CONV_WIDTH - 1):
        start = SUBLANES - (CONV_WIDTH - 1) + j
        acc = acc + cw_ref[j:j + 1, :] * ext_ref[start:start + rows, :]
    return _silu(acc)


def _ssd_kernel(z_ref, xbc_ref, misc_ref, cw_ref, cb_ref, dtb_ref, aneg_ref, dsk_ref, nw_ref,
                y_ref, ext_ref, halo_ref, st_ref, *, nchunk, bsz):
    first = pl.program_id(0) == 0
    length = SSD_CHUNK

    @pl.when(first)
    def _zero_state():
        st_ref[...] = jnp.zeros(st_ref.shape, F32)

    tri = (_lane_iota((length, length)) <= _row_iota((length, length))).astype(F32)
    lower = _lane_iota((length, length)) <= _row_iota((length, length))
    lane = _lane_iota((length, LANES))
    lo = lane < HALF
    bw = SSD_GROUPS * SSD_STATE
    npair = SSD_HEADS // 2
    for b in range(bsz):
        _ssd_rows(b, z_ref, xbc_ref, misc_ref, cw_ref, cb_ref, dtb_ref, aneg_ref, dsk_ref, nw_ref, y_ref,
                  ext_ref.at[b], halo_ref.at[b], st_ref, b * npair, first, nchunk, tri, lower, lo, bw)


def _ssd_rows(b, z_ref, xbc_ref, misc_ref, cw_ref, cb_ref, dtb_ref, aneg_ref, dsk_ref, nw_ref, y_ref,
              ext_ref, halo_ref, st_ref, st0, first, nchunk, tri, lower, lo, bw):
    length = SSD_CHUNK
    xbc = _causal_conv_silu(xbc_ref[b].astype(F32), ext_ref, halo_ref, cw_ref, cb_ref, first)
    dt_all = _softplus(misc_ref[b] + dtb_ref[...])
    a_all = dt_all * aneg_ref[...]

    for c in range(nchunk):
        r0 = c * length
        dt = dt_all[r0:r0 + length]
        acum = _dot_f32(tri, a_all[r0:r0 + length])
        acum_t = acum.T
        x_c = xbc[r0:r0 + length, :SSD_D_INNER]
        b_c = xbc[r0:r0 + length, SSD_D_INNER:SSD_D_INNER + bw]
        c_c = xbc[r0:r0 + length, SSD_D_INNER + bw:]
        y_parts = []
        for g in range(SSD_GROUPS):
            b_g = b_c[:, g * SSD_STATE:(g + 1) * SSD_STATE]
            c_g = c_c[:, g * SSD_STATE:(g + 1) * SSD_STATE].astype(BF16)
            b_gt = b_g.T.astype(BF16)
            cb = jnp.dot(c_g, b_gt, preferred_element_type=F32)
            pairs_per_group = SSD_HEADS // SSD_GROUPS // 2
            for pp in range(pairs_per_group):
                pr = g * pairs_per_group + pp
                h0, h1 = 2 * pr, 2 * pr + 1
                col0, col1 = acum[:, h0:h0 + 1], acum[:, h1:h1 + 1]
                l0 = jnp.exp(jnp.where(lower, col0 - acum_t[h0:h0 + 1, :], -jnp.inf))
                l1 = jnp.exp(jnp.where(lower, col1 - acum_t[h1:h1 + 1, :], -jnp.inf))
                mmat = jnp.concatenate([cb * l0, cb * l1], axis=1).astype(BF16)
                xp = x_c[:, pr * LANES:(pr + 1) * LANES]
                xdt = xp * jnp.where(lo, dt[:, h0:h0 + 1], dt[:, h1:h1 + 1])
                rhs = jnp.concatenate([jnp.where(lo, xdt, 0.0), jnp.where(lo, 0.0, xdt)], axis=0)
                y_diag = jnp.dot(mmat, rhs.astype(BF16), preferred_element_type=F32)
                col_pair = jnp.where(lo, col0, col1)
                last_pair = jnp.where(lo[:1], acum[length - 1:length, h0:h0 + 1],
                                      acum[length - 1:length, h1:h1 + 1])
                st = st_ref[st0 + pr]
                y_off = jnp.dot(c_g, st.astype(BF16), preferred_element_type=F32) * jnp.exp(col_pair)
                xdec = (xdt * jnp.exp(last_pair - col_pair)).astype(BF16)
                st_ref[st0 + pr] = st * jnp.exp(last_pair) + jnp.dot(b_gt, xdec, preferred_element_type=F32)
                y_parts.append(y_diag + y_off + xp * dsk_ref[:, pr * LANES:(pr + 1) * LANES])
        y = jnp.concatenate(y_parts, axis=1) * _silu(z_ref[b, r0:r0 + length, :].astype(F32))
        gw = SSD_D_INNER // SSD_GROUPS
        y = jnp.concatenate([_rms(y[:, g * gw:(g + 1) * gw]) for g in range(SSD_GROUPS)], axis=1)
        y_ref[b, r0:r0 + length, :] = (y * nw_ref[...]).astype(y_ref.dtype)


def _ssd_mixer(z, xbc, misc, conv_w, conv_b, dt_bias, a_log, d_skip, norm_w, bsz, seq):
    nchunk = min(SSD_STEP_CHUNKS, seq // SSD_CHUNK)
    rows = nchunk * SSD_CHUNK
    pad = jnp.zeros((LANES - SSD_HEADS,), F32)
    dtb = jnp.concatenate([dt_bias.astype(F32), pad]).reshape(1, LANES)
    aneg = jnp.concatenate([-jnp.exp(a_log.astype(F32)), pad]).reshape(1, LANES)
    dsk = jnp.repeat(d_skip.astype(F32), SSD_HEAD_DIM).reshape(1, SSD_D_INNER)
    full = lambda a: pl.BlockSpec(a.shape, lambda i: (0,) * a.ndim)
    cb2 = conv_b.reshape(1, -1)
    nw2 = norm_w.reshape(1, -1)
    blk = lambda w: pl.BlockSpec((bsz, rows, w), lambda i: (0, i, 0))
    out = pl.pallas_call(
        functools.partial(_ssd_kernel, nchunk=nchunk, bsz=bsz),
        grid=(seq // rows,),
        in_specs=[blk(SSD_D_INNER), blk(SSD_CONV_DIM), blk(LANES),
                  full(conv_w), full(cb2), full(dtb), full(aneg), full(dsk), full(nw2)],
        out_specs=blk(SSD_D_INNER),
        out_shape=jax.ShapeDtypeStruct((bsz, seq, SSD_D_INNER), BF16),
        scratch_shapes=[pltpu.VMEM((bsz, rows + SUBLANES, SSD_CONV_DIM), F32),
                        pltpu.VMEM((bsz, SUBLANES, SSD_CONV_DIM), F32),
                        pltpu.VMEM((bsz * (SSD_HEADS // 2), SSD_STATE, LANES), F32)],
        compiler_params=_cparams(("arbitrary",)),
        name="ssd_mixer",
    )(z.reshape(bsz, seq, -1), xbc.reshape(bsz, seq, -1), misc.reshape(bsz, seq, -1),
      conv_w, cb2, dtb, aneg, dsk, nw2)
    return out.reshape(bsz * seq, SSD_D_INNER)


def _half_sum(x, lo):
    s_lo = jnp.sum(jnp.where(lo, x, 0.0), axis=1, keepdims=True)
    s_hi = jnp.sum(jnp.where(lo, 0.0, x), axis=1, keepdims=True)
    return jnp.where(lo, s_lo, s_hi)


def _gdn_kernel(qkv_ref, z_ref, misc_ref, cw_ref, cb_ref, dtb_ref, aneg_ref, nw_ref,
                o_ref, ext_ref, halo_ref, st_ref, *, nchunk, bsz):
    first = pl.program_id(0) == 0
    length = GDN_CHUNK
    two = 2 * length

    @pl.when(first)
    def _zero_state():
        st_ref[...] = jnp.zeros(st_ref.shape, F32)

    tri = (_lane_iota((length, length)) <= _row_iota((length, length))).astype(F32)
    lane = _lane_iota((length, LANES))
    lo = lane < HALF
    r2 = _row_iota((two, two))
    c2 = _lane_iota((two, two))
    same_blk = (r2 < length) == (c2 < length)
    low_incl = same_blk & (c2 <= r2)
    low_strict = same_blk & (c2 < r2)
    eye = (r2 == c2).astype(F32)
    top_lo = (r2 < length) == (c2 < HALF)
    hk = GDN_HEADS * GDN_DK
    qscale = GDN_DK ** -0.5
    npair = GDN_HEADS // 2

    def blockdiag(slab):
        return jnp.where(top_lo, jnp.concatenate([slab, slab], axis=0), 0.0)

    def fold(bd):
        return bd[:length] + bd[length:]

    chains = []
    for b in range(bsz):
        qkv = _causal_conv_silu(qkv_ref[b].astype(F32), ext_ref.at[b], halo_ref.at[b], cw_ref, cb_ref, first)
        misc = misc_ref[b]
        beta_all = jax.nn.sigmoid(misc)
        g_all = aneg_ref[...] * _softplus(misc + dtb_ref[...])
        for c in range(nchunk):
            r0 = c * length
            gcum = _dot_f32(tri, g_all[r0:r0 + length])
            beta = beta_all[r0:r0 + length]
            for pr in range(npair):
                h0, h1 = 2 * pr, 2 * pr + 1
                q = qkv[r0:r0 + length, pr * LANES:(pr + 1) * LANES]
                k = qkv[r0:r0 + length, hk + pr * LANES:hk + (pr + 1) * LANES]
                v = qkv[r0:r0 + length, 2 * hk + pr * LANES:2 * hk + (pr + 1) * LANES]
                qn = q * lax.rsqrt(_half_sum(q * q, lo) + EPS) * qscale
                kn = k * lax.rsqrt(_half_sum(k * k, lo) + EPS)
                g0 = gcum[:, MISC_A + h0:MISC_A + h0 + 1]
                g1 = gcum[:, MISC_A + h1:MISC_A + h1 + 1]
                gexp = jnp.where(lo, g0, g1)
                bexp = jnp.where(lo, beta[:, MISC_B + h0:MISC_B + h0 + 1],
                                 beta[:, MISC_B + h1:MISC_B + h1 + 1])
                gcol = jnp.concatenate([jnp.broadcast_to(g0, (length, two)),
                                        jnp.broadcast_to(g1, (length, two))], axis=0)
                decay = jnp.exp(jnp.where(low_incl, gcol - gcol.T, -jnp.inf))
                kb = kn * bexp
                kk2 = jnp.concatenate([kn, kn], axis=0)
                kq = _dot_nt(jnp.concatenate([blockdiag(kb), blockdiag(qn)], axis=0), kk2)
                a_mat = jnp.where(low_strict, kq[:two] * decay, 0.0)
                qk = jnp.where(low_incl, kq[two:] * decay, 0.0)
                g_last = gexp[length - 1:length, :]
                chains.append(dict(
                    b=b, c=c, pr=pr, pw=a_mat, t=eye - a_mat, qk=qk,
                    vb=v * bexp, kbg=kb * jnp.exp(gexp), q_dec=qn * jnp.exp(gexp),
                    k_tail=kn * jnp.exp(g_last - gexp), gl=jnp.exp(g_last)))

    for _ in range(int(math.log2(length)) - 1):
        for ch in chains:
            ch["pw"] = _dot(ch["pw"], ch["pw"])
        for ch in chains:
            ch["t"] = ch["t"] + _dot(ch["t"], ch["pw"])

    for ch in chains:
        ku = _dot(ch["t"], jnp.concatenate([blockdiag(ch["kbg"]), blockdiag(ch["vb"])], axis=1))
        ch["k_cum"] = fold(ku[:, :LANES])
        ch["u"] = fold(ku[:, LANES:])
    for ch in chains:
        rhs = jnp.concatenate([ch["k_cum"], ch["u"]], axis=1)
        mn = _dot_tn(ch["k_tail"], rhs)
        ch["m"] = jnp.where(top_lo, mn[:, :LANES], 0.0)
        ch["n"] = jnp.where(top_lo, mn[:, LANES:], 0.0)
        qo = _dot(ch["qk"], jnp.concatenate([blockdiag(ch["k_cum"]), blockdiag(ch["u"])], axis=1))
        ch["q_eff"] = ch["q_dec"] - fold(qo[:, :LANES])
        ch["o_loc"] = fold(qo[:, LANES:])

    states = {(b, pr): st_ref[b * npair + pr] for b in range(bsz) for pr in range(npair)}
    for c in range(nchunk):
        for ch in chains:
            if ch["c"] != c:
                continue
            b, pr = ch["b"], ch["pr"]
            st = states[(b, pr)]
            both = _dot(jnp.concatenate([ch["q_eff"], ch["m"]], axis=0), st)
            o = both[:length] + ch["o_loc"]
            states[(b, pr)] = st * ch["gl"] - both[length:] + ch["n"]
            r0 = c * length
            ms = _half_sum(o * o, lo) * (1.0 / GDN_DV)
            zz = z_ref[b, r0:r0 + length, pr * LANES:(pr + 1) * LANES].astype(F32)
            out = o * lax.rsqrt(ms + EPS) * nw_ref[...] * _silu(zz)
            o_ref[b, r0:r0 + length, pr * LANES:(pr + 1) * LANES] = out.astype(o_ref.dtype)
    for (b, pr), st in states.items():
        st_ref[b * npair + pr] = st


def _gdn_mixer(qkv, z, misc, conv_w, conv_b, dt_bias, a_log, norm_w, bsz, seq):
    nchunk = min(GDN_STEP_CHUNKS, seq // GDN_CHUNK)
    rows = nchunk * GDN_CHUNK
    dtb = jnp.zeros((1, LANES), F32).at[0, MISC_A:MISC_A + GDN_HEADS].set(dt_bias.astype(F32))
    aneg = jnp.zeros((1, LANES), F32).at[0, MISC_A:MISC_A + GDN_HEADS].set(-jnp.exp(a_log.astype(F32)))
    nw2 = jnp.concatenate([norm_w.astype(F32)] * 2).reshape(1, LANES)
    cb2 = conv_b.reshape(1, -1)
    full = lambda a: pl.BlockSpec(a.shape, lambda i: (0,) * a.ndim)
    blk = lambda w: pl.BlockSpec((bsz, rows, w), lambda i: (0, i, 0))
    out = pl.pallas_call(
        functools.partial(_gdn_kernel, nchunk=nchunk, bsz=bsz),
        grid=(seq // rows,),
        in_specs=[blk(GDN_QKV), blk(GDN_OUT), blk(LANES),
                  full(conv_w), full(cb2), full(dtb), full(aneg), full(nw2)],
        out_specs=blk(GDN_OUT),
        out_shape=jax.ShapeDtypeStruct((bsz, seq, GDN_OUT), BF16),
        scratch_shapes=[pltpu.VMEM((bsz, rows + SUBLANES, GDN_QKV), F32),
                        pltpu.VMEM((bsz, SUBLANES, GDN_QKV), F32),
                        pltpu.VMEM((bsz * (GDN_HEADS // 2), 2 * GDN_DK, LANES), F32)],
        compiler_params=_cparams(("arbitrary",)),
        name="gdn_mixer",
    )(qkv.reshape(bsz, seq, -1), z.reshape(bsz, seq, -1), misc.reshape(bsz, seq, -1),
      conv_w, cb2, dtb, aneg, nw2)
    return out.reshape(bsz * seq, GDN_OUT)


ROUTE_E0, ROUTE_E1, ROUTE_R0, ROUTE_R1, ROUTE_W0, ROUTE_W1 = range(6)


def _post_kernel(*refs, route):
    if route:
        (x_ref, a_ref, s_ref, g_ref, wa_ref, ws_ref, wg_ref, gate_ref, sh_ref, sc_ref, wrh_ref, wrl_ref,
         x1_ref, h_ref, route_ref, cnt_ref, carry_ref) = refs
    else:
        (x_ref, a_ref, s_ref, g_ref, wa_ref, ws_ref, wg_ref, gate_ref, sh_ref, sc_ref,
         x1_ref, h_ref) = refs
    mix = (lax.dot_general(a_ref[0], wa_ref[...], (((0,), (0,)), ((), ())), preferred_element_type=F32)
           + jnp.dot(s_ref[...], ws_ref[...], preferred_element_type=F32)
           + jnp.dot(g_ref[...], wg_ref[...], preferred_element_type=F32))
    x1 = x_ref[...] + gate_ref[0] * mix
    x1_ref[...] = x1
    h = _rms(x1) * (1.0 + sc_ref[0]) + sh_ref[0]
    h_ref[...] = h.astype(h_ref.dtype)
    if not route:
        return

    @pl.when(pl.program_id(0) == 0)
    def _zero_carry():
        carry_ref[...] = jnp.zeros(carry_ref.shape, F32)

    tm = h.shape[0]
    lane = _lane_iota((tm, LANES))
    h_hi = h.astype(BF16)
    h_lo = (h - h_hi.astype(F32)).astype(BF16)
    logits = (jnp.dot(h_hi, wrh_ref[...], preferred_element_type=F32)
              + (jnp.dot(h_lo, wrh_ref[...], preferred_element_type=F32)
                 + jnp.dot(h_hi, wrl_ref[...], preferred_element_type=F32)))
    logits = jnp.where(lane < N_EXPERTS, logits, -jnp.inf)
    lane_f = lane.astype(F32)
    m0 = jnp.max(logits, axis=1, keepdims=True)
    e0 = jnp.min(jnp.where(logits == m0, lane_f, float(LANES)), axis=1, keepdims=True)
    rest = jnp.where(lane_f == e0, -jnp.inf, logits)
    m1 = jnp.max(rest, axis=1, keepdims=True)
    e1 = jnp.min(jnp.where(rest == m1, lane_f, float(LANES)), axis=1, keepdims=True)
    ex = jnp.exp(m1 - m0)
    w0 = 1.0 / (1.0 + ex)
    w1 = ex / (1.0 + ex)
    oh0 = (lane_f == e0).astype(F32)
    oh1 = (lane_f == e1).astype(F32)
    both = oh0 + oh1
    strict = (_lane_iota((tm, tm)) < _row_iota((tm, tm))).astype(BF16)
    before = jnp.dot(strict, both.astype(BF16), preferred_element_type=F32) + carry_ref[0:1, :]
    r0 = jnp.sum(before * oh0, axis=1, keepdims=True)
    r1 = jnp.sum(before * oh1, axis=1, keepdims=True)
    carry_ref[0:1, :] = carry_ref[0:1, :] + jnp.sum(both, axis=0, keepdims=True)
    slab = jnp.zeros((tm, LANES), F32)
    for pos, val in ((ROUTE_E0, e0), (ROUTE_E1, e1), (ROUTE_R0, r0), (ROUTE_R1, r1),
                     (ROUTE_W0, w0), (ROUTE_W1, w1)):
        slab = jnp.where(lane == pos, val, slab)
    route_ref[...] = slab
    cnt_ref[...] = carry_ref[...]


def _post_mixer(x2d, mla_o, ssd_y, gdn_o, w_out, gate, shift, scale, seq, w_router=None):
    t, d = x2d.shape
    route = w_router is not None
    tm = min(ROUTE_TILE if route else TM_PROJ, seq)
    per_b = seq // tm
    wa = w_out[:MLA_OUT].astype(BF16)
    ws = w_out[MLA_OUT:MLA_OUT + SSD_D_INNER].astype(BF16)
    wg = w_out[MLA_OUT + SSD_D_INNER:].astype(BF16)
    row = lambda w: pl.BlockSpec((tm, w), lambda i: (i, 0))
    full = lambda a: pl.BlockSpec(a.shape, lambda i: (0,) * a.ndim)
    mod = pl.BlockSpec((1, 1, d), lambda i: (i // per_b, 0, 0))
    att = pl.BlockSpec((1, MLA_OUT, tm), lambda i: (i // per_b, 0, i % per_b))
    in_specs = [row(d), att, row(SSD_D_INNER), row(GDN_OUT), full(wa), full(ws), full(wg),
                mod, mod, mod]
    args = [x2d, mla_o, ssd_y, gdn_o, wa, ws, wg, gate, shift, scale]
    out_specs = [row(d), row(d)]
    out_shape = [jax.ShapeDtypeStruct((t, d), F32), jax.ShapeDtypeStruct((t, d), F32 if route else BF16)]
    scratch = []
    if route:
        wr = jnp.zeros((d, LANES), F32).at[:, :N_EXPERTS].set(w_router.astype(F32))
        wr_hi = wr.astype(BF16)
        wr_lo = (wr - wr_hi.astype(F32)).astype(BF16)
        in_specs += [full(wr_hi), full(wr_lo)]
        args += [wr_hi, wr_lo]
        out_specs += [row(LANES), pl.BlockSpec((SUBLANES, LANES), lambda i: (0, 0))]
        out_shape += [jax.ShapeDtypeStruct((t, LANES), F32), jax.ShapeDtypeStruct((SUBLANES, LANES), F32)]
        scratch = [pltpu.VMEM((SUBLANES, LANES), F32)]
    return pl.pallas_call(
        functools.partial(_post_kernel, route=route),
        grid=(t // tm,),
        in_specs=in_specs, out_specs=out_specs, out_shape=out_shape, scratch_shapes=scratch,
        compiler_params=_cparams(("arbitrary",) if route else ("parallel",)),
        name="post_mixer_route" if route else "post_mixer",
    )(*args)


def _finish(x, final, fn_ref):
    return _rms(x) * fn_ref[...] if final else x


def _ffn_kernel(h_ref, x1_ref, gate_ref, fn_ref, wg_ref, wu_ref, wd_ref, o_ref, *, final):
    f = pl.program_id(1)
    h = h_ref[...]
    act = _silu(jnp.dot(h, wg_ref[...], preferred_element_type=F32)) * jnp.dot(
        h, wu_ref[...], preferred_element_type=F32)
    part = jnp.dot(act.astype(BF16), wd_ref[...], preferred_element_type=F32)

    @pl.when(f == 0)
    def _set():
        o_ref[...] = part

    @pl.when(f > 0)
    def _add():
        o_ref[...] += part

    @pl.when(f == pl.num_programs(1) - 1)
    def _residual():
        o_ref[...] = _finish(x1_ref[...] + gate_ref[0] * o_ref[...], final, fn_ref)


def _ffn_tile(dff):
    for cand in (2816, 1408, 1024, 512, 256, 128):
        if dff % cand == 0:
            return cand
    raise ValueError(f"unsupported d_ff {dff}")


def _dense_ffn(h, x1, gate, w_gate, w_up, w_down, final_norm, final, seq):
    t, d = x1.shape
    dff = w_gate.shape[1]
    tm = min(TM_FFN, seq)
    tf = _ffn_tile(dff)
    per_b = seq // tm
    fn = final_norm.reshape(1, d).astype(F32)
    return pl.pallas_call(
        functools.partial(_ffn_kernel, final=final),
        grid=(t // tm, dff // tf),
        in_specs=[
            pl.BlockSpec((tm, d), lambda i, f: (i, 0)),
            pl.BlockSpec((tm, d), lambda i, f: (i, 0)),
            pl.BlockSpec((1, 1, d), lambda i, f: (i // per_b, 0, 0)),
            pl.BlockSpec((1, d), lambda i, f: (0, 0)),
            pl.BlockSpec((d, tf), lambda i, f: (0, f)),
            pl.BlockSpec((d, tf), lambda i, f: (0, f)),
            pl.BlockSpec((tf, d), lambda i, f: (f, 0)),
        ],
        out_specs=pl.BlockSpec((tm, d), lambda i, f: (i, 0)),
        out_shape=jax.ShapeDtypeStruct((t, d), F32),
        compiler_params=_cparams(("parallel", "arbitrary")),
        name="dense_ffn",
    )(h, x1, gate, fn, w_gate.astype(BF16), w_up.astype(BF16), w_down.astype(BF16))


def _row_copy(src, dst, sem):
    return pltpu.make_async_copy(src, dst, sem)


def _dispatch_kernel(dest_ref, h_ref, zeros_hbm, xs_hbm, sem, *, tm):
    del zeros_hbm

    def issue(g, carry):
        for j in range(ROW_DMA_UNROLL):
            t = g * ROW_DMA_UNROLL + j
            for kk in range(TOP_K):
                d = dest_ref[0, 0, TOP_K * t + kk]
                _row_copy(h_ref.at[pl.ds(t, 1), :], xs_hbm.at[pl.ds(d, 1), :], sem).start(priority=kk)
        return carry

    lax.fori_loop(0, tm // ROW_DMA_UNROLL, issue, 0)

    def drain(g, carry):
        for _ in range(ROW_DMA_UNROLL * TOP_K):
            _row_copy(h_ref.at[pl.ds(0, 1), :], xs_hbm.at[pl.ds(0, 1), :], sem).wait()
        return carry

    lax.fori_loop(0, tm // ROW_DMA_UNROLL, drain, 0)


def _moe_dispatch(h_f32, dest, n_rows):
    t, d = h_f32.shape
    tm = min(MOE_TOK_TILE, t)
    dest3 = dest.reshape(t // tm, 1, TOP_K * tm)
    return pl.pallas_call(
        functools.partial(_dispatch_kernel, tm=tm),
        grid=(t // tm,),
        in_specs=[
            pl.BlockSpec((1, 1, TOP_K * tm), lambda i: (i, 0, 0), memory_space=pltpu.SMEM),
            pl.BlockSpec((tm, d), lambda i: (i, 0)),
            pl.BlockSpec(memory_space=pl.ANY),
        ],
        out_specs=pl.BlockSpec(memory_space=pl.ANY),
        out_shape=jax.ShapeDtypeStruct((n_rows, d), F32),
        scratch_shapes=[pltpu.SemaphoreType.DMA(())],
        input_output_aliases={2: 0},
        compiler_params=_cparams(("arbitrary",)),
        name="moe_dispatch",
    )(dest3, h_f32, jnp.zeros((n_rows, d), F32))


def _experts_kernel(be_ref, nused_ref, x_ref, wg_ref, wu_ref, wd_ref, y_ref):
    i = pl.program_id(0)
    f = pl.program_id(1)

    @pl.when(f == 0)
    def _zero():
        y_ref[...] = jnp.zeros(y_ref.shape, F32)

    @pl.when(i < nused_ref[0])
    def _compute():
        xb = x_ref[...].astype(BF16)
        act = _silu(jnp.dot(xb, wg_ref[0], preferred_element_type=F32)) * jnp.dot(
            xb, wu_ref[0], preferred_element_type=F32)
        y_ref[...] += jnp.dot(act.astype(BF16), wd_ref[0], preferred_element_type=F32)


def _moe_experts(xs, block_expert, n_used, w_gate, w_up, w_down):
    n_rows, d = xs.shape
    dff = w_gate.shape[2]
    bm = MOE_ROWS
    tf = MOE_FF_TILE if dff % MOE_FF_TILE == 0 else dff

    def ff(i, f, nu):
        return jnp.where(i < nu[0], f, 0)

    return pl.pallas_call(
        _experts_kernel,
        grid_spec=pltpu.PrefetchScalarGridSpec(
            num_scalar_prefetch=2,
            grid=(n_rows // bm, dff // tf),
            in_specs=[
                pl.BlockSpec((bm, d), lambda i, f, be, nu: (i, 0)),
                pl.BlockSpec((1, d, tf), lambda i, f, be, nu: (be[i], 0, ff(i, f, nu))),
                pl.BlockSpec((1, d, tf), lambda i, f, be, nu: (be[i], 0, ff(i, f, nu))),
                pl.BlockSpec((1, tf, d), lambda i, f, be, nu: (be[i], ff(i, f, nu), 0)),
            ],
            out_specs=pl.BlockSpec((bm, d), lambda i, f, be, nu: (i, 0)),
        ),
        out_shape=jax.ShapeDtypeStruct((n_rows, d), F32),
        compiler_params=_cparams(("parallel", "arbitrary")),
        name="moe_experts",
    )(block_expert, n_used, xs, w_gate.astype(BF16), w_up.astype(BF16), w_down.astype(BF16))


def _combine_kernel(dest_ref, dnext_ref, x1_ref, route_ref, gate_ref, fn_ref, ys_hbm, o_ref, buf_ref, sem,
                    *, tm, final):
    i = pl.program_id(0)
    slot = lax.rem(i, 2)

    def gather(d_ref, s):
        def issue(g, carry):
            for j in range(ROW_DMA_UNROLL):
                t = g * ROW_DMA_UNROLL + j
                for kk in range(TOP_K):
                    d = d_ref[0, 0, TOP_K * t + kk]
                    _row_copy(ys_hbm.at[pl.ds(d, 1), :], buf_ref.at[s, kk, pl.ds(t, 1), :],
                              sem.at[s]).start(priority=kk)
            return carry

        lax.fori_loop(0, tm // ROW_DMA_UNROLL, issue, 0)

    @pl.when(i == 0)
    def _first_tile():
        gather(dest_ref, slot)

    @pl.when(i + 1 < pl.num_programs(0))
    def _next_tile():
        gather(dnext_ref, 1 - slot)

    def drain(g, carry):
        for _ in range(ROW_DMA_UNROLL * TOP_K):
            _row_copy(ys_hbm.at[pl.ds(0, 1), :], buf_ref.at[slot, 0, pl.ds(0, 1), :], sem.at[slot]).wait()
        return carry

    lax.fori_loop(0, tm // ROW_DMA_UNROLL, drain, 0)
    route = route_ref[...]
    w0 = route[:, ROUTE_W0:ROUTE_W0 + 1]
    w1 = route[:, ROUTE_W1:ROUTE_W1 + 1]
    f = w0 * buf_ref[slot, 0] + w1 * buf_ref[slot, 1]
    o_ref[...] = _finish(x1_ref[...] + gate_ref[0] * f, final, fn_ref)


def _moe_combine(ys, dest, x1, route, gate, final_norm, final, seq):
    t, d = x1.shape
    tm = min(MOE_TOK_TILE, seq)
    per_b = seq // tm
    dest3 = dest.reshape(t // tm, 1, TOP_K * tm)
    fn = final_norm.reshape(1, d).astype(F32)
    nt = t // tm
    return pl.pallas_call(
        functools.partial(_combine_kernel, tm=tm, final=final),
        grid=(nt,),
        in_specs=[
            pl.BlockSpec((1, 1, TOP_K * tm), lambda i: (i, 0, 0), memory_space=pltpu.SMEM),
            pl.BlockSpec((1, 1, TOP_K * tm), lambda i: (jnp.minimum(i + 1, nt - 1), 0, 0),
                         memory_space=pltpu.SMEM),
            pl.BlockSpec((tm, d), lambda i: (i, 0)),
            pl.BlockSpec((tm, LANES), lambda i: (i, 0)),
            pl.BlockSpec((1, 1, d), lambda i: (i // per_b, 0, 0)),
            pl.BlockSpec((1, d), lambda i: (0, 0)),
            pl.BlockSpec(memory_space=pl.ANY),
        ],
        out_specs=pl.BlockSpec((tm, d), lambda i: (i, 0)),
        out_shape=jax.ShapeDtypeStruct((t, d), F32),
        scratch_shapes=[pltpu.VMEM((2, TOP_K, tm, d), F32), pltpu.SemaphoreType.DMA((2,))],
        compiler_params=_cparams(("arbitrary",)),
        name="moe_combine",
    )(dest3, dest3, x1, route, gate, fn, ys)


def _moe_ffn(h_f32, x1, route, counts_slab, gate, w_gate, w_up, w_down, final_norm, final, seq):
    t, d = x1.shape
    bm = MOE_ROWS
    n_blocks = (t * TOP_K + bm - 1) // bm + N_EXPERTS
    n_rows = n_blocks * bm
    counts = counts_slab[0, :N_EXPERTS].astype(I32)
    padded = ((counts + bm - 1) // bm) * bm
    pend = jnp.cumsum(padded)
    pstart = pend - padded
    e = route[:, ROUTE_E0:ROUTE_E1 + 1].astype(I32)
    rank = route[:, ROUTE_R0:ROUTE_R1 + 1].astype(I32)
    dest = (jnp.sum(jnp.where(e[..., None] == jnp.arange(N_EXPERTS), pstart, 0), axis=-1) + rank)
    blk_start = jnp.arange(n_blocks, dtype=I32) * bm
    block_expert = jnp.minimum(jnp.sum(blk_start[:, None] >= pend[None, :], axis=1), N_EXPERTS - 1).astype(I32)
    n_used = (pend[-1:] // bm).astype(I32)
    xs = _moe_dispatch(h_f32, dest, n_rows)
    ys = _moe_experts(xs, block_expert, n_used, w_gate, w_up, w_down)
    return _moe_combine(ys, dest, x1, route, gate, final_norm, final, seq)


def kernel(x, c, ada_w, ada_b, w_in, mla_q_norm, mla_w_uq, mla_kv_norm, mla_w_ukv, ssd_conv_w, ssd_conv_b, ssd_dt_bias, ssd_a_log, ssd_d, ssd_norm, gdn_conv_w, gdn_conv_b, gdn_dt_bias, gdn_a_log, gdn_norm, w_out, ffn_w_gate, ffn_w_up, ffn_w_down, moe_router, moe_w_gate, moe_w_up, moe_w_down, final_norm):
    bsz, seq, d = x.shape
    depth = w_in.shape[0]
    t = bsz * seq
    mod = _ada_modulation(c, ada_w, ada_b)
    tables = _rope_tables(seq)
    xf = x.reshape(t, d)
    for l in range(depth):
        sh1, sc1, g1, sh2, sc2, g2 = [m.reshape(bsz, 1, d) for m in jnp.split(mod[l], 6, axis=-1)]
        mla_c, misc, ssd_z, ssd_xbc, gdn_qkv, gdn_z = _in_proj(xf, sh1, sc1, _prep_w_in(w_in[l]), seq)
        wqa, wqb, wkv = _prep_mla_weights(mla_w_uq[l], mla_w_ukv[l])
        qt, k, vt = _mla_proj(mla_c, misc, tables, mla_q_norm[l], mla_kv_norm[l], wqa, wqb, wkv, bsz, seq)
        mla_o = _flash_attention(qt, k, vt, bsz, seq)
        ssd_y = _ssd_mixer(ssd_z, ssd_xbc, misc, ssd_conv_w[l], ssd_conv_b[l], ssd_dt_bias[l],
                           ssd_a_log[l], ssd_d[l], ssd_norm[l], bsz, seq)
        gdn_o = _gdn_mixer(gdn_qkv, gdn_z, misc, gdn_conv_w[l], gdn_conv_b[l], gdn_dt_bias[l],
                           gdn_a_log[l], gdn_norm[l], bsz, seq)
        final = l == depth - 1
        if l % 2 == 0:
            x1, h2 = _post_mixer(xf, mla_o, ssd_y, gdn_o, w_out[l], g1, sh2, sc2, seq)
            xf = _dense_ffn(h2, x1, g2, ffn_w_gate[l // 2], ffn_w_up[l // 2], ffn_w_down[l // 2],
                            final_norm, final, seq)
        else:
            x1, h2, route, counts = _post_mixer(xf, mla_o, ssd_y, gdn_o, w_out[l], g1, sh2, sc2, seq,
                                                w_router=moe_router[l // 2])
            xf = _moe_ffn(h2, x1, route, counts, g2, moe_w_gate[l // 2], moe_w_up[l // 2],
                          moe_w_down[l // 2], final_norm, final, seq)
    return xf.reshape(bsz, seq, d)
```

```python
import functools
import math

import jax
import jax.numpy as jnp
import numpy as np
from jax import lax
from jax.experimental import pallas as pl
from jax.experimental.pallas import tpu as pltpu

F32 = jnp.float32
BF16 = jnp.bfloat16
I32 = jnp.int32
HIGHEST = lax.Precision.HIGHEST

D_MODEL = 1024
EPS = 1e-6
CONV_WIDTH = 4
MLA_HEADS, MLA_Q_RANK, MLA_KV_RANK = 4, 256, 128
MLA_NOPE, MLA_ROPE, MLA_V = 64, 32, 64
ROPE_THETA = 10000.0
SSD_HEADS, SSD_HEAD_DIM, SSD_GROUPS, SSD_STATE, SSD_CHUNK = 8, 64, 2, 64, 128
SSD_D_INNER = SSD_HEADS * SSD_HEAD_DIM
SSD_CONV_DIM = SSD_D_INNER + 2 * SSD_GROUPS * SSD_STATE
GDN_HEADS, GDN_DK, GDN_DV, GDN_CHUNK = 4, 64, 64, 64
GDN_QKV = GDN_HEADS * (2 * GDN_DK + GDN_DV)
GDN_OUT = GDN_HEADS * GDN_DV
MLA_IN = MLA_Q_RANK + MLA_KV_RANK + MLA_ROPE
MLA_OUT = MLA_HEADS * MLA_V
SSD_IN = SSD_D_INNER + SSD_CONV_DIM + SSD_HEADS
GDN_IN = GDN_QKV + GDN_HEADS * GDN_DV + 2 * GDN_HEADS
N_EXPERTS, TOP_K = 8, 2

LANES = 128
SUBLANES = 8
HALF = LANES // 2
VMEM_LIMIT_BYTES = 56 * 1024 * 1024

MISC_DT = 0
MISC_B = 8
MISC_A = 12
MISC_KR = 64
MISC_KR_SW = 96
HEAD_PAD = 128

TM_PROJ = 1024
IN_PROJ_SUB = 512
TM_FFN = 512
ATT_Q_BLOCK = 1024
ATT_K_BLOCK = 2048
ATT_Q_COLS = 256
ATT_K_ROWS = 512
ATT_PIPELINE = 4
SSD_STEP_CHUNKS = 2
GDN_STEP_CHUNKS = 8
MOE_ROWS = 512
MOE_FF_TILE = 1792
MOE_TOK_TILE = 512
ROUTE_TILE = 512
ROW_DMA_UNROLL = 8


def _cparams(sem):
    return pltpu.CompilerParams(dimension_semantics=sem, vmem_limit_bytes=VMEM_LIMIT_BYTES)


def _lane_iota(shape):
    return lax.broadcasted_iota(I32, shape, len(shape) - 1)


def _row_iota(shape):
    return lax.broadcasted_iota(I32, shape, len(shape) - 2)


def _softplus(x):
    return jnp.maximum(x, 0.0) + jnp.log1p(jnp.exp(-jnp.abs(x)))


def _silu(x):
    return x * jax.nn.sigmoid(x)


def _rms(x):
    return x * lax.rsqrt(jnp.mean(x * x, axis=-1, keepdims=True) + EPS)


def _dot(a, b):
    return jnp.dot(a.astype(BF16), b.astype(BF16), preferred_element_type=F32)


def _dot_nt(a, b):
    return lax.dot_general(a.astype(BF16), b.astype(BF16), (((1,), (1,)), ((), ())),
                           preferred_element_type=F32)


def _dot_tn(a, b):
    return lax.dot_general(a.astype(BF16), b.astype(BF16), (((0,), (0,)), ((), ())),
                           preferred_element_type=F32)


def _dot_f32(a, b):
    return jnp.dot(a, b, precision=HIGHEST, preferred_element_type=F32)


def _ada_kernel(c_ref, w_ref, b_ref, o_ref):
    c_act = _silu(c_ref[...])
    o_ref[0] = _dot_f32(c_act, w_ref[0]) + b_ref[0]


def _ada_modulation(c, ada_w, ada_b):
    depth, d, n = ada_w.shape
    bsz = c.shape[0]
    rows = max(SUBLANES, bsz)
    c_pad = jnp.zeros((rows, d), F32).at[:bsz].set(c)
    tn = 1536
    out = pl.pallas_call(
        _ada_kernel,
        grid=(depth, n // tn),
        in_specs=[
            pl.BlockSpec((rows, d), lambda l, j: (0, 0)),
            pl.BlockSpec((1, d, tn), lambda l, j: (l, 0, j)),
            pl.BlockSpec((1, 1, tn), lambda l, j: (l, 0, j)),
        ],
        out_specs=pl.BlockSpec((1, rows, tn), lambda l, j: (l, 0, j)),
        out_shape=jax.ShapeDtypeStruct((depth, rows, n), F32),
        compiler_params=_cparams(("parallel", "parallel")),
        name="ada_modulation",
    )(c_pad, ada_w, ada_b.reshape(depth, 1, n))
    return out[:, :bsz]


IN_SLABS = (("mla_c", 384, F32), ("misc", 128, F32), ("ssd_z", 512, BF16), ("ssd_xbc", 768, BF16),
            ("gdn_qkv", 768, BF16), ("gdn_z", 256, BF16))
IN_WIDTH = sum(w for _, w, _ in IN_SLABS)


def _prep_w_in(w):
    d = w.shape[0]
    o_ssd = MLA_IN
    o_gdn = MLA_IN + SSD_IN
    w_kr = w[:, MLA_Q_RANK + MLA_KV_RANK:MLA_IN]
    half = MLA_ROPE // 2
    w_kr_sw = jnp.concatenate([-w_kr[:, half:], w_kr[:, :half]], axis=1)
    ssd_dt = w[:, o_ssd + SSD_D_INNER + SSD_CONV_DIM:o_ssd + SSD_IN]
    gdn_ba = w[:, o_gdn + GDN_QKV + GDN_OUT:o_gdn + GDN_IN]
    misc = jnp.concatenate(
        [ssd_dt, gdn_ba, jnp.zeros((d, MISC_KR - MISC_A - GDN_HEADS), w.dtype), w_kr, w_kr_sw], axis=1)
    cols = [
        w[:, :MLA_Q_RANK + MLA_KV_RANK], misc,
        w[:, o_ssd:o_ssd + SSD_D_INNER],
        w[:, o_ssd + SSD_D_INNER:o_ssd + SSD_D_INNER + SSD_CONV_DIM],
        w[:, o_gdn:o_gdn + GDN_QKV],
        w[:, o_gdn + GDN_QKV:o_gdn + GDN_QKV + GDN_OUT],
    ]
    return jnp.concatenate(cols, axis=1).astype(BF16)


def _in_proj_kernel(x_ref, sh_ref, sc_ref, w_ref, *o_refs):
    rows = x_ref.shape[0]
    sub = min(IN_PROJ_SUB, rows)

    def normed(r0):
        return (_rms(x_ref[r0:r0 + sub, :]) * (1.0 + sc_ref[0]) + sh_ref[0]).astype(BF16)

    nxt = normed(0)
    for r0 in range(0, rows, sub):
        hb = nxt
        if r0 + sub < rows:
            nxt = normed(r0 + sub)
        off = 0
        for o_ref, (_, width, _) in zip(o_refs, IN_SLABS):
            o_ref[r0:r0 + sub, :] = jnp.dot(
                hb, w_ref[:, off:off + width], preferred_element_type=F32).astype(o_ref.dtype)
            off += width


def _in_proj(x2d, shift, scale, w_perm, seq):
    t, d = x2d.shape
    tm = min(TM_PROJ, seq)
    per_b = seq // tm
    return pl.pallas_call(
        _in_proj_kernel,
        grid=(t // tm,),
        in_specs=[
            pl.BlockSpec((tm, d), lambda i: (i, 0)),
            pl.BlockSpec((1, 1, d), lambda i: (i // per_b, 0, 0)),
            pl.BlockSpec((1, 1, d), lambda i: (i // per_b, 0, 0)),
            pl.BlockSpec((d, IN_WIDTH), lambda i: (0, 0)),
        ],
        out_specs=[pl.BlockSpec((tm, w), lambda i: (i, 0)) for _, w, _ in IN_SLABS],
        out_shape=[jax.ShapeDtypeStruct((t, w), dt) for _, w, dt in IN_SLABS],
        compiler_params=_cparams(("parallel",)),
        name="in_proj",
    )(x2d, shift, scale, w_perm)


def _rope_tables(seq):
    pos = jnp.arange(seq, dtype=F32)
    inv_freq = ROPE_THETA ** (-jnp.arange(0, MLA_ROPE, 2, dtype=F32) / MLA_ROPE)
    ang = pos[:, None] * inv_freq[None, :]
    cos, sin = jnp.cos(ang), jnp.sin(ang)
    zeros_l = jnp.zeros((seq, MLA_NOPE), F32)
    zeros_r = jnp.zeros((seq, HEAD_PAD - MLA_NOPE - MLA_ROPE), F32)
    cos_k = jnp.concatenate([zeros_l, cos, cos, zeros_r], axis=1)
    sin_k = jnp.concatenate([zeros_l, sin, sin, zeros_r], axis=1)
    scale = (MLA_NOPE + MLA_ROPE) ** -0.5 * math.log2(math.e)
    cos_q = scale * jnp.concatenate([jnp.ones((seq, MLA_NOPE), F32), cos, cos, zeros_r], axis=1)
    sin_q = scale * sin_k
    return cos_q, sin_q, cos_k, sin_k


def _prep_mla_weights(w_uq, w_ukv):
    r = w_uq.shape[0]
    hq = MLA_NOPE + MLA_ROPE
    half = MLA_ROPE // 2
    qa, qb = [], []
    for h in range(MLA_HEADS):
        nope = w_uq[:, h * hq:h * hq + MLA_NOPE]
        rope = w_uq[:, h * hq + MLA_NOPE:(h + 1) * hq]
        rope_sw = jnp.concatenate([-rope[:, half:], rope[:, :half]], axis=1)
        pad = jnp.zeros((r, HEAD_PAD - hq), w_uq.dtype)
        qa += [nope, rope, pad]
        qb += [jnp.zeros((r, MLA_NOPE), w_uq.dtype), rope_sw, pad]
    rk = w_ukv.shape[0]
    hk = MLA_NOPE + MLA_V
    kcols, vcols = [], []
    for h in range(MLA_HEADS):
        kcols += [w_ukv[:, h * hk:h * hk + MLA_NOPE], jnp.zeros((rk, HEAD_PAD - MLA_NOPE), w_ukv.dtype)]
        vcols += [w_ukv[:, h * hk + MLA_NOPE:(h + 1) * hk], jnp.zeros((rk, HEAD_PAD - MLA_V), w_ukv.dtype)]
    return (jnp.concatenate(qa, axis=1).astype(BF16), jnp.concatenate(qb, axis=1).astype(BF16),
            jnp.concatenate(kcols + vcols, axis=1).astype(BF16))


def _mla_proj_kernel(c_ref, misc_ref, cq_ref, sq_ref, ck_ref, sk_ref, qn_ref, kvn_ref,
                     wqa_ref, wqb_ref, wkv_ref, qt_ref, k_ref, vt_ref):
    c = c_ref[...]
    cq = (_rms(c[:, :MLA_Q_RANK]) * qn_ref[...]).astype(BF16)
    ckv = (_rms(c[:, MLA_Q_RANK:]) * kvn_ref[...]).astype(BF16)
    qa = jnp.dot(cq, wqa_ref[...], preferred_element_type=F32)
    qb = jnp.dot(cq, wqb_ref[...], preferred_element_type=F32)
    cos_q = jnp.concatenate([cq_ref[...]] * MLA_HEADS, axis=1)
    sin_q = jnp.concatenate([sq_ref[...]] * MLA_HEADS, axis=1)
    qt_ref[0] = (qa * cos_q + qb * sin_q).T.astype(BF16)
    kv = jnp.dot(ckv, wkv_ref[...], preferred_element_type=F32)
    misc = misc_ref[...]
    kr = misc * ck_ref[...] + pltpu.roll(misc, LANES - (MISC_KR_SW - MISC_KR), 1) * sk_ref[...]
    kw = MLA_HEADS * HEAD_PAD
    k_ref[0] = (kv[:, :kw] + jnp.concatenate([kr] * MLA_HEADS, axis=1)).astype(BF16)
    v = kv[:, kw:]
    v = jnp.where(_lane_iota(v.shape) % HEAD_PAD == MLA_V, 1.0, v)
    vt_ref[0] = v.T.astype(BF16)


def _mla_proj(mla_c, misc, tables, q_norm, kv_norm, wqa, wqb, wkv, bsz, seq):
    t = mla_c.shape[0]
    tm = min(TM_PROJ, seq)
    per_b = seq // tm
    kw = MLA_HEADS * HEAD_PAD
    tab_spec = pl.BlockSpec((tm, HEAD_PAD), lambda i: (i % per_b, 0))
    full = lambda a: pl.BlockSpec(a.shape, lambda i: (0,) * a.ndim)
    qn = q_norm.reshape(1, -1)
    kvn = kv_norm.reshape(1, -1)
    t_spec = pl.BlockSpec((1, kw, tm), lambda i: (i // per_b, 0, i % per_b))
    return pl.pallas_call(
        _mla_proj_kernel,
        grid=(t // tm,),
        in_specs=[
            pl.BlockSpec((tm, mla_c.shape[1]), lambda i: (i, 0)),
            pl.BlockSpec((tm, LANES), lambda i: (i, 0)),
            tab_spec, tab_spec, tab_spec, tab_spec,
            full(qn), full(kvn), full(wqa), full(wqb), full(wkv),
        ],
        out_specs=[t_spec, pl.BlockSpec((1, tm, kw), lambda i: (i // per_b, i % per_b, 0)), t_spec],
        out_shape=[jax.ShapeDtypeStruct((bsz, kw, seq), BF16), jax.ShapeDtypeStruct((bsz, seq, kw), BF16),
                   jax.ShapeDtypeStruct((bsz, kw, seq), BF16)],
        compiler_params=_cparams(("parallel",)),
        name="mla_proj",
    )(mla_c, misc, *tables, qn, kvn, wqa, wqb, wkv)


def _flash_diag_offsets(qb, kb):
    return sorted({((qi * qb) // kb) * kb - qi * qb for qi in range(max(kb // qb, 1))})


def _flash_kernel(qi_tab, ki_tab, kind_tab, qt_ref, k_ref, vt_ref, o_ref, m_ref, acc_ref, *, qb, kb, rq, ck):
    del qi_tab
    p = pl.program_id(1)
    ki = ki_tab[p]
    kind = kind_tab[p]

    @pl.when(ki == 0)
    def _init():
        m_ref[...] = jnp.full(m_ref.shape, -jnp.inf, F32)
        acc_ref[...] = jnp.zeros(acc_ref.shape, F32)

    def step(rel):
        units = []
        for rc in range(qb // rq):
            for kc in range(kb // ck):
                if rel is not None and rel + kc * ck > rc * rq + rq - 1:
                    continue
                for h in range(MLA_HEADS):
                    units.append((h, rc * rq, kc * ck))

        def scores(unit):
            h, c0, k0 = unit
            kk = k_ref[0, k0:k0 + ck, h * HEAD_PAD:(h + 1) * HEAD_PAD]
            qt = qt_ref[0, h * HEAD_PAD:(h + 1) * HEAD_PAD, c0:c0 + rq]
            s = jnp.dot(kk, qt, preferred_element_type=F32)
            if rel is not None and rel + k0 + ck - 1 > c0:
                key = lax.broadcasted_iota(I32, s.shape, 0) + (k0 + rel)
                qry = lax.broadcasted_iota(I32, s.shape, 1) + c0
                s = jnp.where(key <= qry, s, -jnp.inf)
            return s

        def update(unit, s):
            h, c0, k0 = unit
            m_prev = m_ref[h, :, c0:c0 + rq]
            m_new = jnp.maximum(m_prev, jnp.max(s, axis=0, keepdims=True))
            alpha = jnp.exp2(m_prev - m_new)
            pt = jnp.exp2(s - m_new).astype(BF16)
            m_ref[h, :, c0:c0 + rq] = m_new
            acc_ref[h, :, c0:c0 + rq] = acc_ref[h, :, c0:c0 + rq] * alpha + jnp.dot(
                vt_ref[0, h * HEAD_PAD:(h + 1) * HEAD_PAD, k0:k0 + ck], pt, preferred_element_type=F32)

        pending = [scores(u) for u in units[:ATT_PIPELINE]]
        for n, unit in enumerate(units):
            s = pending.pop(0)
            if n + ATT_PIPELINE < len(units):
                pending.append(scores(units[n + ATT_PIPELINE]))
            update(unit, s)

    @pl.when(kind == 0)
    def _off_diagonal():
        step(None)

    for n, rel in enumerate(_flash_diag_offsets(qb, kb)):
        @pl.when(kind == n + 1)
        def _diagonal(rel=rel):
            step(rel)
            for h in range(MLA_HEADS):
                acc = acc_ref[h]
                o_ref[0, h * MLA_V:(h + 1) * MLA_V, :] = (
                    acc[:MLA_V] / acc[MLA_V:MLA_V + 1]).astype(o_ref.dtype)


def _flash_attention(qt, k, vt, bsz, seq):
    qb = min(ATT_Q_BLOCK, seq)
    kb = min(ATT_K_BLOCK, seq)
    rq = min(ATT_Q_COLS, qb)
    ck = min(ATT_K_ROWS, kb)
    offsets = _flash_diag_offsets(qb, kb)
    qi_l, ki_l, kind_l = [], [], []
    for qi in range(seq // qb):
        for kj in range((qi * qb + qb - 1) // kb + 1):
            qi_l.append(qi)
            ki_l.append(kj)
            visible = (kj + 1) * kb - 1 <= qi * qb
            kind_l.append(0 if visible else 1 + offsets.index(kj * kb - qi * qb))
    tabs = [jnp.asarray(x, I32) for x in (qi_l, ki_l, kind_l)]
    kw = MLA_HEADS * HEAD_PAD
    return pl.pallas_call(
        functools.partial(_flash_kernel, qb=qb, kb=kb, rq=rq, ck=ck),
        grid_spec=pltpu.PrefetchScalarGridSpec(
            num_scalar_prefetch=3,
            grid=(bsz, len(qi_l)),
            in_specs=[
                pl.BlockSpec((1, kw, qb), lambda b, p, qt_, kt_, kd_: (b, 0, qt_[p])),
                pl.BlockSpec((1, kb, kw), lambda b, p, qt_, kt_, kd_: (b, kt_[p], 0)),
                pl.BlockSpec((1, kw, kb), lambda b, p, qt_, kt_, kd_: (b, 0, kt_[p])),
            ],
            out_specs=pl.BlockSpec((1, MLA_OUT, qb), lambda b, p, qt_, kt_, kd_: (b, 0, qt_[p])),
            scratch_shapes=[pltpu.VMEM((MLA_HEADS, 1, qb), F32),
                            pltpu.VMEM((MLA_HEADS, HEAD_PAD, qb), F32)],
        ),
        out_shape=jax.ShapeDtypeStruct((bsz, MLA_OUT, seq), BF16),
        compiler_params=_cparams(("parallel", "arbitrary")),
        name="mla_flash",
    )(*tabs, qt, k, vt)


def _causal_conv_silu(cur, ext_ref, halo_ref, cw_ref, cb_ref, first):
    rows = cur.shape[0]

    @pl.when(first)
    def _zero_halo():
        halo_ref[...] = jnp.zeros(halo_ref.shape, F32)

    ext_ref[0:SUBLANES, :] = halo_ref[...]
    ext_ref[SUBLANES:SUBLANES + rows, :] = cur
    halo_ref[...] = cur[rows - SUBLANES:, :]
    acc = cb_ref[...] + cw_ref[CONV_WIDTH - 1:CONV_WIDTH, :] * cur
    for j in range(CONV_WIDTH - 1):
        start = SUBLANES - (CONV_WIDTH - 1) + j
        acc = acc + cw_ref[j:j + 1, :] * ext_ref[start:start + rows, :]
    return _silu(acc)


def _ssd_kernel(z_ref, xbc_ref, misc_ref, cw_ref, cb_ref, dtb_ref, aneg_ref, dsk_ref, nw_ref,
                y_ref, ext_ref, halo_ref, st_ref, *, nchunk, bsz):
    first = pl.program_id(0) == 0
    length = SSD_CHUNK

    @pl.when(first)
    def _zero_state():
        st_ref[...] = jnp.zeros(st_ref.shape, F32)

    tri = (_lane_iota((length, length)) <= _row_iota((length, length))).astype(F32)
    lower = _lane_iota((length, length)) <= _row_iota((length, length))
    lane = _lane_iota((length, LANES))
    lo = lane < HALF
    bw = SSD_GROUPS * SSD_STATE
    npair = SSD_HEADS // 2
    for b in range(bsz):
        _ssd_rows(b, z_ref, xbc_ref, misc_ref, cw_ref, cb_ref, dtb_ref, aneg_ref, dsk_ref, nw_ref, y_ref,
                  ext_ref.at[b], halo_ref.at[b], st_ref, b * npair, first, nchunk, tri, lower, lo, bw)


def _ssd_rows(b, z_ref, xbc_ref, misc_ref, cw_ref, cb_ref, dtb_ref, aneg_ref, dsk_ref, nw_ref, y_ref,
              ext_ref, halo_ref, st_ref, st0, first, nchunk, tri, lower, lo, bw):
    length = SSD_CHUNK
    xbc = _causal_conv_silu(xbc_ref[b].astype(F32), ext_ref, halo_ref, cw_ref, cb_ref, first)
    dt_all = _softplus(misc_ref[b] + dtb_ref[...])
    a_all = dt_all * aneg_ref[...]

    for c in range(nchunk):
        r0 = c * length
        dt = dt_all[r0:r0 + length]
        acum = _dot_f32(tri, a_all[r0:r0 + length])
        acum_t = acum.T
        x_c = xbc[r0:r0 + length, :SSD_D_INNER]
        b_c = xbc[r0:r0 + length, SSD_D_INNER:SSD_D_INNER + bw]
        c_c = xbc[r0:r0 + length, SSD_D_INNER + bw:]
        y_parts = []
        for g in range(SSD_GROUPS):
            b_g = b_c[:, g * SSD_STATE:(g + 1) * SSD_STATE]
            c_g = c_c[:, g * SSD_STATE:(g + 1) * SSD_STATE].astype(BF16)
            b_gt = b_g.T.astype(BF16)
            cb = jnp.dot(c_g, b_gt, preferred_element_type=F32)
            pairs_per_group = SSD_HEADS // SSD_GROUPS // 2
            for pp in range(pairs_per_group):
                pr = g * pairs_per_group + pp
                h0, h1 = 2 * pr, 2 * pr + 1
                col0, col1 = acum[:, h0:h0 + 1], acum[:, h1:h1 + 1]
                l0 = jnp.exp(jnp.where(lower, col0 - acum_t[h0:h0 + 1, :], -jnp.inf))
                l1 = jnp.exp(jnp.where(lower, col1 - acum_t[h1:h1 + 1, :], -jnp.inf))
                mmat = jnp.concatenate([cb * l0, cb * l1], axis=1).astype(BF16)
                xp = x_c[:, pr * LANES:(pr + 1) * LANES]
                xdt = xp * jnp.where(lo, dt[:, h0:h0 + 1], dt[:, h1:h1 + 1])
                rhs = jnp.concatenate([jnp.where(lo, xdt, 0.0), jnp.where(lo, 0.0, xdt)], axis=0)
                y_diag = jnp.dot(mmat, rhs.astype(BF16), preferred_element_type=F32)
                col_pair = jnp.where(lo, col0, col1)
                last_pair = jnp.where(lo[:1], acum[length - 1:length, h0:h0 + 1],
                                      acum[length - 1:length, h1:h1 + 1])
                st = st_ref[st0 + pr]
                y_off = jnp.dot(c_g, st.astype(BF16), preferred_element_type=F32) * jnp.exp(col_pair)
                xdec = (xdt * jnp.exp(last_pair - col_pair)).astype(BF16)
                st_ref[st0 + pr] = st * jnp.exp(last_pair) + jnp.dot(b_gt, xdec, preferred_element_type=F32)
                y_parts.append(y_diag + y_off + xp * dsk_ref[:, pr * LANES:(pr + 1) * LANES])
        y = jnp.concatenate(y_parts, axis=1) * _silu(z_ref[b, r0:r0 + length, :].astype(F32))
        gw = SSD_D_INNER // SSD_GROUPS
        y = jnp.concatenate([_rms(y[:, g * gw:(g + 1) * gw]) for g in range(SSD_GROUPS)], axis=1)
        y_ref[b, r0:r0 + length, :] = (y * nw_ref[...]).astype(y_ref.dtype)


def _ssd_mixer(z, xbc, misc, conv_w, conv_b, dt_bias, a_log, d_skip, norm_w, bsz, seq):
    nchunk = min(SSD_STEP_CHUNKS, seq // SSD_CHUNK)
    rows = nchunk * SSD_CHUNK
    pad = jnp.zeros((LANES - SSD_HEADS,), F32)
    dtb = jnp.concatenate([dt_bias.astype(F32), pad]).reshape(1, LANES)
    aneg = jnp.concatenate([-jnp.exp(a_log.astype(F32)), pad]).reshape(1, LANES)
    dsk = jnp.repeat(d_skip.astype(F32), SSD_HEAD_DIM).reshape(1, SSD_D_INNER)
    full = lambda a: pl.BlockSpec(a.shape, lambda i: (0,) * a.ndim)
    cb2 = conv_b.reshape(1, -1)
    nw2 = norm_w.reshape(1, -1)
    blk = lambda w: pl.BlockSpec((bsz, rows, w), lambda i: (0, i, 0))
    out = pl.pallas_call(
        functools.partial(_ssd_kernel, nchunk=nchunk, bsz=bsz),
        grid=(seq // rows,),
        in_specs=[blk(SSD_D_INNER), blk(SSD_CONV_DIM), blk(LANES),
                  full(conv_w), full(cb2), full(dtb), full(aneg), full(dsk), full(nw2)],
        out_specs=blk(SSD_D_INNER),
        out_shape=jax.ShapeDtypeStruct((bsz, seq, SSD_D_INNER), BF16),
        scratch_shapes=[pltpu.VMEM((bsz, rows + SUBLANES, SSD_CONV_DIM), F32),
                        pltpu.VMEM((bsz, SUBLANES, SSD_CONV_DIM), F32),
                        pltpu.VMEM((bsz * (SSD_HEADS // 2), SSD_STATE, LANES), F32)],
        compiler_params=_cparams(("arbitrary",)),
        name="ssd_mixer",
    )(z.reshape(bsz, seq, -1), xbc.reshape(bsz, seq, -1), misc.reshape(bsz, seq, -1),
      conv_w, cb2, dtb, aneg, dsk, nw2)
    return out.reshape(bsz * seq, SSD_D_INNER)


def _half_sum(x, lo):
    s_lo = jnp.sum(jnp.where(lo, x, 0.0), axis=1, keepdims=True)
    s_hi = jnp.sum(jnp.where(lo, 0.0, x), axis=1, keepdims=True)
    return jnp.where(lo, s_lo, s_hi)


def _gdn_kernel(qkv_ref, z_ref, misc_ref, cw_ref, cb_ref, dtb_ref, aneg_ref, nw_ref,
                o_ref, ext_ref, halo_ref, st_ref, *, nchunk, bsz):
    first = pl.program_id(0) == 0
    length = GDN_CHUNK
    two = 2 * length

    @pl.when(first)
    def _zero_state():
        st_ref[...] = jnp.zeros(st_ref.shape, F32)

    tri = (_lane_iota((length, length)) <= _row_iota((length, length))).astype(F32)
    lane = _lane_iota((length, LANES))
    lo = lane < HALF
    r2 = _row_iota((two, two))
    c2 = _lane_iota((two, two))
    same_blk = (r2 < length) == (c2 < length)
    low_incl = same_blk & (c2 <= r2)
    low_strict = same_blk & (c2 < r2)
    eye = (r2 == c2).astype(F32)
    top_lo = (r2 < length) == (c2 < HALF)
    hk = GDN_HEADS * GDN_DK
    qscale = GDN_DK ** -0.5
    npair = GDN_HEADS // 2

    def blockdiag(slab):
        return jnp.where(top_lo, jnp.concatenate([slab, slab], axis=0), 0.0)

    def fold(bd):
        return bd[:length] + bd[length:]

    chains = []
    for b in range(bsz):
        qkv = _causal_conv_silu(qkv_ref[b].astype(F32), ext_ref.at[b], halo_ref.at[b], cw_ref, cb_ref, first)
        misc = misc_ref[b]
        beta_all = jax.nn.sigmoid(misc)
        g_all = aneg_ref[...] * _softplus(misc + dtb_ref[...])
        for c in range(nchunk):
            r0 = c * length
            gcum = _dot_f32(tri, g_all[r0:r0 + length])
            beta = beta_all[r0:r0 + length]
            for pr in range(npair):
                h0, h1 = 2 * pr, 2 * pr + 1
                q = qkv[r0:r0 + length, pr * LANES:(pr + 1) * LANES]
                k = qkv[r0:r0 + length, hk + pr * LANES:hk + (pr + 1) * LANES]
                v = qkv[r0:r0 + length, 2 * hk + pr * LANES:2 * hk + (pr + 1) * LANES]
                qn = q * lax.rsqrt(_half_sum(q * q, lo) + EPS) * qscale
                kn = k * lax.rsqrt(_half_sum(k * k, lo) + EPS)
                g0 = gcum[:, MISC_A + h0:MISC_A + h0 + 1]
                g1 = gcum[:, MISC_A + h1:MISC_A + h1 + 1]
                gexp = jnp.where(lo, g0, g1)
                bexp = jnp.where(lo, beta[:, MISC_B + h0:MISC_B + h0 + 1],
                                 beta[:, MISC_B + h1:MISC_B + h1 + 1])
                gcol = jnp.concatenate([jnp.broadcast_to(g0, (length, two)),
                                        jnp.broadcast_to(g1, (length, two))], axis=0)
                decay = jnp.exp(jnp.where(low_incl, gcol - gcol.T, -jnp.inf))
                kb = kn * bexp
                kk2 = jnp.concatenate([kn, kn], axis=0)
                kq = _dot_nt(jnp.concatenate([blockdiag(kb), blockdiag(qn)], axis=0), kk2)
                a_mat = jnp.where(low_strict, kq[:two] * decay, 0.0)
                qk = jnp.where(low_incl, kq[two:] * decay, 0.0)
                g_last = gexp[length - 1:length, :]
                chains.append(dict(
                    b=b, c=c, pr=pr, pw=a_mat, t=eye - a_mat, qk=qk,
                    vb=v * bexp, kbg=kb * jnp.exp(gexp), q_dec=qn * jnp.exp(gexp),
                    k_tail=kn * jnp.exp(g_last - gexp), gl=jnp.exp(g_last)))

    for _ in range(int(math.log2(length)) - 1):
        for ch in chains:
            ch["pw"] = _dot(ch["pw"], ch["pw"])
        for ch in chains:
            ch["t"] = ch["t"] + _dot(ch["t"], ch["pw"])

    for ch in chains:
        ku = _dot(ch["t"], jnp.concatenate([blockdiag(ch["kbg"]), blockdiag(ch["vb"])], axis=1))
        ch["k_cum"] = fold(ku[:, :LANES])
        ch["u"] = fold(ku[:, LANES:])
    for ch in chains:
        rhs = jnp.concatenate([ch["k_cum"], ch["u"]], axis=1)
        mn = _dot_tn(ch["k_tail"], rhs)
        ch["m"] = jnp.where(top_lo, mn[:, :LANES], 0.0)
        ch["n"] = jnp.where(top_lo, mn[:, LANES:], 0.0)
        qo = _dot(ch["qk"], jnp.concatenate([blockdiag(ch["k_cum"]), blockdiag(ch["u"])], axis=1))
        ch["q_eff"] = ch["q_dec"] - fold(qo[:, :LANES])
        ch["o_loc"] = fold(qo[:, LANES:])

    states = {(b, pr): st_ref[b * npair + pr] for b in range(bsz) for pr in range(npair)}
    for c in range(nchunk):
        for ch in chains:
            if ch["c"] != c:
                continue
            b, pr = ch["b"], ch["pr"]
            st = states[(b, pr)]
            both = _dot(jnp.concatenate([ch["q_eff"], ch["m"]], axis=0), st)
            o = both[:length] + ch["o_loc"]
            states[(b, pr)] = st * ch["gl"] - both[length:] + ch["n"]
            r0 = c * length
            ms = _half_sum(o * o, lo) * (1.0 / GDN_DV)
            zz = z_ref[b, r0:r0 + length, pr * LANES:(pr + 1) * LANES].astype(F32)
            out = o * lax.rsqrt(ms + EPS) * nw_ref[...] * _silu(zz)
            o_ref[b, r0:r0 + length, pr * LANES:(pr + 1) * LANES] = out.astype(o_ref.dtype)
    for (b, pr), st in states.items():
        st_ref[b * npair + pr] = st


def _gdn_mixer(qkv, z, misc, conv_w, conv_b, dt_bias, a_log, norm_w, bsz, seq):
    nchunk = min(GDN_STEP_CHUNKS, seq // GDN_CHUNK)
    rows = nchunk * GDN_CHUNK
    dtb = jnp.zeros((1, LANES), F32).at[0, MISC_A:MISC_A + GDN_HEADS].set(dt_bias.astype(F32))
    aneg = jnp.zeros((1, LANES), F32).at[0, MISC_A:MISC_A + GDN_HEADS].set(-jnp.exp(a_log.astype(F32)))
    nw2 = jnp.concatenate([norm_w.astype(F32)] * 2).reshape(1, LANES)
    cb2 = conv_b.reshape(1, -1)
    full = lambda a: pl.BlockSpec(a.shape, lambda i: (0,) * a.ndim)
    blk = lambda w: pl.BlockSpec((bsz, rows, w), lambda i: (0, i, 0))
    out = pl.pallas_call(
        functools.partial(_gdn_kernel, nchunk=nchunk, bsz=bsz),
        grid=(seq // rows,),
        in_specs=[blk(GDN_QKV), blk(GDN_OUT), blk(LANES),
                  full(conv_w), full(cb2), full(dtb), full(aneg), full(nw2)],
        out_specs=blk(GDN_OUT),
        out_shape=jax.ShapeDtypeStruct((bsz, seq, GDN_OUT), BF16),
        scratch_shapes=[pltpu.VMEM((bsz, rows + SUBLANES, GDN_QKV), F32),
                        pltpu.VMEM((bsz, SUBLANES, GDN_QKV), F32),
                        pltpu.VMEM((bsz * (GDN_HEADS // 2), 2 * GDN_DK, LANES), F32)],
        compiler_params=_cparams(("arbitrary",)),
        name="gdn_mixer",
    )(qkv.reshape(bsz, seq, -1), z.reshape(bsz, seq, -1), misc.reshape(bsz, seq, -1),
      conv_w, cb2, dtb, aneg, nw2)
    return out.reshape(bsz * seq, GDN_OUT)


ROUTE_E0, ROUTE_E1, ROUTE_R0, ROUTE_R1, ROUTE_W0, ROUTE_W1 = range(6)


def _post_kernel(*refs, route):
    if route:
        (x_ref, a_ref, s_ref, g_ref, wa_ref, ws_ref, wg_ref, gate_ref, sh_ref, sc_ref, wrh_ref, wrl_ref,
         x1_ref, h_ref, route_ref, cnt_ref, carry_ref) = refs
    else:
        (x_ref, a_ref, s_ref, g_ref, wa_ref, ws_ref, wg_ref, gate_ref, sh_ref, sc_ref,
         x1_ref, h_ref) = refs
    mix = (lax.dot_general(a_ref[0], wa_ref[...], (((0,), (0,)), ((), ())), preferred_element_type=F32)
           + jnp.dot(s_ref[...], ws_ref[...], preferred_element_type=F32)
           + jnp.dot(g_ref[...], wg_ref[...], preferred_element_type=F32))
    x1 = x_ref[...] + gate_ref[0] * mix
    x1_ref[...] = x1
    h = _rms(x1) * (1.0 + sc_ref[0]) + sh_ref[0]
    h_ref[...] = h.astype(h_ref.dtype)
    if not route:
        return

    @pl.when(pl.program_id(0) == 0)
    def _zero_carry():
        carry_ref[...] = jnp.zeros(carry_ref.shape, F32)

    tm = h.shape[0]
    lane = _lane_iota((tm, LANES))
    h_hi = h.astype(BF16)
    h_lo = (h - h_hi.astype(F32)).astype(BF16)
    hh = jnp.dot(h_hi, jnp.concatenate([wrh_ref[...], wrl_ref[...]], axis=1), preferred_element_type=F32)
    logits = hh[:, :LANES] + (jnp.dot(h_lo, wrh_ref[...], preferred_element_type=F32) + hh[:, LANES:])
    logits = jnp.where(lane < N_EXPERTS, logits, -jnp.inf)
    lane_f = lane.astype(F32)
    m0 = jnp.max(logits, axis=1, keepdims=True)
    e0 = jnp.min(jnp.where(logits == m0, lane_f, float(LANES)), axis=1, keepdims=True)
    rest = jnp.where(lane_f == e0, -jnp.inf, logits)
    m1 = jnp.max(rest, axis=1, keepdims=True)
    e1 = jnp.min(jnp.where(rest == m1, lane_f, float(LANES)), axis=1, keepdims=True)
    ex = jnp.exp(m1 - m0)
    w0 = 1.0 / (1.0 + ex)
    w1 = ex / (1.0 + ex)
    oh0 = (lane_f == e0).astype(F32)
    oh1 = (lane_f == e1).astype(F32)
    both = oh0 + oh1
    strict = (_lane_iota((tm, tm)) < _row_iota((tm, tm))).astype(BF16)
    before = jnp.dot(strict, both.astype(BF16), preferred_element_type=F32) + carry_ref[0:1, :]
    r0 = jnp.sum(before * oh0, axis=1, keepdims=True)
    r1 = jnp.sum(before * oh1, axis=1, keepdims=True)
    carry_ref[0:1, :] = carry_ref[0:1, :] + jnp.sum(both, axis=0, keepdims=True)
    slab = jnp.zeros((tm, LANES), F32)
    for pos, val in ((ROUTE_E0, e0), (ROUTE_E1, e1), (ROUTE_R0, r0), (ROUTE_R1, r1),
                     (ROUTE_W0, w0), (ROUTE_W1, w1)):
        slab = jnp.where(lane == pos, val, slab)
    route_ref[...] = slab
    cnt_ref[...] = carry_ref[...]


def _post_mixer(x2d, mla_o, ssd_y, gdn_o, w_out, gate, shift, scale, seq, w_router=None):
    t, d = x2d.shape
    route = w_router is not None
    tm = min(ROUTE_TILE if route else TM_PROJ, seq)
    per_b = seq // tm
    wa = w_out[:MLA_OUT].astype(BF16)
    ws = w_out[MLA_OUT:MLA_OUT + SSD_D_INNER].astype(BF16)
    wg = w_out[MLA_OUT + SSD_D_INNER:].astype(BF16)
    row = lambda w: pl.BlockSpec((tm, w), lambda i: (i, 0))
    full = lambda a: pl.BlockSpec(a.shape, lambda i: (0,) * a.ndim)
    mod = pl.BlockSpec((1, 1, d), lambda i: (i // per_b, 0, 0))
    att = pl.BlockSpec((1, MLA_OUT, tm), lambda i: (i // per_b, 0, i % per_b))
    in_specs = [row(d), att, row(SSD_D_INNER), row(GDN_OUT), full(wa), full(ws), full(wg),
                mod, mod, mod]
    args = [x2d, mla_o, ssd_y, gdn_o, wa, ws, wg, gate, shift, scale]
    out_specs = [row(d), row(d)]
    out_shape = [jax.ShapeDtypeStruct((t, d), F32), jax.ShapeDtypeStruct((t, d), F32 if route else BF16)]
    scratch = []
    if route:
        wr = jnp.zeros((d, LANES), F32).at[:, :N_EXPERTS].set(w_router.astype(F32))
        wr_hi = wr.astype(BF16)
        wr_lo = (wr - wr_hi.astype(F32)).astype(BF16)
        in_specs += [full(wr_hi), full(wr_lo)]
        args += [wr_hi, wr_lo]
        out_specs += [row(LANES), pl.BlockSpec((SUBLANES, LANES), lambda i: (0, 0))]
        out_shape += [jax.ShapeDtypeStruct((t, LANES), F32), jax.ShapeDtypeStruct((SUBLANES, LANES), F32)]
        scratch = [pltpu.VMEM((SUBLANES, LANES), F32)]
    return pl.pallas_call(
        functools.partial(_post_kernel, route=route),
        grid=(t // tm,),
        in_specs=in_specs, out_specs=out_specs, out_shape=out_shape, scratch_shapes=scratch,
        compiler_params=_cparams(("arbitrary",) if route else ("parallel",)),
        name="post_mixer_route" if route else "post_mixer",
    )(*args)


def _finish(x, final, fn_ref):
    return _rms(x) * fn_ref[...] if final else x


def _ffn_kernel(h_ref, x1_ref, gate_ref, fn_ref, wg_ref, wu_ref, wd_ref, o_ref, *, final):
    f = pl.program_id(1)
    h = h_ref[...]
    act = _silu(jnp.dot(h, wg_ref[...], preferred_element_type=F32)) * jnp.dot(
        h, wu_ref[...], preferred_element_type=F32)
    part = jnp.dot(act.astype(BF16), wd_ref[...], preferred_element_type=F32)

    @pl.when(f == 0)
    def _set():
        o_ref[...] = part

    @pl.when(f > 0)
    def _add():
        o_ref[...] += part

    @pl.when(f == pl.num_programs(1) - 1)
    def _residual():
        o_ref[...] = _finish(x1_ref[...] + gate_ref[0] * o_ref[...], final, fn_ref)


def _ffn_tile(dff):
    for cand in (2816, 1408, 1024, 512, 256, 128):
        if dff % cand == 0:
            return cand
    raise ValueError(f"unsupported d_ff {dff}")


def _dense_ffn(h, x1, gate, w_gate, w_up, w_down, final_norm, final, seq):
    t, d = x1.shape
    dff = w_gate.shape[1]
    tm = min(TM_FFN, seq)
    tf = _ffn_tile(dff)
    per_b = seq // tm
    fn = final_norm.reshape(1, d).astype(F32)
    return pl.pallas_call(
        functools.partial(_ffn_kernel, final=final),
        grid=(t // tm, dff // tf),
        in_specs=[
            pl.BlockSpec((tm, d), lambda i, f: (i, 0)),
            pl.BlockSpec((tm, d), lambda i, f: (i, 0)),
            pl.BlockSpec((1, 1, d), lambda i, f: (i // per_b, 0, 0)),
            pl.BlockSpec((1, d), lambda i, f: (0, 0)),
            pl.BlockSpec((d, tf), lambda i, f: (0, f)),
            pl.BlockSpec((d, tf), lambda i, f: (0, f)),
            pl.BlockSpec((tf, d), lambda i, f: (f, 0)),
        ],
        out_specs=pl.BlockSpec((tm, d), lambda i, f: (i, 0)),
        out_shape=jax.ShapeDtypeStruct((t, d), F32),
        compiler_params=_cparams(("parallel", "arbitrary")),
        name="dense_ffn",
    )(h, x1, gate, fn, w_gate.astype(BF16), w_up.astype(BF16), w_down.astype(BF16))


def _row_copy(src, dst, sem):
    return pltpu.make_async_copy(src, dst, sem)


def _dispatch_kernel(fill_ref, dest_ref, h_ref, xs_hbm, zbuf_ref, sem, zsem, *, tm, bm):
    @pl.when(pl.program_id(0) == 0)
    def _zero_padding():
        zbuf_ref[...] = jnp.zeros(zbuf_ref.shape, F32)

        def row_fill(r):
            return _row_copy(zbuf_ref.at[pl.ds(0, 1), :], xs_hbm.at[pl.ds(r, 1), :], zsem)

        def block_fill(blk):
            return _row_copy(zbuf_ref, xs_hbm.at[pl.ds(pl.multiple_of(blk * bm, bm), bm), :], zsem)

        spans = [(fill_ref[2 * e], fill_ref[2 * e + 1], row_fill) for e in range(N_EXPERTS)]
        spans.append((fill_ref[2 * N_EXPERTS], fill_ref[2 * N_EXPERTS + 1], block_fill))
        for lo, hi, fill in spans:
            lax.fori_loop(lo, hi, lambda r, c, fill=fill: (fill(r).start(), c)[1], 0)
        for lo, hi, fill in spans:
            lax.fori_loop(lo, hi, lambda r, c, fill=fill: (fill(r).wait(), c)[1], 0)

    def issue(g, carry):
        for j in range(ROW_DMA_UNROLL):
            t = g * ROW_DMA_UNROLL + j
            for kk in range(TOP_K):
                d = dest_ref[0, 0, TOP_K * t + kk]
                _row_copy(h_ref.at[pl.ds(t, 1), :], xs_hbm.at[pl.ds(d, 1), :], sem).start(priority=kk)
        return carry

    lax.fori_loop(0, tm // ROW_DMA_UNROLL, issue, 0)

    def drain(g, carry):
        for _ in range(ROW_DMA_UNROLL * TOP_K):
            _row_copy(h_ref.at[pl.ds(0, 1), :], xs_hbm.at[pl.ds(0, 1), :], sem).wait()
        return carry

    lax.fori_loop(0, tm // ROW_DMA_UNROLL, drain, 0)


def _moe_dispatch(h_f32, dest, fill, n_rows):
    t, d = h_f32.shape
    tm = min(MOE_TOK_TILE, t)
    bm = MOE_ROWS
    dest3 = dest.reshape(t // tm, 1, TOP_K * tm)
    return pl.pallas_call(
        functools.partial(_dispatch_kernel, tm=tm, bm=bm),
        grid=(t // tm,),
        in_specs=[
            pl.BlockSpec(memory_space=pltpu.SMEM),
            pl.BlockSpec((1, 1, TOP_K * tm), lambda i: (i, 0, 0), memory_space=pltpu.SMEM),
            pl.BlockSpec((tm, d), lambda i: (i, 0)),
        ],
        out_specs=pl.BlockSpec(memory_space=pl.ANY),
        out_shape=jax.ShapeDtypeStruct((n_rows, d), F32),
        scratch_shapes=[pltpu.VMEM((bm, d), F32), pltpu.SemaphoreType.DMA(()), pltpu.SemaphoreType.DMA(())],
        compiler_params=_cparams(("arbitrary",)),
        name="moe_dispatch",
    )(fill, dest3, h_f32)


def _experts_kernel(be_ref, nused_ref, x_ref, wg_ref, wu_ref, wd_ref, y_ref):
    i = pl.program_id(0)
    f = pl.program_id(1)

    @pl.when(f == 0)
    def _zero():
        y_ref[...] = jnp.zeros(y_ref.shape, F32)

    @pl.when(i < nused_ref[0])
    def _compute():
        xb = x_ref[...].astype(BF16)
        act = _silu(jnp.dot(xb, wg_ref[0], preferred_element_type=F32)) * jnp.dot(
            xb, wu_ref[0], preferred_element_type=F32)
        y_ref[...] += jnp.dot(act.astype(BF16), wd_ref[0], preferred_element_type=F32)


def _moe_experts(xs, block_expert, n_used, w_gate, w_up, w_down):
    n_rows, d = xs.shape
    dff = w_gate.shape[2]
    bm = MOE_ROWS
    tf = MOE_FF_TILE if dff % MOE_FF_TILE == 0 else dff

    def ff(i, f, nu):
        return jnp.where(i < nu[0], f, 0)

    return pl.pallas_call(
        _experts_kernel,
        grid_spec=pltpu.PrefetchScalarGridSpec(
            num_scalar_prefetch=2,
            grid=(n_rows // bm, dff // tf),
            in_specs=[
                pl.BlockSpec((bm, d), lambda i, f, be, nu: (i, 0)),
                pl.BlockSpec((1, d, tf), lambda i, f, be, nu: (be[i], 0, ff(i, f, nu))),
                pl.BlockSpec((1, d, tf), lambda i, f, be, nu: (be[i], 0, ff(i, f, nu))),
                pl.BlockSpec((1, tf, d), lambda i, f, be, nu: (be[i], ff(i, f, nu), 0)),
            ],
            out_specs=pl.BlockSpec((bm, d), lambda i, f, be, nu: (i, 0)),
        ),
        out_shape=jax.ShapeDtypeStruct((n_rows, d), F32),
        compiler_params=_cparams(("parallel", "arbitrary")),
        name="moe_experts",
    )(block_expert, n_used, xs, w_gate.astype(BF16), w_up.astype(BF16), w_down.astype(BF16))


def _combine_kernel(dest_ref, dnext_ref, x1_ref, route_ref, gate_ref, fn_ref, ys_hbm, o_ref, buf_ref, sem,
                    *, tm, final):
    i = pl.program_id(0)
    slot = lax.rem(i, 2)

    def gather(d_ref, s):
        def issue(g, carry):
            for j in range(ROW_DMA_UNROLL):
                t = g * ROW_DMA_UNROLL + j
                for kk in range(TOP_K):
                    d = d_ref[0, 0, TOP_K * t + kk]
                    _row_copy(ys_hbm.at[pl.ds(d, 1), :], buf_ref.at[s, kk, pl.ds(t, 1), :],
                              sem.at[s]).start(priority=kk)
            return carry

        lax.fori_loop(0, tm // ROW_DMA_UNROLL, issue, 0)

    @pl.when(i == 0)
    def _first_tile():
        gather(dest_ref, slot)

    @pl.when(i + 1 < pl.num_programs(0))
    def _next_tile():
        gather(dnext_ref, 1 - slot)

    def drain(g, carry):
        for _ in range(ROW_DMA_UNROLL * TOP_K):
            _row_copy(ys_hbm.at[pl.ds(0, 1), :], buf_ref.at[slot, 0, pl.ds(0, 1), :], sem.at[slot]).wait()
        return carry

    lax.fori_loop(0, tm // ROW_DMA_UNROLL, drain, 0)
    route = route_ref[...]
    w0 = route[:, ROUTE_W0:ROUTE_W0 + 1]
    w1 = route[:, ROUTE_W1:ROUTE_W1 + 1]
    f = w0 * buf_ref[slot, 0] + w1 * buf_ref[slot, 1]
    o_ref[...] = _finish(x1_ref[...] + gate_ref[0] * f, final, fn_ref)


def _moe_combine(ys, dest, x1, route, gate, final_norm, final, seq):
    t, d = x1.shape
    tm = min(MOE_TOK_TILE, seq)
    per_b = seq // tm
    dest3 = dest.reshape(t // tm, 1, TOP_K * tm)
    fn = final_norm.reshape(1, d).astype(F32)
    nt = t // tm
    return pl.pallas_call(
        functools.partial(_combine_kernel, tm=tm, final=final),
        grid=(nt,),
        in_specs=[
            pl.BlockSpec((1, 1, TOP_K * tm), lambda i: (i, 0, 0), memory_space=pltpu.SMEM),
            pl.BlockSpec((1, 1, TOP_K * tm), lambda i: (jnp.minimum(i + 1, nt - 1), 0, 0),
                         memory_space=pltpu.SMEM),
            pl.BlockSpec((tm, d), lambda i: (i, 0)),
            pl.BlockSpec((tm, LANES), lambda i: (i, 0)),
            pl.BlockSpec((1, 1, d), lambda i: (i // per_b, 0, 0)),
            pl.BlockSpec((1, d), lambda i: (0, 0)),
            pl.BlockSpec(memory_space=pl.ANY),
        ],
        out_specs=pl.BlockSpec((tm, d), lambda i: (i, 0)),
        out_shape=jax.ShapeDtypeStruct((t, d), F32),
        scratch_shapes=[pltpu.VMEM((2, TOP_K, tm, d), F32), pltpu.SemaphoreType.DMA((2,))],
        compiler_params=_cparams(("arbitrary",)),
        name="moe_combine",
    )(dest3, dest3, x1, route, gate, fn, ys)


def _moe_ffn(h_f32, x1, route, counts_slab, gate, w_gate, w_up, w_down, final_norm, final, seq):
    t, d = x1.shape
    bm = MOE_ROWS
    n_blocks = (t * TOP_K + bm - 1) // bm + N_EXPERTS
    n_rows = n_blocks * bm
    counts = counts_slab[0, :N_EXPERTS].astype(I32)
    padded = ((counts + bm - 1) // bm) * bm
    pend = jnp.cumsum(padded)
    pstart = pend - padded
    e = route[:, ROUTE_E0:ROUTE_E1 + 1].astype(I32)
    rank = route[:, ROUTE_R0:ROUTE_R1 + 1].astype(I32)
    dest = (jnp.sum(jnp.where(e[..., None] == jnp.arange(N_EXPERTS), pstart, 0), axis=-1) + rank)
    blk_start = jnp.arange(n_blocks, dtype=I32) * bm
    block_expert = jnp.minimum(jnp.sum(blk_start[:, None] >= pend[None, :], axis=1), N_EXPERTS - 1).astype(I32)
    n_used = (pend[-1:] // bm).astype(I32)
    fill = jnp.concatenate([jnp.stack([pstart + counts, pend], axis=1).reshape(-1),
                            n_used, jnp.full((1,), n_blocks, I32)]).astype(I32)
    xs = _moe_dispatch(h_f32, dest, fill, n_rows)
    ys = _moe_experts(xs, block_expert, n_used, w_gate, w_up, w_down)
    return _moe_combine(ys, dest, x1, route, gate, final_norm, final, seq)


def kernel(x, c, ada_w, ada_b, w_in, mla_q_norm, mla_w_uq, mla_kv_norm, mla_w_ukv, ssd_conv_w, ssd_conv_b, ssd_dt_bias, ssd_a_log, ssd_d, ssd_norm, gdn_conv_w, gdn_conv_b, gdn_dt_bias, gdn_a_log, gdn_norm, w_out, ffn_w_gate, ffn_w_up, ffn_w_down, moe_router, moe_w_gate, moe_w_up, moe_w_down, final_norm):
    bsz, seq, d = x.shape
    depth = w_in.shape[0]
    t = bsz * seq
    mod = _ada_modulation(c, ada_w, ada_b)
    tables = _rope_tables(seq)
    xf = x.reshape(t, d)
    for l in range(depth):
        sh1, sc1, g1, sh2, sc2, g2 = [m.reshape(bsz, 1, d) for m in jnp.split(mod[l], 6, axis=-1)]
        mla_c, misc, ssd_z, ssd_xbc, gdn_qkv, gdn_z = _in_proj(xf, sh1, sc1, _prep_w_in(w_in[l]), seq)
        wqa, wqb, wkv = _prep_mla_weights(mla_w_uq[l], mla_w_ukv[l])
        qt, k, vt = _mla_proj(mla_c, misc, tables, mla_q_norm[l], mla_kv_norm[l], wqa, wqb, wkv, bsz, seq)
        mla_o = _flash_attention(qt, k, vt, bsz, seq)
        ssd_y = _ssd_mixer(ssd_z, ssd_xbc, misc, ssd_conv_w[l], ssd_conv_b[l], ssd_dt_bias[l],
                           ssd_a_log[l], ssd_d[l], ssd_norm[l], bsz, seq)
        gdn_o = _gdn_mixer(gdn_qkv, gdn_z, misc, gdn_conv_w[l], gdn_conv_b[l], gdn_dt_bias[l],
                           gdn_a_log[l], gdn_norm[l], bsz, seq)
        final = l == depth - 1
        if l % 2 == 0:
            x1, h2 = _post_mixer(xf, mla_o, ssd_y, gdn_o, w_out[l], g1, sh2, sc2, seq)
            xf = _dense_ffn(h2, x1, g2, ffn_w_gate[l // 2], ffn_w_up[l // 2], ffn_w_down[l // 2],
                            final_norm, final, seq)
        else:
            x1, h2, route, counts = _post_mixer(xf, mla_o, ssd_y, gdn_o, w_out[l], g1, sh2, sc2, seq,
                                                w_router=moe_router[l // 2])
            xf = _moe_ffn(h2, x1, route, counts, g2, moe_w_gate[l // 2], moe_w_up[l // 2],
                          moe_w_down[l // 2], final_norm, final, seq)
    return xf.reshape(bsz, seq, d)
```

```python
import functools
import math

import jax
import jax.numpy as jnp
import numpy as np
from jax import lax
from jax.experimental import pallas as pl
from jax.experimental.pallas import tpu as pltpu

F32 = jnp.float32
BF16 = jnp.bfloat16
I32 = jnp.int32
HIGHEST = lax.Precision.HIGHEST

D_MODEL = 1024
EPS = 1e-6
CONV_WIDTH = 4
MLA_HEADS, MLA_Q_RANK, MLA_KV_RANK = 4, 256, 128
MLA_NOPE, MLA_ROPE, MLA_V = 64, 32, 64
ROPE_THETA = 10000.0
SSD_HEADS, SSD_HEAD_DIM, SSD_GROUPS, SSD_STATE, SSD_CHUNK = 8, 64, 2, 64, 128
SSD_D_INNER = SSD_HEADS * SSD_HEAD_DIM
SSD_CONV_DIM = SSD_D_INNER + 2 * SSD_GROUPS * SSD_STATE
GDN_HEADS, GDN_DK, GDN_DV, GDN_CHUNK = 4, 64, 64, 64
GDN_QKV = GDN_HEADS * (2 * GDN_DK + GDN_DV)
GDN_OUT = GDN_HEADS * GDN_DV
MLA_IN = MLA_Q_RANK + MLA_KV_RANK + MLA_ROPE
MLA_OUT = MLA_HEADS * MLA_V
SSD_IN = SSD_D_INNER + SSD_CONV_DIM + SSD_HEADS
GDN_IN = GDN_QKV + GDN_HEADS * GDN_DV + 2 * GDN_HEADS
N_EXPERTS, TOP_K = 8, 2

LANES = 128
SUBLANES = 8
HALF = LANES // 2
VMEM_LIMIT_BYTES = 56 * 1024 * 1024

MISC_DT = 0
MISC_B = 8
MISC_A = 12
MISC_KR = 64
MISC_KR_SW = 96
HEAD_PAD = 128

TM_PROJ = 1024
IN_PROJ_SUB = 512
TM_FFN = 512
ATT_Q_BLOCK = 1024
ATT_K_BLOCK = 2048
ATT_Q_COLS = 256
ATT_K_ROWS = 512
ATT_PIPELINE = 4
SSD_STEP_CHUNKS = 2
GDN_STEP_CHUNKS = 8
MOE_ROWS = 512
MOE_FF_TILE = 1792
MOE_TOK_TILE = 1024
ROUTE_TILE = 512
ROW_DMA_UNROLL = 8


def _cparams(sem):
    return pltpu.CompilerParams(dimension_semantics=sem, vmem_limit_bytes=VMEM_LIMIT_BYTES)


def _lane_iota(shape):
    return lax.broadcasted_iota(I32, shape, len(shape) - 1)


def _row_iota(shape):
    return lax.broadcasted_iota(I32, shape, len(shape) - 2)


def _softplus(x):
    return jnp.maximum(x, 0.0) + jnp.log1p(jnp.exp(-jnp.abs(x)))


def _silu(x):
    return x * jax.nn.sigmoid(x)


def _rms(x):
    return x * lax.rsqrt(jnp.mean(x * x, axis=-1, keepdims=True) + EPS)


def _dot(a, b):
    return jnp.dot(a.astype(BF16), b.astype(BF16), preferred_element_type=F32)


def _dot_nt(a, b):
    return lax.dot_general(a.astype(BF16), b.astype(BF16), (((1,), (1,)), ((), ())),
                           preferred_element_type=F32)


def _dot_tn(a, b):
    return lax.dot_general(a.astype(BF16), b.astype(BF16), (((0,), (0,)), ((), ())),
                           preferred_element_type=F32)


def _dot_f32(a, b):
    return jnp.dot(a, b, precision=HIGHEST, preferred_element_type=F32)


def _ada_kernel(c_ref, w_ref, b_ref, o_ref):
    c_act = _silu(c_ref[...])
    o_ref[0] = _dot_f32(c_act, w_ref[0]) + b_ref[0]


def _ada_modulation(c, ada_w, ada_b):
    depth, d, n = ada_w.shape
    bsz = c.shape[0]
    rows = max(SUBLANES, bsz)
    c_pad = jnp.zeros((rows, d), F32).at[:bsz].set(c)
    tn = 1536
    out = pl.pallas_call(
        _ada_kernel,
        grid=(depth, n // tn),
        in_specs=[
            pl.BlockSpec((rows, d), lambda l, j: (0, 0)),
            pl.BlockSpec((1, d, tn), lambda l, j: (l, 0, j)),
            pl.BlockSpec((1, 1, tn), lambda l, j: (l, 0, j)),
        ],
        out_specs=pl.BlockSpec((1, rows, tn), lambda l, j: (l, 0, j)),
        out_shape=jax.ShapeDtypeStruct((depth, rows, n), F32),
        compiler_params=_cparams(("parallel", "parallel")),
        name="ada_modulation",
    )(c_pad, ada_w, ada_b.reshape(depth, 1, n))
    return out[:, :bsz]


IN_SLABS = (("mla_c", 384, F32), ("misc", 128, F32), ("ssd_z", 512, BF16), ("ssd_xbc", 768, BF16),
            ("gdn_qkv", 768, BF16), ("gdn_z", 256, BF16))
IN_WIDTH = sum(w for _, w, _ in IN_SLABS)


def _prep_w_in(w):
    d = w.shape[0]
    o_ssd = MLA_IN
    o_gdn = MLA_IN + SSD_IN
    w_kr = w[:, MLA_Q_RANK + MLA_KV_RANK:MLA_IN]
    half = MLA_ROPE // 2
    w_kr_sw = jnp.concatenate([-w_kr[:, half:], w_kr[:, :half]], axis=1)
    ssd_dt = w[:, o_ssd + SSD_D_INNER + SSD_CONV_DIM:o_ssd + SSD_IN]
    gdn_ba = w[:, o_gdn + GDN_QKV + GDN_OUT:o_gdn + GDN_IN]
    misc = jnp.concatenate(
        [ssd_dt, gdn_ba, jnp.zeros((d, MISC_KR - MISC_A - GDN_HEADS), w.dtype), w_kr, w_kr_sw], axis=1)
    cols = [
        w[:, :MLA_Q_RANK + MLA_KV_RANK], misc,
        w[:, o_ssd:o_ssd + SSD_D_INNER],
        w[:, o_ssd + SSD_D_INNER:o_ssd + SSD_D_INNER + SSD_CONV_DIM],
        w[:, o_gdn:o_gdn + GDN_QKV],
        w[:, o_gdn + GDN_QKV:o_gdn + GDN_QKV + GDN_OUT],
    ]
    return jnp.concatenate(cols, axis=1).astype(BF16)


def _in_proj_kernel(x_ref, sh_ref, sc_ref, w_ref, *o_refs):
    rows = x_ref.shape[0]
    sub = min(IN_PROJ_SUB, rows)

    def normed(r0):
        return (_rms(x_ref[r0:r0 + sub, :]) * (1.0 + sc_ref[0]) + sh_ref[0]).astype(BF16)

    nxt = normed(0)
    for r0 in range(0, rows, sub):
        hb = nxt
        if r0 + sub < rows:
            nxt = normed(r0 + sub)
        off = 0
        for o_ref, (_, width, _) in zip(o_refs, IN_SLABS):
            o_ref[r0:r0 + sub, :] = jnp.dot(
                hb, w_ref[:, off:off + width], preferred_element_type=F32).astype(o_ref.dtype)
            off += width


def _in_proj(x2d, shift, scale, w_perm, seq):
    t, d = x2d.shape
    tm = min(TM_PROJ, seq)
    per_b = seq // tm
    return pl.pallas_call(
        _in_proj_kernel,
        grid=(t // tm,),
        in_specs=[
            pl.BlockSpec((tm, d), lambda i: (i, 0)),
            pl.BlockSpec((1, 1, d), lambda i: (i // per_b, 0, 0)),
            pl.BlockSpec((1, 1, d), lambda i: (i // per_b, 0, 0)),
            pl.BlockSpec((d, IN_WIDTH), lambda i: (0, 0)),
        ],
        out_specs=[pl.BlockSpec((tm, w), lambda i: (i, 0)) for _, w, _ in IN_SLABS],
        out_shape=[jax.ShapeDtypeStruct((t, w), dt) for _, w, dt in IN_SLABS],
        compiler_params=_cparams(("parallel",)),
        name="in_proj",
    )(x2d, shift, scale, w_perm)


def _rope_tables(seq):
    pos = jnp.arange(seq, dtype=F32)
    inv_freq = ROPE_THETA ** (-jnp.arange(0, MLA_ROPE, 2, dtype=F32) / MLA_ROPE)
    ang = pos[:, None] * inv_freq[None, :]
    cos, sin = jnp.cos(ang), jnp.sin(ang)
    zeros_l = jnp.zeros((seq, MLA_NOPE), F32)
    zeros_r = jnp.zeros((seq, HEAD_PAD - MLA_NOPE - MLA_ROPE), F32)
    cos_k = jnp.concatenate([zeros_l, cos, cos, zeros_r], axis=1)
    sin_k = jnp.concatenate([zeros_l, sin, sin, zeros_r], axis=1)
    scale = (MLA_NOPE + MLA_ROPE) ** -0.5 * math.log2(math.e)
    cos_q = scale * jnp.concatenate([jnp.ones((seq, MLA_NOPE), F32), cos, cos, zeros_r], axis=1)
    sin_q = scale * sin_k
    return cos_q, sin_q, cos_k, sin_k


def _prep_mla_weights(w_uq, w_ukv):
    r = w_uq.shape[0]
    hq = MLA_NOPE + MLA_ROPE
    half = MLA_ROPE // 2
    qa, qb = [], []
    for h in range(MLA_HEADS):
        nope = w_uq[:, h * hq:h * hq + MLA_NOPE]
        rope = w_uq[:, h * hq + MLA_NOPE:(h + 1) * hq]
        rope_sw = jnp.concatenate([-rope[:, half:], rope[:, :half]], axis=1)
        pad = jnp.zeros((r, HEAD_PAD - hq), w_uq.dtype)
        qa += [nope, rope, pad]
        qb += [jnp.zeros((r, MLA_NOPE), w_uq.dtype), rope_sw, pad]
    rk = w_ukv.shape[0]
    hk = MLA_NOPE + MLA_V
    kcols, vcols = [], []
    for h in range(MLA_HEADS):
        kcols += [w_ukv[:, h * hk:h * hk + MLA_NOPE], jnp.zeros((rk, HEAD_PAD - MLA_NOPE), w_ukv.dtype)]
        vcols += [w_ukv[:, h * hk + MLA_NOPE:(h + 1) * hk], jnp.zeros((rk, HEAD_PAD - MLA_V), w_ukv.dtype)]
    return (jnp.concatenate(qa, axis=1).astype(BF16), jnp.concatenate(qb, axis=1).astype(BF16),
            jnp.concatenate(kcols + vcols, axis=1).astype(BF16))


def _mla_proj_kernel(c_ref, misc_ref, cq_ref, sq_ref, ck_ref, sk_ref, qn_ref, kvn_ref,
                     wqa_ref, wqb_ref, wkv_ref, qt_ref, k_ref, vt_ref):
    c = c_ref[...]
    cq = (_rms(c[:, :MLA_Q_RANK]) * qn_ref[...]).astype(BF16)
    ckv = (_rms(c[:, MLA_Q_RANK:]) * kvn_ref[...]).astype(BF16)
    qa = jnp.dot(cq, wqa_ref[...], preferred_element_type=F32)
    qb = jnp.dot(cq, wqb_ref[...], preferred_element_type=F32)
    cos_q = jnp.concatenate([cq_ref[...]] * MLA_HEADS, axis=1)
    sin_q = jnp.concatenate([sq_ref[...]] * MLA_HEADS, axis=1)
    qt_ref[0] = (qa * cos_q + qb * sin_q).T.astype(BF16)
    kv = jnp.dot(ckv, wkv_ref[...], preferred_element_type=F32)
    misc = misc_ref[...]
    kr = misc * ck_ref[...] + pltpu.roll(misc, LANES - (MISC_KR_SW - MISC_KR), 1) * sk_ref[...]
    kw = MLA_HEADS * HEAD_PAD
    k_ref[0] = (kv[:, :kw] + jnp.concatenate([kr] * MLA_HEADS, axis=1)).astype(BF16)
    v = kv[:, kw:]
    v = jnp.where(_lane_iota(v.shape) % HEAD_PAD == MLA_V, 1.0, v)
    vt_ref[0] = v.T.astype(BF16)


def _mla_proj(mla_c, misc, tables, q_norm, kv_norm, wqa, wqb, wkv, bsz, seq):
    t = mla_c.shape[0]
    tm = min(TM_PROJ, seq)
    per_b = seq // tm
    kw = MLA_HEADS * HEAD_PAD
    tab_spec = pl.BlockSpec((tm, HEAD_PAD), lambda i: (i % per_b, 0))
    full = lambda a: pl.BlockSpec(a.shape, lambda i: (0,) * a.ndim)
    qn = q_norm.reshape(1, -1)
    kvn = kv_norm.reshape(1, -1)
    t_spec = pl.BlockSpec((1, kw, tm), lambda i: (i // per_b, 0, i % per_b))
    return pl.pallas_call(
        _mla_proj_kernel,
        grid=(t // tm,),
        in_specs=[
            pl.BlockSpec((tm, mla_c.shape[1]), lambda i: (i, 0)),
            pl.BlockSpec((tm, LANES), lambda i: (i, 0)),
            tab_spec, tab_spec, tab_spec, tab_spec,
            full(qn), full(kvn), full(wqa), full(wqb), full(wkv),
        ],
        out_specs=[t_spec, pl.BlockSpec((1, tm, kw), lambda i: (i // per_b, i % per_b, 0)), t_spec],
        out_shape=[jax.ShapeDtypeStruct((bsz, kw, seq), BF16), jax.ShapeDtypeStruct((bsz, seq, kw), BF16),
                   jax.ShapeDtypeStruct((bsz, kw, seq), BF16)],
        compiler_params=_cparams(("parallel",)),
        name="mla_proj",
    )(mla_c, misc, *tables, qn, kvn, wqa, wqb, wkv)


def _flash_diag_offsets(qb, kb):
    return sorted({((qi * qb) // kb) * kb - qi * qb for qi in range(max(kb // qb, 1))})


def _flash_kernel(qi_tab, ki_tab, kind_tab, qt_ref, k_ref, vt_ref, o_ref, m_ref, acc_ref, *, qb, kb, rq, ck):
    del qi_tab
    p = pl.program_id(1)
    ki = ki_tab[p]
    kind = kind_tab[p]

    @pl.when(ki == 0)
    def _init():
        m_ref[...] = jnp.full(m_ref.shape, -jnp.inf, F32)
        acc_ref[...] = jnp.zeros(acc_ref.shape, F32)

    def step(rel):
        units = []
        for rc in range(qb // rq):
            for kc in range(kb // ck):
                if rel is not None and rel + kc * ck > rc * rq + rq - 1:
                    continue
                for h in range(MLA_HEADS):
                    units.append((h, rc * rq, kc * ck))

        def scores(unit):
            h, c0, k0 = unit
            kk = k_ref[0, k0:k0 + ck, h * HEAD_PAD:(h + 1) * HEAD_PAD]
            qt = qt_ref[0, h * HEAD_PAD:(h + 1) * HEAD_PAD, c0:c0 + rq]
            s = jnp.dot(kk, qt, preferred_element_type=F32)
            if rel is not None and rel + k0 + ck - 1 > c0:
                key = lax.broadcasted_iota(I32, s.shape, 0) + (k0 + rel)
                qry = lax.broadcasted_iota(I32, s.shape, 1) + c0
                s = jnp.where(key <= qry, s, -jnp.inf)
            return s

        def update(unit, s):
            h, c0, k0 = unit
            m_prev = m_ref[h, :, c0:c0 + rq]
            m_new = jnp.maximum(m_prev, jnp.max(s, axis=0, keepdims=True))
            alpha = jnp.exp2(m_prev - m_new)
            pt = jnp.exp2(s - m_new).astype(BF16)
            m_ref[h, :, c0:c0 + rq] = m_new
            acc_ref[h, :, c0:c0 + rq] = acc_ref[h, :, c0:c0 + rq] * alpha + jnp.dot(
                vt_ref[0, h * HEAD_PAD:(h + 1) * HEAD_PAD, k0:k0 + ck], pt, preferred_element_type=F32)

        pending = [scores(u) for u in units[:ATT_PIPELINE]]
        for n, unit in enumerate(units):
            s = pending.pop(0)
            if n + ATT_PIPELINE < len(units):
                pending.append(scores(units[n + ATT_PIPELINE]))
            update(unit, s)

    @pl.when(kind == 0)
    def _off_diagonal():
        step(None)

    for n, rel in enumerate(_flash_diag_offsets(qb, kb)):
        @pl.when(kind == n + 1)
        def _diagonal(rel=rel):
            step(rel)
            for h in range(MLA_HEADS):
                acc = acc_ref[h]
                o_ref[0, h * MLA_V:(h + 1) * MLA_V, :] = (
                    acc[:MLA_V] / acc[MLA_V:MLA_V + 1]).astype(o_ref.dtype)


def _flash_attention(qt, k, vt, bsz, seq):
    qb = min(ATT_Q_BLOCK, seq)
    kb = min(ATT_K_BLOCK, seq)
    rq = min(ATT_Q_COLS, qb)
    ck = min(ATT_K_ROWS, kb)
    offsets = _flash_diag_offsets(qb, kb)
    qi_l, ki_l, kind_l = [], [], []
    for qi in range(seq // qb):
        for kj in range((qi * qb + qb - 1) // kb + 1):
            qi_l.append(qi)
            ki_l.append(kj)
            visible = (kj + 1) * kb - 1 <= qi * qb
            kind_l.append(0 if visible else 1 + offsets.index(kj * kb - qi * qb))
    tabs = [jnp.asarray(x, I32) for x in (qi_l, ki_l, kind_l)]
    kw = MLA_HEADS * HEAD_PAD
    return pl.pallas_call(
        functools.partial(_flash_kernel, qb=qb, kb=kb, rq=rq, ck=ck),
        grid_spec=pltpu.PrefetchScalarGridSpec(
            num_scalar_prefetch=3,
            grid=(bsz, len(qi_l)),
            in_specs=[
                pl.BlockSpec((1, kw, qb), lambda b, p, qt_, kt_, kd_: (b, 0, qt_[p])),
                pl.BlockSpec((1, kb, kw), lambda b, p, qt_, kt_, kd_: (b, kt_[p], 0)),
                pl.BlockSpec((1, kw, kb), lambda b, p, qt_, kt_, kd_: (b, 0, kt_[p])),
            ],
            out_specs=pl.BlockSpec((1, MLA_OUT, qb), lambda b, p, qt_, kt_, kd_: (b, 0, qt_[p])),
            scratch_shapes=[pltpu.VMEM((MLA_HEADS, 1, qb), F32),
                            pltpu.VMEM((MLA_HEADS, HEAD_PAD, qb), F32)],
        ),
        out_shape=jax.ShapeDtypeStruct((bsz, MLA_OUT, seq), BF16),
        compiler_params=_cparams(("parallel", "arbitrary")),
        name="mla_flash",
    )(*tabs, qt, k, vt)


def _causal_conv_silu(cur, ext_ref, halo_ref, cw_ref, cb_ref, first):
    rows = cur.shape[0]

    @pl.when(first)
    def _zero_halo():
        halo_ref[...] = jnp.zeros(halo_ref.shape, F32)

    ext_ref[0:SUBLANES, :] = halo_ref[...]
    ext_ref[SUBLANES:SUBLANES + rows, :] = cur
    halo_ref[...] = cur[rows - SUBLANES:, :]
    acc = cb_ref[...] + cw_ref[CONV_WIDTH - 1:CONV_WIDTH, :] * cur
    for j in range(CONV_WIDTH - 1):
        start = SUBLANES - (CONV_WIDTH - 1) + j
        acc = acc + cw_ref[j:j + 1, :] * ext_ref[start:start + rows, :]
    return _silu(acc)


def _ssd_kernel(z_ref, xbc_ref, misc_ref, cw_ref, cb_ref, dtb_ref, aneg_ref, dsk_ref, nw_ref,
                y_ref, ext_ref, halo_ref, st_ref, *, nchunk, bsz):
    first = pl.program_id(0) == 0
    length = SSD_CHUNK

    @pl.when(first)
    def _zero_state():
        st_ref[...] = jnp.zeros(st_ref.shape, F32)

    tri = (_lane_iota((length, length)) <= _row_iota((length, length))).astype(F32)
    lower = _lane_iota((length, length)) <= _row_iota((length, length))
    lane = _lane_iota((length, LANES))
    lo = lane < HALF
    bw = SSD_GROUPS * SSD_STATE
    npair = SSD_HEADS // 2
    for b in range(bsz):
        _ssd_rows(b, z_ref, xbc_ref, misc_ref, cw_ref, cb_ref, dtb_ref, aneg_ref, dsk_ref, nw_ref, y_ref,
                  ext_ref.at[b], halo_ref.at[b], st_ref, b * npair, first, nchunk, tri, lower, lo, bw)


def _ssd_rows(b, z_ref, xbc_ref, misc_ref, cw_ref, cb_ref, dtb_ref, aneg_ref, dsk_ref, nw_ref, y_ref,
              ext_ref, halo_ref, st_ref, st0, first, nchunk, tri, lower, lo, bw):
    length = SSD_CHUNK
    xbc = _causal_conv_silu(xbc_ref[b].astype(F32), ext_ref, halo_ref, cw_ref, cb_ref, first)
    dt_all = _softplus(misc_ref[b] + dtb_ref[...])
    a_all = dt_all * aneg_ref[...]

    for c in range(nchunk):
        r0 = c * length
        dt = dt_all[r0:r0 + length]
        acum = _dot_f32(tri, a_all[r0:r0 + length])
        acum_t = acum.T
        x_c = xbc[r0:r0 + length, :SSD_D_INNER]
        b_c = xbc[r0:r0 + length, SSD_D_INNER:SSD_D_INNER + bw]
        c_c = xbc[r0:r0 + length, SSD_D_INNER + bw:]
        y_parts = []
        for g in range(SSD_GROUPS):
            b_g = b_c[:, g * SSD_STATE:(g + 1) * SSD_STATE]
            c_g = c_c[:, g * SSD_STATE:(g + 1) * SSD_STATE].astype(BF16)
            b_gt = b_g.T.astype(BF16)
            cb = jnp.dot(c_g, b_gt, preferred_element_type=F32)
            pairs_per_group = SSD_HEADS // SSD_GROUPS // 2
            for pp in range(pairs_per_group):
                pr = g * pairs_per_group + pp
                h0, h1 = 2 * pr, 2 * pr + 1
                col0, col1 = acum[:, h0:h0 + 1], acum[:, h1:h1 + 1]
                l0 = jnp.exp(jnp.where(lower, col0 - acum_t[h0:h0 + 1, :], -jnp.inf))
                l1 = jnp.exp(jnp.where(lower, col1 - acum_t[h1:h1 + 1, :], -jnp.inf))
                mmat = jnp.concatenate([cb * l0, cb * l1], axis=1).astype(BF16)
                xp = x_c[:, pr * LANES:(pr + 1) * LANES]
                xdt = xp * jnp.where(lo, dt[:, h0:h0 + 1], dt[:, h1:h1 + 1])
                rhs = jnp.concatenate([jnp.where(lo, xdt, 0.0), jnp.where(lo, 0.0, xdt)], axis=0)
                y_diag = jnp.dot(mmat, rhs.astype(BF16), preferred_element_type=F32)
                col_pair = jnp.where(lo, col0, col1)
                last_pair = jnp.where(lo[:1], acum[length - 1:length, h0:h0 + 1],
                                      acum[length - 1:length, h1:h1 + 1])
                st = st_ref[st0 + pr]
                y_off = jnp.dot(c_g, st.astype(BF16), preferred_element_type=F32) * jnp.exp(col_pair)
                xdec = (xdt * jnp.exp(last_pair - col_pair)).astype(BF16)
                st_ref[st0 + pr] = st * jnp.exp(last_pair) + jnp.dot(b_gt, xdec, preferred_element_type=F32)
                y_parts.append(y_diag + y_off + xp * dsk_ref[:, pr * LANES:(pr + 1) * LANES])
        y = jnp.concatenate(y_parts, axis=1) * _silu(z_ref[b, r0:r0 + length, :].astype(F32))
        gw = SSD_D_INNER // SSD_GROUPS
        y = jnp.concatenate([_rms(y[:, g * gw:(g + 1) * gw]) for g in range(SSD_GROUPS)], axis=1)
        y_ref[b, r0:r0 + length, :] = (y * nw_ref[...]).astype(y_ref.dtype)


def _ssd_mixer(z, xbc, misc, conv_w, conv_b, dt_bias, a_log, d_skip, norm_w, bsz, seq):
    nchunk = min(SSD_STEP_CHUNKS, seq // SSD_CHUNK)
    rows = nchunk * SSD_CHUNK
    pad = jnp.zeros((LANES - SSD_HEADS,), F32)
    dtb = jnp.concatenate([dt_bias.astype(F32), pad]).reshape(1, LANES)
    aneg = jnp.concatenate([-jnp.exp(a_log.astype(F32)), pad]).reshape(1, LANES)
    dsk = jnp.repeat(d_skip.astype(F32), SSD_HEAD_DIM).reshape(1, SSD_D_INNER)
    full = lambda a: pl.BlockSpec(a.shape, lambda i: (0,) * a.ndim)
    cb2 = conv_b.reshape(1, -1)
    nw2 = norm_w.reshape(1, -1)
    blk = lambda w: pl.BlockSpec((bsz, rows, w), lambda i: (0, i, 0))
    out = pl.pallas_call(
        functools.partial(_ssd_kernel, nchunk=nchunk, bsz=bsz),
        grid=(seq // rows,),
        in_specs=[blk(SSD_D_INNER), blk(SSD_CONV_DIM), blk(LANES),
                  full(conv_w), full(cb2), full(dtb), full(aneg), full(dsk), full(nw2)],
        out_specs=blk(SSD_D_INNER),
        out_shape=jax.ShapeDtypeStruct((bsz, seq, SSD_D_INNER), BF16),
        scratch_shapes=[pltpu.VMEM((bsz, rows + SUBLANES, SSD_CONV_DIM), F32),
                        pltpu.VMEM((bsz, SUBLANES, SSD_CONV_DIM), F32),
                        pltpu.VMEM((bsz * (SSD_HEADS // 2), SSD_STATE, LANES), F32)],
        compiler_params=_cparams(("arbitrary",)),
        name="ssd_mixer",
    )(z.reshape(bsz, seq, -1), xbc.reshape(bsz, seq, -1), misc.reshape(bsz, seq, -1),
      conv_w, cb2, dtb, aneg, dsk, nw2)
    return out.reshape(bsz * seq, SSD_D_INNER)


def _half_sum(x, lo):
    s_lo = jnp.sum(jnp.where(lo, x, 0.0), axis=1, keepdims=True)
    s_hi = jnp.sum(jnp.where(lo, 0.0, x), axis=1, keepdims=True)
    return jnp.where(lo, s_lo, s_hi)


def _gdn_kernel(qkv_ref, z_ref, misc_ref, cw_ref, cb_ref, dtb_ref, aneg_ref, nw_ref,
                o_ref, ext_ref, halo_ref, st_ref, *, nchunk, bsz):
    first = pl.program_id(0) == 0
    length = GDN_CHUNK
    two = 2 * length

    @pl.when(first)
    def _zero_state():
        st_ref[...] = jnp.zeros(st_ref.shape, F32)

    tri = (_lane_iota((length, length)) <= _row_iota((length, length))).astype(F32)
    lane = _lane_iota((length, LANES))
    lo = lane < HALF
    r2 = _row_iota((two, two))
    c2 = _lane_iota((two, two))
    same_blk = (r2 < length) == (c2 < length)
    low_incl = same_blk & (c2 <= r2)
    low_strict = same_blk & (c2 < r2)
    eye = (r2 == c2).astype(F32)
    top_lo = (r2 < length) == (c2 < HALF)
    hk = GDN_HEADS * GDN_DK
    qscale = GDN_DK ** -0.5
    npair = GDN_HEADS // 2

    def blockdiag(slab):
        return jnp.where(top_lo, jnp.concatenate([slab, slab], axis=0), 0.0)

    def fold(bd):
        return bd[:length] + bd[length:]

    chains = []
    for b in range(bsz):
        qkv = _causal_conv_silu(qkv_ref[b].astype(F32), ext_ref.at[b], halo_ref.at[b], cw_ref, cb_ref, first)
        misc = misc_ref[b]
        beta_all = jax.nn.sigmoid(misc)
        g_all = aneg_ref[...] * _softplus(misc + dtb_ref[...])
        for c in range(nchunk):
            r0 = c * length
            gcum = _dot_f32(tri, g_all[r0:r0 + length])
            beta = beta_all[r0:r0 + length]
            for pr in range(npair):
                h0, h1 = 2 * pr, 2 * pr + 1
                q = qkv[r0:r0 + length, pr * LANES:(pr + 1) * LANES]
                k = qkv[r0:r0 + length, hk + pr * LANES:hk + (pr + 1) * LANES]
                v = qkv[r0:r0 + length, 2 * hk + pr * LANES:2 * hk + (pr + 1) * LANES]
                qn = q * lax.rsqrt(_half_sum(q * q, lo) + EPS) * qscale
                kn = k * lax.rsqrt(_half_sum(k * k, lo) + EPS)
                g0 = gcum[:, MISC_A + h0:MISC_A + h0 + 1]
                g1 = gcum[:, MISC_A + h1:MISC_A + h1 + 1]
                gexp = jnp.where(lo, g0, g1)
                bexp = jnp.where(lo, beta[:, MISC_B + h0:MISC_B + h0 + 1],
                                 beta[:, MISC_B + h1:MISC_B + h1 + 1])
                gcol = jnp.concatenate([jnp.broadcast_to(g0, (length, two)),
                                        jnp.broadcast_to(g1, (length, two))], axis=0)
                decay = jnp.exp(jnp.where(low_incl, gcol - gcol.T, -jnp.inf))
                kb = kn * bexp
                kk2 = jnp.concatenate([kn, kn], axis=0)
                kq = _dot_nt(jnp.concatenate([blockdiag(kb), blockdiag(qn)], axis=0), kk2)
                a_mat = jnp.where(low_strict, kq[:two] * decay, 0.0)
                qk = jnp.where(low_incl, kq[two:] * decay, 0.0)
                g_last = gexp[length - 1:length, :]
                chains.append(dict(
                    b=b, c=c, pr=pr, pw=a_mat, t=eye - a_mat, qk=qk,
                    vb=v * bexp, kbg=kb * jnp.exp(gexp), q_dec=qn * jnp.exp(gexp),
                    k_tail=kn * jnp.exp(g_last - gexp), gl=jnp.exp(g_last)))

    for _ in range(int(math.log2(length)) - 1):
        for ch in chains:
            ch["pw"] = _dot(ch["pw"], ch["pw"])
        for ch in chains:
            ch["t"] = ch["t"] + _dot(ch["t"], ch["pw"])

    for ch in chains:
        ku = _dot(ch["t"], jnp.concatenate([blockdiag(ch["kbg"]), blockdiag(ch["vb"])], axis=1))
        ch["k_cum"] = fold(ku[:, :LANES])
        ch["u"] = fold(ku[:, LANES:])
    for ch in chains:
        rhs = jnp.concatenate([ch["k_cum"], ch["u"]], axis=1)
        mn = _dot_tn(ch["k_tail"], rhs)
        ch["m"] = jnp.where(top_lo, mn[:, :LANES], 0.0)
        ch["n"] = jnp.where(top_lo, mn[:, LANES:], 0.0)
        qo = _dot(ch["qk"], jnp.concatenate([blockdiag(ch["k_cum"]), blockdiag(ch["u"])], axis=1))
        ch["q_eff"] = ch["q_dec"] - fold(qo[:, :LANES])
        ch["o_loc"] = fold(qo[:, LANES:])

    states = {(b, pr): st_ref[b * npair + pr] for b in range(bsz) for pr in range(npair)}
    for c in range(nchunk):
        for ch in chains:
            if ch["c"] != c:
                continue
            b, pr = ch["b"], ch["pr"]
            st = states[(b, pr)]
            both = _dot(jnp.concatenate([ch["q_eff"], ch["m"]], axis=0), st)
            o = both[:length] + ch["o_loc"]
            states[(b, pr)] = st * ch["gl"] - both[length:] + ch["n"]
            r0 = c * length
            ms = _half_sum(o * o, lo) * (1.0 / GDN_DV)
            zz = z_ref[b, r0:r0 + length, pr * LANES:(pr + 1) * LANES].astype(F32)
            out = o * lax.rsqrt(ms + EPS) * nw_ref[...] * _silu(zz)
            o_ref[b, r0:r0 + length, pr * LANES:(pr + 1) * LANES] = out.astype(o_ref.dtype)
    for (b, pr), st in states.items():
        st_ref[b * npair + pr] = st


def _gdn_mixer(qkv, z, misc, conv_w, conv_b, dt_bias, a_log, norm_w, bsz, seq):
    nchunk = min(GDN_STEP_CHUNKS, seq // GDN_CHUNK)
    rows = nchunk * GDN_CHUNK
    dtb = jnp.zeros((1, LANES), F32).at[0, MISC_A:MISC_A + GDN_HEADS].set(dt_bias.astype(F32))
    aneg = jnp.zeros((1, LANES), F32).at[0, MISC_A:MISC_A + GDN_HEADS].set(-jnp.exp(a_log.astype(F32)))
    nw2 = jnp.concatenate([norm_w.astype(F32)] * 2).reshape(1, LANES)
    cb2 = conv_b.reshape(1, -1)
    full = lambda a: pl.BlockSpec(a.shape, lambda i: (0,) * a.ndim)
    blk = lambda w: pl.BlockSpec((bsz, rows, w), lambda i: (0, i, 0))
    out = pl.pallas_call(
        functools.partial(_gdn_kernel, nchunk=nchunk, bsz=bsz),
        grid=(seq // rows,),
        in_specs=[blk(GDN_QKV), blk(GDN_OUT), blk(LANES),
                  full(conv_w), full(cb2), full(dtb), full(aneg), full(nw2)],
        out_specs=blk(GDN_OUT),
        out_shape=jax.ShapeDtypeStruct((bsz, seq, GDN_OUT), BF16),
        scratch_shapes=[pltpu.VMEM((bsz, rows + SUBLANES, GDN_QKV), F32),
                        pltpu.VMEM((bsz, SUBLANES, GDN_QKV), F32),
                        pltpu.VMEM((bsz * (GDN_HEADS // 2), 2 * GDN_DK, LANES), F32)],
        compiler_params=_cparams(("arbitrary",)),
        name="gdn_mixer",
    )(qkv.reshape(bsz, seq, -1), z.reshape(bsz, seq, -1), misc.reshape(bsz, seq, -1),
      conv_w, cb2, dtb, aneg, nw2)
    return out.reshape(bsz * seq, GDN_OUT)


ROUTE_E0, ROUTE_E1, ROUTE_R0, ROUTE_R1, ROUTE_W0, ROUTE_W1 = range(6)


def _post_kernel(*refs, route):
    if route:
        (x_ref, a_ref, s_ref, g_ref, wa_ref, ws_ref, wg_ref, gate_ref, sh_ref, sc_ref, wrh_ref, wrl_ref,
         x1_ref, h_ref, route_ref, cnt_ref, carry_ref) = refs
    else:
        (x_ref, a_ref, s_ref, g_ref, wa_ref, ws_ref, wg_ref, gate_ref, sh_ref, sc_ref,
         x1_ref, h_ref) = refs
    mix = (lax.dot_general(a_ref[0], wa_ref[...], (((0,), (0,)), ((), ())), preferred_element_type=F32)
           + jnp.dot(s_ref[...], ws_ref[...], preferred_element_type=F32)
           + jnp.dot(g_ref[...], wg_ref[...], preferred_element_type=F32))
    x1 = x_ref[...] + gate_ref[0] * mix
    x1_ref[...] = x1
    h = _rms(x1) * (1.0 + sc_ref[0]) + sh_ref[0]
    h_ref[...] = h.astype(h_ref.dtype)
    if not route:
        return

    @pl.when(pl.program_id(0) == 0)
    def _zero_carry():
        carry_ref[...] = jnp.zeros(carry_ref.shape, F32)

    tm = h.shape[0]
    lane = _lane_iota((tm, LANES))
    h_hi = h.astype(BF16)
    h_lo = (h - h_hi.astype(F32)).astype(BF16)
    hh = jnp.dot(h_hi, jnp.concatenate([wrh_ref[...], wrl_ref[...]], axis=1), preferred_element_type=F32)
    logits = hh[:, :LANES] + (jnp.dot(h_lo, wrh_ref[...], preferred_element_type=F32) + hh[:, LANES:])
    logits = jnp.where(lane < N_EXPERTS, logits, -jnp.inf)
    lane_f = lane.astype(F32)
    m0 = jnp.max(logits, axis=1, keepdims=True)
    e0 = jnp.min(jnp.where(logits == m0, lane_f, float(LANES)), axis=1, keepdims=True)
    rest = jnp.where(lane_f == e0, -jnp.inf, logits)
    m1 = jnp.max(rest, axis=1, keepdims=True)
    e1 = jnp.min(jnp.where(rest == m1, lane_f, float(LANES)), axis=1, keepdims=True)
    ex = jnp.exp(m1 - m0)
    w0 = 1.0 / (1.0 + ex)
    w1 = ex / (1.0 + ex)
    oh0 = (lane_f == e0).astype(F32)
    oh1 = (lane_f == e1).astype(F32)
    both = oh0 + oh1
    strict = (_lane_iota((tm, tm)) < _row_iota((tm, tm))).astype(BF16)
    before = jnp.dot(strict, both.astype(BF16), preferred_element_type=F32) + carry_ref[0:1, :]
    r0 = jnp.sum(before * oh0, axis=1, keepdims=True)
    r1 = jnp.sum(before * oh1, axis=1, keepdims=True)
    carry_ref[0:1, :] = carry_ref[0:1, :] + jnp.sum(both, axis=0, keepdims=True)
    slab = jnp.zeros((tm, LANES), F32)
    for pos, val in ((ROUTE_E0, e0), (ROUTE_E1, e1), (ROUTE_R0, r0), (ROUTE_R1, r1),
                     (ROUTE_W0, w0), (ROUTE_W1, w1)):
        slab = jnp.where(lane == pos, val, slab)
    route_ref[...] = slab
    cnt_ref[...] = carry_ref[...]


def _post_mixer(x2d, mla_o, ssd_y, gdn_o, w_out, gate, shift, scale, seq, w_router=None):
    t, d = x2d.shape
    route = w_router is not None
    tm = min(ROUTE_TILE if route else TM_PROJ, seq)
    per_b = seq // tm
    wa = w_out[:MLA_OUT].astype(BF16)
    ws = w_out[MLA_OUT:MLA_OUT + SSD_D_INNER].astype(BF16)
    wg = w_out[MLA_OUT + SSD_D_INNER:].astype(BF16)
    row = lambda w: pl.BlockSpec((tm, w), lambda i: (i, 0))
    full = lambda a: pl.BlockSpec(a.shape, lambda i: (0,) * a.ndim)
    mod = pl.BlockSpec((1, 1, d), lambda i: (i // per_b, 0, 0))
    att = pl.BlockSpec((1, MLA_OUT, tm), lambda i: (i // per_b, 0, i % per_b))
    in_specs = [row(d), att, row(SSD_D_INNER), row(GDN_OUT), full(wa), full(ws), full(wg),
                mod, mod, mod]
    args = [x2d, mla_o, ssd_y, gdn_o, wa, ws, wg, gate, shift, scale]
    out_specs = [row(d), row(d)]
    out_shape = [jax.ShapeDtypeStruct((t, d), F32), jax.ShapeDtypeStruct((t, d), F32 if route else BF16)]
    scratch = []
    if route:
        wr = jnp.zeros((d, LANES), F32).at[:, :N_EXPERTS].set(w_router.astype(F32))
        wr_hi = wr.astype(BF16)
        wr_lo = (wr - wr_hi.astype(F32)).astype(BF16)
        in_specs += [full(wr_hi), full(wr_lo)]
        args += [wr_hi, wr_lo]
        out_specs += [row(LANES), pl.BlockSpec((SUBLANES, LANES), lambda i: (0, 0))]
        out_shape += [jax.ShapeDtypeStruct((t, LANES), F32), jax.ShapeDtypeStruct((SUBLANES, LANES), F32)]
        scratch = [pltpu.VMEM((SUBLANES, LANES), F32)]
    return pl.pallas_call(
        functools.partial(_post_kernel, route=route),
        grid=(t // tm,),
        in_specs=in_specs, out_specs=out_specs, out_shape=out_shape, scratch_shapes=scratch,
        compiler_params=_cparams(("arbitrary",) if route else ("parallel",)),
        name="post_mixer_route" if route else "post_mixer",
    )(*args)


def _finish(x, final, fn_ref):
    return _rms(x) * fn_ref[...] if final else x


def _ffn_kernel(h_ref, x1_ref, gate_ref, fn_ref, wg_ref, wu_ref, wd_ref, o_ref, *, final):
    f = pl.program_id(1)
    h = h_ref[...]
    act = _silu(jnp.dot(h, wg_ref[...], preferred_element_type=F32)) * jnp.dot(
        h, wu_ref[...], preferred_element_type=F32)
    part = jnp.dot(act.astype(BF16), wd_ref[...], preferred_element_type=F32)

    @pl.when(f == 0)
    def _set():
        o_ref[...] = part

    @pl.when(f > 0)
    def _add():
        o_ref[...] += part

    @pl.when(f == pl.num_programs(1) - 1)
    def _residual():
        o_ref[...] = _finish(x1_ref[...] + gate_ref[0] * o_ref[...], final, fn_ref)


def _ffn_tile(dff):
    for cand in (2816, 1408, 1024, 512, 256, 128):
        if dff % cand == 0:
            return cand
    raise ValueError(f"unsupported d_ff {dff}")


def _dense_ffn(h, x1, gate, w_gate, w_up, w_down, final_norm, final, seq):
    t, d = x1.shape
    dff = w_gate.shape[1]
    tm = min(TM_FFN, seq)
    tf = _ffn_tile(dff)
    per_b = seq // tm
    fn = final_norm.reshape(1, d).astype(F32)
    return pl.pallas_call(
        functools.partial(_ffn_kernel, final=final),
        grid=(t // tm, dff // tf),
        in_specs=[
            pl.BlockSpec((tm, d), lambda i, f: (i, 0)),
            pl.BlockSpec((tm, d), lambda i, f: (i, 0)),
            pl.BlockSpec((1, 1, d), lambda i, f: (i // per_b, 0, 0)),
            pl.BlockSpec((1, d), lambda i, f: (0, 0)),
            pl.BlockSpec((d, tf), lambda i, f: (0, f)),
            pl.BlockSpec((d, tf), lambda i, f: (0, f)),
            pl.BlockSpec((tf, d), lambda i, f: (f, 0)),
        ],
        out_specs=pl.BlockSpec((tm, d), lambda i, f: (i, 0)),
        out_shape=jax.ShapeDtypeStruct((t, d), F32),
        compiler_params=_cparams(("parallel", "arbitrary")),
        name="dense_ffn",
    )(h, x1, gate, fn, w_gate.astype(BF16), w_up.astype(BF16), w_down.astype(BF16))


def _row_copy(src, dst, sem):
    return pltpu.make_async_copy(src, dst, sem)


def _dispatch_kernel(fill_ref, dest_ref, h_ref, xs_hbm, zbuf_ref, sem, zsem, *, tm, bm):
    @pl.when(pl.program_id(0) == 0)
    def _zero_padding():
        zbuf_ref[...] = jnp.zeros(zbuf_ref.shape, F32)

        def row_fill(r):
            return _row_copy(zbuf_ref.at[pl.ds(0, 1), :], xs_hbm.at[pl.ds(r, 1), :], zsem)

        def block_fill(blk):
            return _row_copy(zbuf_ref, xs_hbm.at[pl.ds(pl.multiple_of(blk * bm, bm), bm), :], zsem)

        spans = [(fill_ref[2 * e], fill_ref[2 * e + 1], row_fill) for e in range(N_EXPERTS)]
        spans.append((fill_ref[2 * N_EXPERTS], fill_ref[2 * N_EXPERTS + 1], block_fill))
        for lo, hi, fill in spans:
            lax.fori_loop(lo, hi, lambda r, c, fill=fill: (fill(r).start(), c)[1], 0)
        for lo, hi, fill in spans:
            lax.fori_loop(lo, hi, lambda r, c, fill=fill: (fill(r).wait(), c)[1], 0)

    def issue(g, carry):
        for j in range(ROW_DMA_UNROLL):
            t = g * ROW_DMA_UNROLL + j
            for kk in range(TOP_K):
                d = dest_ref[0, 0, TOP_K * t + kk]
                _row_copy(h_ref.at[pl.ds(t, 1), :], xs_hbm.at[pl.ds(d, 1), :], sem).start(priority=kk)
        return carry

    lax.fori_loop(0, tm // ROW_DMA_UNROLL, issue, 0)

    def drain(g, carry):
        for _ in range(ROW_DMA_UNROLL * TOP_K):
            _row_copy(h_ref.at[pl.ds(0, 1), :], xs_hbm.at[pl.ds(0, 1), :], sem).wait()
        return carry

    lax.fori_loop(0, tm // ROW_DMA_UNROLL, drain, 0)


def _moe_dispatch(h_f32, dest, fill, n_rows):
    t, d = h_f32.shape
    tm = min(MOE_TOK_TILE, t)
    bm = MOE_ROWS
    dest3 = dest.reshape(t // tm, 1, TOP_K * tm)
    return pl.pallas_call(
        functools.partial(_dispatch_kernel, tm=tm, bm=bm),
        grid=(t // tm,),
        in_specs=[
            pl.BlockSpec(memory_space=pltpu.SMEM),
            pl.BlockSpec((1, 1, TOP_K * tm), lambda i: (i, 0, 0), memory_space=pltpu.SMEM),
            pl.BlockSpec((tm, d), lambda i: (i, 0)),
        ],
        out_specs=pl.BlockSpec(memory_space=pl.ANY),
        out_shape=jax.ShapeDtypeStruct((n_rows, d), F32),
        scratch_shapes=[pltpu.VMEM((bm, d), F32), pltpu.SemaphoreType.DMA(()), pltpu.SemaphoreType.DMA(())],
        compiler_params=_cparams(("arbitrary",)),
        name="moe_dispatch",
    )(fill, dest3, h_f32)


def _experts_kernel(be_ref, nused_ref, x_ref, wg_ref, wu_ref, wd_ref, y_ref):
    i = pl.program_id(0)
    f = pl.program_id(1)

    used = i < nused_ref[0]

    @pl.when(jnp.logical_not(used) & (f == 0))
    def _unused_block():
        y_ref[...] = jnp.zeros(y_ref.shape, F32)

    def partial_out():
        xb = x_ref[...].astype(BF16)
        act = _silu(jnp.dot(xb, wg_ref[0], preferred_element_type=F32)) * jnp.dot(
            xb, wu_ref[0], preferred_element_type=F32)
        return jnp.dot(act.astype(BF16), wd_ref[0], preferred_element_type=F32)

    @pl.when(used & (f == 0))
    def _first_ff_tile():
        y_ref[...] = partial_out()

    @pl.when(used & (f > 0))
    def _next_ff_tile():
        y_ref[...] += partial_out()


def _moe_experts(xs, block_expert, n_used, w_gate, w_up, w_down):
    n_rows, d = xs.shape
    dff = w_gate.shape[2]
    bm = MOE_ROWS
    tf = MOE_FF_TILE if dff % MOE_FF_TILE == 0 else dff

    def ff(i, f, nu):
        return jnp.where(i < nu[0], f, 0)

    return pl.pallas_call(
        _experts_kernel,
        grid_spec=pltpu.PrefetchScalarGridSpec(
            num_scalar_prefetch=2,
            grid=(n_rows // bm, dff // tf),
            in_specs=[
                pl.BlockSpec((bm, d), lambda i, f, be, nu: (i, 0)),
                pl.BlockSpec((1, d, tf), lambda i, f, be, nu: (be[i], 0, ff(i, f, nu))),
                pl.BlockSpec((1, d, tf), lambda i, f, be, nu: (be[i], 0, ff(i, f, nu))),
                pl.BlockSpec((1, tf, d), lambda i, f, be, nu: (be[i], ff(i, f, nu), 0)),
            ],
            out_specs=pl.BlockSpec((bm, d), lambda i, f, be, nu: (i, 0)),
        ),
        out_shape=jax.ShapeDtypeStruct((n_rows, d), F32),
        compiler_params=_cparams(("parallel", "arbitrary")),
        name="moe_experts",
    )(block_expert, n_used, xs, w_gate.astype(BF16), w_up.astype(BF16), w_down.astype(BF16))


def _combine_kernel(dest_ref, dnext_ref, x1_ref, route_ref, gate_ref, fn_ref, ys_hbm, o_ref, buf_ref, sem,
                    *, tm, final):
    i = pl.program_id(0)
    slot = lax.rem(i, 2)

    def gather(d_ref, s):
        def issue(g, carry):
            for j in range(ROW_DMA_UNROLL):
                t = g * ROW_DMA_UNROLL + j
                for kk in range(TOP_K):
                    d = d_ref[0, 0, TOP_K * t + kk]
                    _row_copy(ys_hbm.at[pl.ds(d, 1), :], buf_ref.at[s, kk, pl.ds(t, 1), :],
                              sem.at[s]).start(priority=kk)
            return carry

        lax.fori_loop(0, tm // ROW_DMA_UNROLL, issue, 0)

    @pl.when(i == 0)
    def _first_tile():
        gather(dest_ref, slot)

    @pl.when(i + 1 < pl.num_programs(0))
    def _next_tile():
        gather(dnext_ref, 1 - slot)

    def drain(g, carry):
        for _ in range(ROW_DMA_UNROLL * TOP_K):
            _row_copy(ys_hbm.at[pl.ds(0, 1), :], buf_ref.at[slot, 0, pl.ds(0, 1), :], sem.at[slot]).wait()
        return carry

    lax.fori_loop(0, tm // ROW_DMA_UNROLL, drain, 0)
    route = route_ref[...]
    w0 = route[:, ROUTE_W0:ROUTE_W0 + 1]
    w1 = route[:, ROUTE_W1:ROUTE_W1 + 1]
    f = w0 * buf_ref[slot, 0] + w1 * buf_ref[slot, 1]
    o_ref[...] = _finish(x1_ref[...] + gate_ref[0] * f, final, fn_ref)


def _moe_combine(ys, dest, x1, route, gate, final_norm, final, seq):
    t, d = x1.shape
    tm = min(MOE_TOK_TILE, seq)
    per_b = seq // tm
    dest3 = dest.reshape(t // tm, 1, TOP_K * tm)
    fn = final_norm.reshape(1, d).astype(F32)
    nt = t // tm
    return pl.pallas_call(
        functools.partial(_combine_kernel, tm=tm, final=final),
        grid=(nt,),
        in_specs=[
            pl.BlockSpec((1, 1, TOP_K * tm), lambda i: (i, 0, 0), memory_space=pltpu.SMEM),
            pl.BlockSpec((1, 1, TOP_K * tm), lambda i: (jnp.minimum(i + 1, nt - 1), 0, 0),
                         memory_space=pltpu.SMEM),
            pl.BlockSpec((tm, d), lambda i: (i, 0)),
            pl.BlockSpec((tm, LANES), lambda i: (i, 0)),
            pl.BlockSpec((1, 1, d), lambda i: (i // per_b, 0, 0)),
            pl.BlockSpec((1, d), lambda i: (0, 0)),
            pl.BlockSpec(memory_space=pl.ANY),
        ],
        out_specs=pl.BlockSpec((tm, d), lambda i: (i, 0)),
        out_shape=jax.ShapeDtypeStruct((t, d), F32),
        scratch_shapes=[pltpu.VMEM((2, TOP_K, tm, d), F32), pltpu.SemaphoreType.DMA((2,))],
        compiler_params=_cparams(("arbitrary",)),
        name="moe_combine",
    )(dest3, dest3, x1, route, gate, fn, ys)


def _moe_ffn(h_f32, x1, route, counts_slab, gate, w_gate, w_up, w_down, final_norm, final, seq):
    t, d = x1.shape
    bm = MOE_ROWS
    n_blocks = (t * TOP_K + bm - 1) // bm + N_EXPERTS
    n_rows = n_blocks * bm
    counts = counts_slab[0, :N_EXPERTS].astype(I32)
    padded = ((counts + bm - 1) // bm) * bm
    pend = jnp.cumsum(padded)
    pstart = pend - padded
    e = route[:, ROUTE_E0:ROUTE_E1 + 1].astype(I32)
    rank = route[:, ROUTE_R0:ROUTE_R1 + 1].astype(I32)
    dest = (jnp.sum(jnp.where(e[..., None] == jnp.arange(N_EXPERTS), pstart, 0), axis=-1) + rank)
    blk_start = jnp.arange(n_blocks, dtype=I32) * bm
    block_expert = jnp.minimum(jnp.sum(blk_start[:, None] >= pend[None, :], axis=1), N_EXPERTS - 1).astype(I32)
    n_used = (pend[-1:] // bm).astype(I32)
    fill = jnp.concatenate([jnp.stack([pstart + counts, pend], axis=1).reshape(-1),
                            n_used, jnp.full((1,), n_blocks, I32)]).astype(I32)
    xs = _moe_dispatch(h_f32, dest, fill, n_rows)
    ys = _moe_experts(xs, block_expert, n_used, w_gate, w_up, w_down)
    return _moe_combine(ys, dest, x1, route, gate, final_norm, final, seq)


def kernel(x, c, ada_w, ada_b, w_in, mla_q_norm, mla_w_uq, mla_kv_norm, mla_w_ukv, ssd_conv_w, ssd_conv_b, ssd_dt_bias, ssd_a_log, ssd_d, ssd_norm, gdn_conv_w, gdn_conv_b, gdn_dt_bias, gdn_a_log, gdn_norm, w_out, ffn_w_gate, ffn_w_up, ffn_w_down, moe_router, moe_w_gate, moe_w_up, moe_w_down, final_norm):
    bsz, seq, d = x.shape
    depth = w_in.shape[0]
    t = bsz * seq
    mod = _ada_modulation(c, ada_w, ada_b)
    tables = _rope_tables(seq)
    xf = x.reshape(t, d)
    for l in range(depth):
        sh1, sc1, g1, sh2, sc2, g2 = [m.reshape(bsz, 1, d) for m in jnp.split(mod[l], 6, axis=-1)]
        mla_c, misc, ssd_z, ssd_xbc, gdn_qkv, gdn_z = _in_proj(xf, sh1, sc1, _prep_w_in(w_in[l]), seq)
        wqa, wqb, wkv = _prep_mla_weights(mla_w_uq[l], mla_w_ukv[l])
        qt, k, vt = _mla_proj(mla_c, misc, tables, mla_q_norm[l], mla_kv_norm[l], wqa, wqb, wkv, bsz, seq)
        mla_o = _flash_attention(qt, k, vt, bsz, seq)
        ssd_y = _ssd_mixer(ssd_z, ssd_xbc, misc, ssd_conv_w[l], ssd_conv_b[l], ssd_dt_bias[l],
                           ssd_a_log[l], ssd_d[l], ssd_norm[l], bsz, seq)
        gdn_o = _gdn_mixer(gdn_qkv, gdn_z, misc, gdn_conv_w[l], gdn_conv_b[l], gdn_dt_bias[l],
                           gdn_a_log[l], gdn_norm[l], bsz, seq)
        final = l == depth - 1
        if l % 2 == 0:
            x1, h2 = _post_mixer(xf, mla_o, ssd_y, gdn_o, w_out[l], g1, sh2, sc2, seq)
            xf = _dense_ffn(h2, x1, g2, ffn_w_gate[l // 2], ffn_w_up[l // 2], ffn_w_down[l // 2],
                            final_norm, final, seq)
        else:
            x1, h2, route, counts = _post_mixer(xf, mla_o, ssd_y, gdn_o, w_out[l], g1, sh2, sc2, seq,
                                                w_router=moe_router[l // 2])
            xf = _moe_ffn(h2, x1, route, counts, g2, moe_w_gate[l // 2], moe_w_up[l // 2],
                          moe_w_down[l // 2], final_norm, final, seq)
    return xf.reshape(bsz, seq, d)
```

```python
import functools
import math

import jax
import jax.numpy as jnp
import numpy as np
from jax import lax
from jax.experimental import pallas as pl
from jax.experimental.pallas import tpu as pltpu

F32 = jnp.float32
BF16 = jnp.bfloat16
I32 = jnp.int32
HIGHEST = lax.Precision.HIGHEST

D_MODEL = 1024
EPS = 1e-6
CONV_WIDTH = 4
MLA_HEADS, MLA_Q_RANK, MLA_KV_RANK = 4, 256, 128
MLA_NOPE, MLA_ROPE, MLA_V = 64, 32, 64
ROPE_THETA = 10000.0
SSD_HEADS, SSD_HEAD_DIM, SSD_GROUPS, SSD_STATE, SSD_CHUNK = 8, 64, 2, 64, 128
SSD_D_INNER = SSD_HEADS * SSD_HEAD_DIM
SSD_CONV_DIM = SSD_D_INNER + 2 * SSD_GROUPS * SSD_STATE
GDN_HEADS, GDN_DK, GDN_DV, GDN_CHUNK = 4, 64, 64, 64
GDN_QKV = GDN_HEADS * (2 * GDN_DK + GDN_DV)
GDN_OUT = GDN_HEADS * GDN_DV
MLA_IN = MLA_Q_RANK + MLA_KV_RANK + MLA_ROPE
MLA_OUT = MLA_HEADS * MLA_V
SSD_IN = SSD_D_INNER + SSD_CONV_DIM + SSD_HEADS
GDN_IN = GDN_QKV + GDN_HEADS * GDN_DV + 2 * GDN_HEADS
N_EXPERTS, TOP_K = 8, 2

LANES = 128
SUBLANES = 8
HALF = LANES // 2
VMEM_LIMIT_BYTES = 56 * 1024 * 1024

MISC_DT = 0
MISC_B = 8
MISC_A = 12
MISC_KR = 64
MISC_KR_SW = 96
HEAD_PAD = 128

TM_PROJ = 1024
IN_PROJ_SUB = 512
TM_FFN = 512
ATT_Q_BLOCK = 2048
ATT_K_BLOCK = 2048
ATT_Q_COLS = 256
ATT_K_ROWS = 512
ATT_PIPELINE = 4
SSD_STEP_CHUNKS = 2
GDN_STEP_CHUNKS = 8
MOE_ROWS = 512
MOE_FF_TILE = 1792
MOE_TOK_TILE = 512
ROUTE_TILE = 512
ROW_DMA_UNROLL = 8


def _cparams(sem):
    return pltpu.CompilerParams(dimension_semantics=sem, vmem_limit_bytes=VMEM_LIMIT_BYTES)


def _lane_iota(shape):
    return lax.broadcasted_iota(I32, shape, len(shape) - 1)


def _row_iota(shape):
    return lax.broadcasted_iota(I32, shape, len(shape) - 2)


def _softplus(x):
    return jnp.maximum(x, 0.0) + jnp.log1p(jnp.exp(-jnp.abs(x)))


def _silu(x):
    return x * jax.nn.sigmoid(x)


def _rms(x):
    return x * lax.rsqrt(jnp.mean(x * x, axis=-1, keepdims=True) + EPS)


def _dot(a, b):
    return jnp.dot(a.astype(BF16), b.astype(BF16), preferred_element_type=F32)


def _dot_nt(a, b):
    return lax.dot_general(a.astype(BF16), b.astype(BF16), (((1,), (1,)), ((), ())),
                           preferred_element_type=F32)


def _dot_tn(a, b):
    return lax.dot_general(a.astype(BF16), b.astype(BF16), (((0,), (0,)), ((), ())),
                           preferred_element_type=F32)


def _dot_f32(a, b):
    return jnp.dot(a, b, precision=HIGHEST, preferred_element_type=F32)


def _ada_kernel(c_ref, w_ref, b_ref, o_ref):
    c_act = _silu(c_ref[...])
    o_ref[0] = _dot_f32(c_act, w_ref[0]) + b_ref[0]


def _ada_modulation(c, ada_w, ada_b):
    depth, d, n = ada_w.shape
    bsz = c.shape[0]
    rows = max(SUBLANES, bsz)
    c_pad = jnp.zeros((rows, d), F32).at[:bsz].set(c)
    tn = 1536
    out = pl.pallas_call(
        _ada_kernel,
        grid=(depth, n // tn),
        in_specs=[
            pl.BlockSpec((rows, d), lambda l, j: (0, 0)),
            pl.BlockSpec((1, d, tn), lambda l, j: (l, 0, j)),
            pl.BlockSpec((1, 1, tn), lambda l, j: (l, 0, j)),
        ],
        out_specs=pl.BlockSpec((1, rows, tn), lambda l, j: (l, 0, j)),
        out_shape=jax.ShapeDtypeStruct((depth, rows, n), F32),
        compiler_params=_cparams(("parallel", "parallel")),
        name="ada_modulation",
    )(c_pad, ada_w, ada_b.reshape(depth, 1, n))
    return out[:, :bsz]


IN_SLABS = (("mla_c", 384, F32), ("misc", 128, F32), ("ssd_z", 512, BF16), ("ssd_xbc", 768, BF16),
            ("gdn_qkv", 768, BF16), ("gdn_z", 256, BF16))
IN_WIDTH = sum(w for _, w, _ in IN_SLABS)


def _prep_w_in(w):
    d = w.shape[0]
    o_ssd = MLA_IN
    o_gdn = MLA_IN + SSD_IN
    w_kr = w[:, MLA_Q_RANK + MLA_KV_RANK:MLA_IN]
    half = MLA_ROPE // 2
    w_kr_sw = jnp.concatenate([-w_kr[:, half:], w_kr[:, :half]], axis=1)
    ssd_dt = w[:, o_ssd + SSD_D_INNER + SSD_CONV_DIM:o_ssd + SSD_IN]
    gdn_ba = w[:, o_gdn + GDN_QKV + GDN_OUT:o_gdn + GDN_IN]
    misc = jnp.concatenate(
        [ssd_dt, gdn_ba, jnp.zeros((d, MISC_KR - MISC_A - GDN_HEADS), w.dtype), w_kr, w_kr_sw], axis=1)
    cols = [
        w[:, :MLA_Q_RANK + MLA_KV_RANK], misc,
        w[:, o_ssd:o_ssd + SSD_D_INNER],
        w[:, o_ssd + SSD_D_INNER:o_ssd + SSD_D_INNER + SSD_CONV_DIM],
        w[:, o_gdn:o_gdn + GDN_QKV],
        w[:, o_gdn + GDN_QKV:o_gdn + GDN_QKV + GDN_OUT],
    ]
    return jnp.concatenate(cols, axis=1).astype(BF16)


def _in_proj_kernel(x_ref, sh_ref, sc_ref, w_ref, *o_refs):
    rows = x_ref.shape[0]
    sub = min(IN_PROJ_SUB, rows)

    def normed(r0):
        return (_rms(x_ref[r0:r0 + sub, :]) * (1.0 + sc_ref[0]) + sh_ref[0]).astype(BF16)

    nxt = normed(0)
    for r0 in range(0, rows, sub):
        hb = nxt
        if r0 + sub < rows:
            nxt = normed(r0 + sub)
        off = 0
        for o_ref, (_, width, _) in zip(o_refs, IN_SLABS):
            o_ref[r0:r0 + sub, :] = jnp.dot(
                hb, w_ref[:, off:off + width], preferred_element_type=F32).astype(o_ref.dtype)
            off += width


def _in_proj(x2d, shift, scale, w_perm, seq):
    t, d = x2d.shape
    tm = min(TM_PROJ, seq)
    per_b = seq // tm
    return pl.pallas_call(
        _in_proj_kernel,
        grid=(t // tm,),
        in_specs=[
            pl.BlockSpec((tm, d), lambda i: (i, 0)),
            pl.BlockSpec((1, 1, d), lambda i: (i // per_b, 0, 0)),
            pl.BlockSpec((1, 1, d), lambda i: (i // per_b, 0, 0)),
            pl.BlockSpec((d, IN_WIDTH), lambda i: (0, 0)),
        ],
        out_specs=[pl.BlockSpec((tm, w), lambda i: (i, 0)) for _, w, _ in IN_SLABS],
        out_shape=[jax.ShapeDtypeStruct((t, w), dt) for _, w, dt in IN_SLABS],
        compiler_params=_cparams(("parallel",)),
        name="in_proj",
    )(x2d, shift, scale, w_perm)


def _rope_tables(seq):
    pos = jnp.arange(seq, dtype=F32)
    inv_freq = ROPE_THETA ** (-jnp.arange(0, MLA_ROPE, 2, dtype=F32) / MLA_ROPE)
    ang = pos[:, None] * inv_freq[None, :]
    cos, sin = jnp.cos(ang), jnp.sin(ang)
    zeros_l = jnp.zeros((seq, MLA_NOPE), F32)
    zeros_r = jnp.zeros((seq, HEAD_PAD - MLA_NOPE - MLA_ROPE), F32)
    cos_k = jnp.concatenate([zeros_l, cos, cos, zeros_r], axis=1)
    sin_k = jnp.concatenate([zeros_l, sin, sin, zeros_r], axis=1)
    scale = (MLA_NOPE + MLA_ROPE) ** -0.5 * math.log2(math.e)
    cos_q = scale * jnp.concatenate([jnp.ones((seq, MLA_NOPE), F32), cos, cos, zeros_r], axis=1)
    sin_q = scale * sin_k
    return cos_q, sin_q, cos_k, sin_k


def _prep_mla_weights(w_uq, w_ukv):
    r = w_uq.shape[0]
    hq = MLA_NOPE + MLA_ROPE
    half = MLA_ROPE // 2
    qa, qb = [], []
    for h in range(MLA_HEADS):
        nope = w_uq[:, h * hq:h * hq + MLA_NOPE]
        rope = w_uq[:, h * hq + MLA_NOPE:(h + 1) * hq]
        rope_sw = jnp.concatenate([-rope[:, half:], rope[:, :half]], axis=1)
        pad = jnp.zeros((r, HEAD_PAD - hq), w_uq.dtype)
        qa += [nope, rope, pad]
        qb += [jnp.zeros((r, MLA_NOPE), w_uq.dtype), rope_sw, pad]
    rk = w_ukv.shape[0]
    hk = MLA_NOPE + MLA_V
    kcols, vcols = [], []
    for h in range(MLA_HEADS):
        kcols += [w_ukv[:, h * hk:h * hk + MLA_NOPE], jnp.zeros((rk, HEAD_PAD - MLA_NOPE), w_ukv.dtype)]
        vcols += [w_ukv[:, h * hk + MLA_NOPE:(h + 1) * hk], jnp.zeros((rk, HEAD_PAD - MLA_V), w_ukv.dtype)]
    return (jnp.concatenate(qa, axis=1).astype(BF16), jnp.concatenate(qb, axis=1).astype(BF16),
            jnp.concatenate(kcols + vcols, axis=1).astype(BF16))


def _mla_proj_kernel(c_ref, misc_ref, cq_ref, sq_ref, ck_ref, sk_ref, qn_ref, kvn_ref,
                     wqa_ref, wqb_ref, wkv_ref, qt_ref, k_ref, vt_ref):
    c = c_ref[...]
    cq = (_rms(c[:, :MLA_Q_RANK]) * qn_ref[...]).astype(BF16)
    ckv = (_rms(c[:, MLA_Q_RANK:]) * kvn_ref[...]).astype(BF16)
    qa = jnp.dot(cq, wqa_ref[...], preferred_element_type=F32)
    qb = jnp.dot(cq, wqb_ref[...], preferred_element_type=F32)
    cos_q = jnp.concatenate([cq_ref[...]] * MLA_HEADS, axis=1)
    sin_q = jnp.concatenate([sq_ref[...]] * MLA_HEADS, axis=1)
    qt_ref[0] = (qa * cos_q + qb * sin_q).T.astype(BF16)
    kv = jnp.dot(ckv, wkv_ref[...], preferred_element_type=F32)
    misc = misc_ref[...]
    kr = misc * ck_ref[...] + pltpu.roll(misc, LANES - (MISC_KR_SW - MISC_KR), 1) * sk_ref[...]
    kw = MLA_HEADS * HEAD_PAD
    k_ref[0] = (kv[:, :kw] + jnp.concatenate([kr] * MLA_HEADS, axis=1)).astype(BF16)
    v = kv[:, kw:]
    v = jnp.where(_lane_iota(v.shape) % HEAD_PAD == MLA_V, 1.0, v)
    vt_ref[0] = v.T.astype(BF16)


def _mla_proj(mla_c, misc, tables, q_norm, kv_norm, wqa, wqb, wkv, bsz, seq):
    t = mla_c.shape[0]
    tm = min(TM_PROJ, seq)
    per_b = seq // tm
    kw = MLA_HEADS * HEAD_PAD
    tab_spec = pl.BlockSpec((tm, HEAD_PAD), lambda i: (i % per_b, 0))
    full = lambda a: pl.BlockSpec(a.shape, lambda i: (0,) * a.ndim)
    qn = q_norm.reshape(1, -1)
    kvn = kv_norm.reshape(1, -1)
    t_spec = pl.BlockSpec((1, kw, tm), lambda i: (i // per_b, 0, i % per_b))
    return pl.pallas_call(
        _mla_proj_kernel,
        grid=(t // tm,),
        in_specs=[
            pl.BlockSpec((tm, mla_c.shape[1]), lambda i: (i, 0)),
            pl.BlockSpec((tm, LANES), lambda i: (i, 0)),
            tab_spec, tab_spec, tab_spec, tab_spec,
            full(qn), full(kvn), full(wqa), full(wqb), full(wkv),
        ],
        out_specs=[t_spec, pl.BlockSpec((1, tm, kw), lambda i: (i // per_b, i % per_b, 0)), t_spec],
        out_shape=[jax.ShapeDtypeStruct((bsz, kw, seq), BF16), jax.ShapeDtypeStruct((bsz, seq, kw), BF16),
                   jax.ShapeDtypeStruct((bsz, kw, seq), BF16)],
        compiler_params=_cparams(("parallel",)),
        name="mla_proj",
    )(mla_c, misc, *tables, qn, kvn, wqa, wqb, wkv)


def _flash_diag_offsets(qb, kb):
    return sorted({((qi * qb) // kb) * kb - qi * qb for qi in range(max(kb // qb, 1))})


def _flash_kernel(qi_tab, ki_tab, kind_tab, qt_ref, k_ref, vt_ref, o_ref, m_ref, acc_ref, *, qb, kb, rq, ck):
    del qi_tab
    p = pl.program_id(1)
    ki = ki_tab[p]
    kind = kind_tab[p]

    @pl.when(ki == 0)
    def _init():
        m_ref[...] = jnp.full(m_ref.shape, -jnp.inf, F32)
        acc_ref[...] = jnp.zeros(acc_ref.shape, F32)

    def step(rel):
        units = []
        for rc in range(qb // rq):
            for kc in range(kb // ck):
                if rel is not None and rel + kc * ck > rc * rq + rq - 1:
                    continue
                for h in range(MLA_HEADS):
                    units.append((h, rc * rq, kc * ck))

        def scores(unit):
            h, c0, k0 = unit
            kk = k_ref[0, k0:k0 + ck, h * HEAD_PAD:(h + 1) * HEAD_PAD]
            qt = qt_ref[0, h * HEAD_PAD:(h + 1) * HEAD_PAD, c0:c0 + rq]
            s = jnp.dot(kk, qt, preferred_element_type=F32)
            if rel is not None and rel + k0 + ck - 1 > c0:
                key = lax.broadcasted_iota(I32, s.shape, 0) + (k0 + rel)
                qry = lax.broadcasted_iota(I32, s.shape, 1) + c0
                s = jnp.where(key <= qry, s, -jnp.inf)
            return s

        def update(unit, s):
            h, c0, k0 = unit
            m_prev = m_ref[h, :, c0:c0 + rq]
            m_new = jnp.maximum(m_prev, jnp.max(s, axis=0, keepdims=True))
            alpha = jnp.exp2(m_prev - m_new)
            pt = jnp.exp2(s - m_new).astype(BF16)
            m_ref[h, :, c0:c0 + rq] = m_new
            acc_ref[h, :, c0:c0 + rq] = acc_ref[h, :, c0:c0 + rq] * alpha + jnp.dot(
                vt_ref[0, h * HEAD_PAD:(h + 1) * HEAD_PAD, k0:k0 + ck], pt, preferred_element_type=F32)

        pending = [scores(u) for u in units[:ATT_PIPELINE]]
        for n, unit in enumerate(units):
            s = pending.pop(0)
            if n + ATT_PIPELINE < len(units):
                pending.append(scores(units[n + ATT_PIPELINE]))
            update(unit, s)

    @pl.when(kind == 0)
    def _off_diagonal():
        step(None)

    for n, rel in enumerate(_flash_diag_offsets(qb, kb)):
        @pl.when(kind == n + 1)
        def _diagonal(rel=rel):
            step(rel)
            for h in range(MLA_HEADS):
                acc = acc_ref[h]
                o_ref[0, h * MLA_V:(h + 1) * MLA_V, :] = (
                    acc[:MLA_V] / acc[MLA_V:MLA_V + 1]).astype(o_ref.dtype)


def _flash_attention(qt, k, vt, bsz, seq):
    qb = min(ATT_Q_BLOCK, seq)
    kb = min(ATT_K_BLOCK, seq)
    rq = min(ATT_Q_COLS, qb)
    ck = min(ATT_K_ROWS, kb)
    offsets = _flash_diag_offsets(qb, kb)
    qi_l, ki_l, kind_l = [], [], []
    for qi in range(seq // qb):
        for kj in range((qi * qb + qb - 1) // kb + 1):
            qi_l.append(qi)
            ki_l.append(kj)
            visible = (kj + 1) * kb - 1 <= qi * qb
            kind_l.append(0 if visible else 1 + offsets.index(kj * kb - qi * qb))
    tabs = [jnp.asarray(x, I32) for x in (qi_l, ki_l, kind_l)]
    kw = MLA_HEADS * HEAD_PAD
    return pl.pallas_call(
        functools.partial(_flash_kernel, qb=qb, kb=kb, rq=rq, ck=ck),
        grid_spec=pltpu.PrefetchScalarGridSpec(
            num_scalar_prefetch=3,
            grid=(bsz, len(qi_l)),
            in_specs=[
                pl.BlockSpec((1, kw, qb), lambda b, p, qt_, kt_, kd_: (b, 0, qt_[p])),
                pl.BlockSpec((1, kb, kw), lambda b, p, qt_, kt_, kd_: (b, kt_[p], 0)),
                pl.BlockSpec((1, kw, kb), lambda b, p, qt_, kt_, kd_: (b, 0, kt_[p])),
            ],
            out_specs=pl.BlockSpec((1, MLA_OUT, qb), lambda b, p, qt_, kt_, kd_: (b, 0, qt_[p])),
            scratch_shapes=[pltpu.VMEM((MLA_HEADS, 1, qb), F32),
                            pltpu.VMEM((MLA_HEADS, HEAD_PAD, qb), F32)],
        ),
        out_shape=jax.ShapeDtypeStruct((bsz, MLA_OUT, seq), BF16),
        compiler_params=_cparams(("parallel", "arbitrary")),
        name="mla_flash",
    )(*tabs, qt, k, vt)


def _causal_conv_silu(cur, ext_ref, halo_ref, cw_ref, cb_ref, first):
    rows = cur.shape[0]

    @pl.when(first)
    def _zero_halo():
        halo_ref[...] = jnp.zeros(halo_ref.shape, F32)

    ext_ref[0:SUBLANES, :] = halo_ref[...]
    ext_ref[SUBLANES:SUBLANES + rows, :] = cur
    halo_ref[...] = cur[rows - SUBLANES:, :]
    acc = cb_ref[...] + cw_ref[CONV_WIDTH - 1:CONV_WIDTH, :] * cur
    for j in range(CONV_WIDTH - 1):
        start = SUBLANES - (CONV_WIDTH - 1) + j
        acc = acc + cw_ref[j:j + 1, :] * ext_ref[start:start + rows, :]
    return _silu(acc)


def _ssd_kernel(z_ref, xbc_ref, misc_ref, cw_ref, cb_ref, dtb_ref, aneg_ref, dsk_ref, nw_ref,
                y_ref, ext_ref, halo_ref, st_ref, *, nchunk, bsz):
    first = pl.program_id(0) == 0
    length = SSD_CHUNK

    @pl.when(first)
    def _zero_state():
        st_ref[...] = jnp.zeros(st_ref.shape, F32)

    tri = (_lane_iota((length, length)) <= _row_iota((length, length))).astype(F32)
    lower = _lane_iota((length, length)) <= _row_iota((length, length))
    lane = _lane_iota((length, LANES))
    lo = lane < HALF
    bw = SSD_GROUPS * SSD_STATE
    npair = SSD_HEADS // 2
    for b in range(bsz):
        _ssd_rows(b, z_ref, xbc_ref, misc_ref, cw_ref, cb_ref, dtb_ref, aneg_ref, dsk_ref, nw_ref, y_ref,
                  ext_ref.at[b], halo_ref.at[b], st_ref, b * npair, first, nchunk, tri, lower, lo, bw)


def _ssd_rows(b, z_ref, xbc_ref, misc_ref, cw_ref, cb_ref, dtb_ref, aneg_ref, dsk_ref, nw_ref, y_ref,
              ext_ref, halo_ref, st_ref, st0, first, nchunk, tri, lower, lo, bw):
    length = SSD_CHUNK
    xbc = _causal_conv_silu(xbc_ref[b].astype(F32), ext_ref, halo_ref, cw_ref, cb_ref, first)
    dt_all = _softplus(misc_ref[b] + dtb_ref[...])
    a_all = dt_all * aneg_ref[...]

    for c in range(nchunk):
        r0 = c * length
        dt = dt_all[r0:r0 + length]
        acum = _dot_f32(tri, a_all[r0:r0 + length])
        acum_t = acum.T
        x_c = xbc[r0:r0 + length, :SSD_D_INNER]
        b_c = xbc[r0:r0 + length, SSD_D_INNER:SSD_D_INNER + bw]
        c_c = xbc[r0:r0 + length, SSD_D_INNER + bw:]
        y_parts = []
        for g in range(SSD_GROUPS):
            b_g = b_c[:, g * SSD_STATE:(g + 1) * SSD_STATE]
            c_g = c_c[:, g * SSD_STATE:(g + 1) * SSD_STATE].astype(BF16)
            b_gt = b_g.T.astype(BF16)
            cb = jnp.dot(c_g, b_gt, preferred_element_type=F32)
            pairs_per_group = SSD_HEADS // SSD_GROUPS // 2
            for pp in range(pairs_per_group):
                pr = g * pairs_per_group + pp
                h0, h1 = 2 * pr, 2 * pr + 1
                col0, col1 = acum[:, h0:h0 + 1], acum[:, h1:h1 + 1]
                l0 = jnp.exp(jnp.where(lower, col0 - acum_t[h0:h0 + 1, :], -jnp.inf))
                l1 = jnp.exp(jnp.where(lower, col1 - acum_t[h1:h1 + 1, :], -jnp.inf))
                mmat = jnp.concatenate([cb * l0, cb * l1], axis=1).astype(BF16)
                xp = x_c[:, pr * LANES:(pr + 1) * LANES]
                xdt = xp * jnp.where(lo, dt[:, h0:h0 + 1], dt[:, h1:h1 + 1])
                rhs = jnp.concatenate([jnp.where(lo, xdt, 0.0), jnp.where(lo, 0.0, xdt)], axis=0)
                y_diag = jnp.dot(mmat, rhs.astype(BF16), preferred_element_type=F32)
                col_pair = jnp.where(lo, col0, col1)
                last_pair = jnp.where(lo[:1], acum[length - 1:length, h0:h0 + 1],
                                      acum[length - 1:length, h1:h1 + 1])
                st = st_ref[st0 + pr]
                y_off = jnp.dot(c_g, st.astype(BF16), preferred_element_type=F32) * jnp.exp(col_pair)
                xdec = (xdt * jnp.exp(last_pair - col_pair)).astype(BF16)
                st_ref[st0 + pr] = st * jnp.exp(last_pair) + jnp.dot(b_gt, xdec, preferred_element_type=F32)
                y_parts.append(y_diag + y_off + xp * dsk_ref[:, pr * LANES:(pr + 1) * LANES])
        y = jnp.concatenate(y_parts, axis=1) * _silu(z_ref[b, r0:r0 + length, :].astype(F32))
        gw = SSD_D_INNER // SSD_GROUPS
        y = jnp.concatenate([_rms(y[:, g * gw:(g + 1) * gw]) for g in range(SSD_GROUPS)], axis=1)
        y_ref[b, r0:r0 + length, :] = (y * nw_ref[...]).astype(y_ref.dtype)


def _ssd_mixer(z, xbc, misc, conv_w, conv_b, dt_bias, a_log, d_skip, norm_w, bsz, seq):
    nchunk = min(SSD_STEP_CHUNKS, seq // SSD_CHUNK)
    rows = nchunk * SSD_CHUNK
    pad = jnp.zeros((LANES - SSD_HEADS,), F32)
    dtb = jnp.concatenate([dt_bias.astype(F32), pad]).reshape(1, LANES)
    aneg = jnp.concatenate([-jnp.exp(a_log.astype(F32)), pad]).reshape(1, LANES)
    dsk = jnp.repeat(d_skip.astype(F32), SSD_HEAD_DIM).reshape(1, SSD_D_INNER)
    full = lambda a: pl.BlockSpec(a.shape, lambda i: (0,) * a.ndim)
    cb2 = conv_b.reshape(1, -1)
    nw2 = norm_w.reshape(1, -1)
    blk = lambda w: pl.BlockSpec((bsz, rows, w), lambda i: (0, i, 0))
    out = pl.pallas_call(
        functools.partial(_ssd_kernel, nchunk=nchunk, bsz=bsz),
        grid=(seq // rows,),
        in_specs=[blk(SSD_D_INNER), blk(SSD_CONV_DIM), blk(LANES),
                  full(conv_w), full(cb2), full(dtb), full(aneg), full(dsk), full(nw2)],
        out_specs=blk(SSD_D_INNER),
        out_shape=jax.ShapeDtypeStruct((bsz, seq, SSD_D_INNER), BF16),
        scratch_shapes=[pltpu.VMEM((bsz, rows + SUBLANES, SSD_CONV_DIM), F32),
                        pltpu.VMEM((bsz, SUBLANES, SSD_CONV_DIM), F32),
                        pltpu.VMEM((bsz * (SSD_HEADS // 2), SSD_STATE, LANES), F32)],
        compiler_params=_cparams(("arbitrary",)),
        name="ssd_mixer",
    )(z.reshape(bsz, seq, -1), xbc.reshape(bsz, seq, -1), misc.reshape(bsz, seq, -1),
      conv_w, cb2, dtb, aneg, dsk, nw2)
    return out.reshape(bsz * seq, SSD_D_INNER)


def _half_sum(x, lo):
    s_lo = jnp.sum(jnp.where(lo, x, 0.0), axis=1, keepdims=True)
    s_hi = jnp.sum(jnp.where(lo, 0.0, x), axis=1, keepdims=True)
    return jnp.where(lo, s_lo, s_hi)


def _gdn_kernel(qkv_ref, z_ref, misc_ref, cw_ref, cb_ref, dtb_ref, aneg_ref, nw_ref,
                o_ref, ext_ref, halo_ref, st_ref, *, nchunk, bsz):
    first = pl.program_id(0) == 0
    length = GDN_CHUNK
    two = 2 * length

    @pl.when(first)
    def _zero_state():
        st_ref[...] = jnp.zeros(st_ref.shape, F32)

    tri = (_lane_iota((length, length)) <= _row_iota((length, length))).astype(F32)
    lane = _lane_iota((length, LANES))
    lo = lane < HALF
    r2 = _row_iota((two, two))
    c2 = _lane_iota((two, two))
    same_blk = (r2 < length) == (c2 < length)
    low_incl = same_blk & (c2 <= r2)
    low_strict = same_blk & (c2 < r2)
    eye = (r2 == c2).astype(F32)
    top_lo = (r2 < length) == (c2 < HALF)
    hk = GDN_HEADS * GDN_DK
    qscale = GDN_DK ** -0.5
    npair = GDN_HEADS // 2

    def blockdiag(slab):
        return jnp.where(top_lo, jnp.concatenate([slab, slab], axis=0), 0.0)

    def fold(bd):
        return bd[:length] + bd[length:]

    chains = []
    for b in range(bsz):
        qkv = _causal_conv_silu(qkv_ref[b].astype(F32), ext_ref.at[b], halo_ref.at[b], cw_ref, cb_ref, first)
        misc = misc_ref[b]
        beta_all = jax.nn.sigmoid(misc)
        g_all = aneg_ref[...] * _softplus(misc + dtb_ref[...])
        for c in range(nchunk):
            r0 = c * length
            gcum = _dot_f32(tri, g_all[r0:r0 + length])
            beta = beta_all[r0:r0 + length]
            for pr in range(npair):
                h0, h1 = 2 * pr, 2 * pr + 1
                q = qkv[r0:r0 + length, pr * LANES:(pr + 1) * LANES]
                k = qkv[r0:r0 + length, hk + pr * LANES:hk + (pr + 1) * LANES]
                v = qkv[r0:r0 + length, 2 * hk + pr * LANES:2 * hk + (pr + 1) * LANES]
                qn = q * lax.rsqrt(_half_sum(q * q, lo) + EPS) * qscale
                kn = k * lax.rsqrt(_half_sum(k * k, lo) + EPS)
                g0 = gcum[:, MISC_A + h0:MISC_A + h0 + 1]
                g1 = gcum[:, MISC_A + h1:MISC_A + h1 + 1]
                gexp = jnp.where(lo, g0, g1)
                bexp = jnp.where(lo, beta[:, MISC_B + h0:MISC_B + h0 + 1],
                                 beta[:, MISC_B + h1:MISC_B + h1 + 1])
                gcol = jnp.concatenate([jnp.broadcast_to(g0, (length, two)),
                                        jnp.broadcast_to(g1, (length, two))], axis=0)
                decay = jnp.exp(jnp.where(low_incl, gcol - gcol.T, -jnp.inf))
                kb = kn * bexp
                kk2 = jnp.concatenate([kn, kn], axis=0)
                kq = _dot_nt(jnp.concatenate([blockdiag(kb), blockdiag(qn)], axis=0), kk2)
                a_mat = jnp.where(low_strict, kq[:two] * decay, 0.0)
                qk = jnp.where(low_incl, kq[two:] * decay, 0.0)
                g_last = gexp[length - 1:length, :]
                chains.append(dict(
                    b=b, c=c, pr=pr, pw=a_mat, t=eye - a_mat, qk=qk,
                    vb=v * bexp, kbg=kb * jnp.exp(gexp), q_dec=qn * jnp.exp(gexp),
                    k_tail=kn * jnp.exp(g_last - gexp), gl=jnp.exp(g_last)))

    for _ in range(int(math.log2(length)) - 1):
        for ch in chains:
            ch["pw"] = _dot(ch["pw"], ch["pw"])
        for ch in chains:
            ch["t"] = ch["t"] + _dot(ch["t"], ch["pw"])

    for ch in chains:
        ku = _dot(ch["t"], jnp.concatenate([blockdiag(ch["kbg"]), blockdiag(ch["vb"])], axis=1))
        ch["k_cum"] = fold(ku[:, :LANES])
        ch["u"] = fold(ku[:, LANES:])
    for ch in chains:
        rhs = jnp.concatenate([ch["k_cum"], ch["u"]], axis=1)
        mn = _dot_tn(ch["k_tail"], rhs)
        ch["m"] = jnp.where(top_lo, mn[:, :LANES], 0.0)
        ch["n"] = jnp.where(top_lo, mn[:, LANES:], 0.0)
        qo = _dot(ch["qk"], jnp.concatenate([blockdiag(ch["k_cum"]), blockdiag(ch["u"])], axis=1))
        ch["q_eff"] = ch["q_dec"] - fold(qo[:, :LANES])
        ch["o_loc"] = fold(qo[:, LANES:])

    states = {(b, pr): st_ref[b * npair + pr] for b in range(bsz) for pr in range(npair)}
    for c in range(nchunk):
        for ch in chains:
            if ch["c"] != c:
                continue
            b, pr = ch["b"], ch["pr"]
            st = states[(b, pr)]
            both = _dot(jnp.concatenate([ch["q_eff"], ch["m"]], axis=0), st)
            o = both[:length] + ch["o_loc"]
            states[(b, pr)] = st * ch["gl"] - both[length:] + ch["n"]
            r0 = c * length
            ms = _half_sum(o * o, lo) * (1.0 / GDN_DV)
            zz = z_ref[b, r0:r0 + length, pr * LANES:(pr + 1) * LANES].astype(F32)
            out = o * lax.rsqrt(ms + EPS) * nw_ref[...] * _silu(zz)
            o_ref[b, r0:r0 + length, pr * LANES:(pr + 1) * LANES] = out.astype(o_ref.dtype)
    for (b, pr), st in states.items():
        st_ref[b * npair + pr] = st


def _gdn_mixer(qkv, z, misc, conv_w, conv_b, dt_bias, a_log, norm_w, bsz, seq):
    nchunk = min(GDN_STEP_CHUNKS, seq // GDN_CHUNK)
    rows = nchunk * GDN_CHUNK
    dtb = jnp.zeros((1, LANES), F32).at[0, MISC_A:MISC_A + GDN_HEADS].set(dt_bias.astype(F32))
    aneg = jnp.zeros((1, LANES), F32).at[0, MISC_A:MISC_A + GDN_HEADS].set(-jnp.exp(a_log.astype(F32)))
    nw2 = jnp.concatenate([norm_w.astype(F32)] * 2).reshape(1, LANES)
    cb2 = conv_b.reshape(1, -1)
    full = lambda a: pl.BlockSpec(a.shape, lambda i: (0,) * a.ndim)
    blk = lambda w: pl.BlockSpec((bsz, rows, w), lambda i: (0, i, 0))
    out = pl.pallas_call(
        functools.partial(_gdn_kernel, nchunk=nchunk, bsz=bsz),
        grid=(seq // rows,),
        in_specs=[blk(GDN_QKV), blk(GDN_OUT), blk(LANES),
                  full(conv_w), full(cb2), full(dtb), full(aneg), full(nw2)],
        out_specs=blk(GDN_OUT),
        out_shape=jax.ShapeDtypeStruct((bsz, seq, GDN_OUT), BF16),
        scratch_shapes=[pltpu.VMEM((bsz, rows + SUBLANES, GDN_QKV), F32),
                        pltpu.VMEM((bsz, SUBLANES, GDN_QKV), F32),
                        pltpu.VMEM((bsz * (GDN_HEADS // 2), 2 * GDN_DK, LANES), F32)],
        compiler_params=_cparams(("arbitrary",)),
        name="gdn_mixer",
    )(qkv.reshape(bsz, seq, -1), z.reshape(bsz, seq, -1), misc.reshape(bsz, seq, -1),
      conv_w, cb2, dtb, aneg, nw2)
    return out.reshape(bsz * seq, GDN_OUT)


ROUTE_E0, ROUTE_E1, ROUTE_R0, ROUTE_R1, ROUTE_W0, ROUTE_W1 = range(6)


def _post_kernel(*refs, route):
    if route:
        (x_ref, a_ref, s_ref, g_ref, wa_ref, ws_ref, wg_ref, gate_ref, sh_ref, sc_ref, wrh_ref, wrl_ref,
         x1_ref, h_ref, route_ref, cnt_ref, carry_ref) = refs
    else:
        (x_ref, a_ref, s_ref, g_ref, wa_ref, ws_ref, wg_ref, gate_ref, sh_ref, sc_ref,
         x1_ref, h_ref) = refs
    mix = (lax.dot_general(a_ref[0], wa_ref[...], (((0,), (0,)), ((), ())), preferred_element_type=F32)
           + jnp.dot(s_ref[...], ws_ref[...], preferred_element_type=F32)
           + jnp.dot(g_ref[...], wg_ref[...], preferred_element_type=F32))
    x1 = x_ref[...] + gate_ref[0] * mix
    x1_ref[...] = x1
    h = _rms(x1) * (1.0 + sc_ref[0]) + sh_ref[0]
    h_ref[...] = h.astype(h_ref.dtype)
    if not route:
        return

    @pl.when(pl.program_id(0) == 0)
    def _zero_carry():
        carry_ref[...] = jnp.zeros(carry_ref.shape, F32)

    tm = h.shape[0]
    lane = _lane_iota((tm, LANES))
    h_hi = h.astype(BF16)
    h_lo = (h - h_hi.astype(F32)).astype(BF16)
    hh = jnp.dot(h_hi, jnp.concatenate([wrh_ref[...], wrl_ref[...]], axis=1), preferred_element_type=F32)
    logits = hh[:, :LANES] + (jnp.dot(h_lo, wrh_ref[...], preferred_element_type=F32) + hh[:, LANES:])
    logits = jnp.where(lane < N_EXPERTS, logits, -jnp.inf)
    lane_f = lane.astype(F32)
    m0 = jnp.max(logits, axis=1, keepdims=True)
    e0 = jnp.min(jnp.where(logits == m0, lane_f, float(LANES)), axis=1, keepdims=True)
    rest = jnp.where(lane_f == e0, -jnp.inf, logits)
    m1 = jnp.max(rest, axis=1, keepdims=True)
    e1 = jnp.min(jnp.where(rest == m1, lane_f, float(LANES)), axis=1, keepdims=True)
    ex = jnp.exp(m1 - m0)
    w0 = 1.0 / (1.0 + ex)
    w1 = ex / (1.0 + ex)
    oh0 = (lane_f == e0).astype(F32)
    oh1 = (lane_f == e1).astype(F32)
    both = oh0 + oh1
    strict = (_lane_iota((tm, tm)) < _row_iota((tm, tm))).astype(BF16)
    before = jnp.dot(strict, both.astype(BF16), preferred_element_type=F32) + carry_ref[0:1, :]
    r0 = jnp.sum(before * oh0, axis=1, keepdims=True)
    r1 = jnp.sum(before * oh1, axis=1, keepdims=True)
    carry_ref[0:1, :] = carry_ref[0:1, :] + jnp.sum(both, axis=0, keepdims=True)
    slab = jnp.zeros((tm, LANES), F32)
    for pos, val in ((ROUTE_E0, e0), (ROUTE_E1, e1), (ROUTE_R0, r0), (ROUTE_R1, r1),
                     (ROUTE_W0, w0), (ROUTE_W1, w1)):
        slab = jnp.where(lane == pos, val, slab)
    route_ref[...] = slab
    cnt_ref[...] = carry_ref[...]


def _post_mixer(x2d, mla_o, ssd_y, gdn_o, w_out, gate, shift, scale, seq, w_router=None):
    t, d = x2d.shape
    route = w_router is not None
    tm = min(ROUTE_TILE if route else TM_PROJ, seq)
    per_b = seq // tm
    wa = w_out[:MLA_OUT].astype(BF16)
    ws = w_out[MLA_OUT:MLA_OUT + SSD_D_INNER].astype(BF16)
    wg = w_out[MLA_OUT + SSD_D_INNER:].astype(BF16)
    row = lambda w: pl.BlockSpec((tm, w), lambda i: (i, 0))
    full = lambda a: pl.BlockSpec(a.shape, lambda i: (0,) * a.ndim)
    mod = pl.BlockSpec((1, 1, d), lambda i: (i // per_b, 0, 0))
    att = pl.BlockSpec((1, MLA_OUT, tm), lambda i: (i // per_b, 0, i % per_b))
    in_specs = [row(d), att, row(SSD_D_INNER), row(GDN_OUT), full(wa), full(ws), full(wg),
                mod, mod, mod]
    args = [x2d, mla_o, ssd_y, gdn_o, wa, ws, wg, gate, shift, scale]
    out_specs = [row(d), row(d)]
    out_shape = [jax.ShapeDtypeStruct((t, d), F32), jax.ShapeDtypeStruct((t, d), F32 if route else BF16)]
    scratch = []
    if route:
        wr = jnp.zeros((d, LANES), F32).at[:, :N_EXPERTS].set(w_router.astype(F32))
        wr_hi = wr.astype(BF16)
        wr_lo = (wr - wr_hi.astype(F32)).astype(BF16)
        in_specs += [full(wr_hi), full(wr_lo)]
        args += [wr_hi, wr_lo]
        out_specs += [row(LANES), pl.BlockSpec((SUBLANES, LANES), lambda i: (0, 0))]
        out_shape += [jax.ShapeDtypeStruct((t, LANES), F32), jax.ShapeDtypeStruct((SUBLANES, LANES), F32)]
        scratch = [pltpu.VMEM((SUBLANES, LANES), F32)]
    return pl.pallas_call(
        functools.partial(_post_kernel, route=route),
        grid=(t // tm,),
        in_specs=in_specs, out_specs=out_specs, out_shape=out_shape, scratch_shapes=scratch,
        compiler_params=_cparams(("arbitrary",) if route else ("parallel",)),
        name="post_mixer_route" if route else "post_mixer",
    )(*args)


def _finish(x, final, fn_ref):
    return _rms(x) * fn_ref[...] if final else x


def _ffn_kernel(h_ref, x1_ref, gate_ref, fn_ref, wg_ref, wu_ref, wd_ref, o_ref, *, final):
    f = pl.program_id(1)
    h = h_ref[...]
    act = _silu(jnp.dot(h, wg_ref[...], preferred_element_type=F32)) * jnp.dot(
        h, wu_ref[...], preferred_element_type=F32)
    part = jnp.dot(act.astype(BF16), wd_ref[...], preferred_element_type=F32)

    @pl.when(f == 0)
    def _set():
        o_ref[...] = part

    @pl.when(f > 0)
    def _add():
        o_ref[...] += part

    @pl.when(f == pl.num_programs(1) - 1)
    def _residual():
        o_ref[...] = _finish(x1_ref[...] + gate_ref[0] * o_ref[...], final, fn_ref)


def _ffn_tile(dff):
    for cand in (2816, 1408, 1024, 512, 256, 128):
        if dff % cand == 0:
            return cand
    raise ValueError(f"unsupported d_ff {dff}")


def _dense_ffn(h, x1, gate, w_gate, w_up, w_down, final_norm, final, seq):
    t, d = x1.shape
    dff = w_gate.shape[1]
    tm = min(TM_FFN, seq)
    tf = _ffn_tile(dff)
    per_b = seq // tm
    fn = final_norm.reshape(1, d).astype(F32)
    return pl.pallas_call(
        functools.partial(_ffn_kernel, final=final),
        grid=(t // tm, dff // tf),
        in_specs=[
            pl.BlockSpec((tm, d), lambda i, f: (i, 0)),
            pl.BlockSpec((tm, d), lambda i, f: (i, 0)),
            pl.BlockSpec((1, 1, d), lambda i, f: (i // per_b, 0, 0)),
            pl.BlockSpec((1, d), lambda i, f: (0, 0)),
            pl.BlockSpec((d, tf), lambda i, f: (0, f)),
            pl.BlockSpec((d, tf), lambda i, f: (0, f)),
            pl.BlockSpec((tf, d), lambda i, f: (f, 0)),
        ],
        out_specs=pl.BlockSpec((tm, d), lambda i, f: (i, 0)),
        out_shape=jax.ShapeDtypeStruct((t, d), F32),
        compiler_params=_cparams(("parallel", "arbitrary")),
        name="dense_ffn",
    )(h, x1, gate, fn, w_gate.astype(BF16), w_up.astype(BF16), w_down.astype(BF16))


def _row_copy(src, dst, sem):
    return pltpu.make_async_copy(src, dst, sem)


def _dispatch_kernel(fill_ref, dest_ref, h_ref, xs_hbm, zbuf_ref, sem, zsem, *, tm, bm):
    @pl.when(pl.program_id(0) == 0)
    def _zero_padding():
        zbuf_ref[...] = jnp.zeros(zbuf_ref.shape, F32)

        def row_fill(r):
            return _row_copy(zbuf_ref.at[pl.ds(0, 1), :], xs_hbm.at[pl.ds(r, 1), :], zsem)

        def block_fill(blk):
            return _row_copy(zbuf_ref, xs_hbm.at[pl.ds(pl.multiple_of(blk * bm, bm), bm), :], zsem)

        spans = [(fill_ref[2 * e], fill_ref[2 * e + 1], row_fill) for e in range(N_EXPERTS)]
        spans.append((fill_ref[2 * N_EXPERTS], fill_ref[2 * N_EXPERTS + 1], block_fill))
        for lo, hi, fill in spans:
            lax.fori_loop(lo, hi, lambda r, c, fill=fill: (fill(r).start(), c)[1], 0)
        for lo, hi, fill in spans:
            lax.fori_loop(lo, hi, lambda r, c, fill=fill: (fill(r).wait(), c)[1], 0)

    def issue(g, carry):
        for j in range(ROW_DMA_UNROLL):
            t = g * ROW_DMA_UNROLL + j
            for kk in range(TOP_K):
                d = dest_ref[0, 0, TOP_K * t + kk]
                _row_copy(h_ref.at[pl.ds(t, 1), :], xs_hbm.at[pl.ds(d, 1), :], sem).start(priority=kk)
        return carry

    lax.fori_loop(0, tm // ROW_DMA_UNROLL, issue, 0)

    def drain(g, carry):
        for _ in range(ROW_DMA_UNROLL * TOP_K):
            _row_copy(h_ref.at[pl.ds(0, 1), :], xs_hbm.at[pl.ds(0, 1), :], sem).wait()
        return carry

    lax.fori_loop(0, tm // ROW_DMA_UNROLL, drain, 0)


def _moe_dispatch(h_f32, dest, fill, n_rows):
    t, d = h_f32.shape
    tm = min(MOE_TOK_TILE, t)
    bm = MOE_ROWS
    dest3 = dest.reshape(t // tm, 1, TOP_K * tm)
    return pl.pallas_call(
        functools.partial(_dispatch_kernel, tm=tm, bm=bm),
        grid=(t // tm,),
        in_specs=[
            pl.BlockSpec(memory_space=pltpu.SMEM),
            pl.BlockSpec((1, 1, TOP_K * tm), lambda i: (i, 0, 0), memory_space=pltpu.SMEM),
            pl.BlockSpec((tm, d), lambda i: (i, 0)),
        ],
        out_specs=pl.BlockSpec(memory_space=pl.ANY),
        out_shape=jax.ShapeDtypeStruct((n_rows, d), F32),
        scratch_shapes=[pltpu.VMEM((bm, d), F32), pltpu.SemaphoreType.DMA(()), pltpu.SemaphoreType.DMA(())],
        compiler_params=_cparams(("arbitrary",)),
        name="moe_dispatch",
    )(fill, dest3, h_f32)


def _experts_kernel(be_ref, nused_ref, x_ref, wg_ref, wu_ref, wd_ref, y_ref):
    i = pl.program_id(0)
    f = pl.program_id(1)

    used = i < nused_ref[0]

    @pl.when(jnp.logical_not(used) & (f == 0))
    def _unused_block():
        y_ref[...] = jnp.zeros(y_ref.shape, F32)

    def partial_out():
        xb = x_ref[...].astype(BF16)
        act = _silu(jnp.dot(xb, wg_ref[0], preferred_element_type=F32)) * jnp.dot(
            xb, wu_ref[0], preferred_element_type=F32)
        return jnp.dot(act.astype(BF16), wd_ref[0], preferred_element_type=F32)

    @pl.when(used & (f == 0))
    def _first_ff_tile():
        y_ref[...] = partial_out()

    @pl.when(used & (f > 0))
    def _next_ff_tile():
        y_ref[...] += partial_out()


def _moe_experts(xs, block_expert, n_used, w_gate, w_up, w_down):
    n_rows, d = xs.shape
    dff = w_gate.shape[2]
    bm = MOE_ROWS
    tf = MOE_FF_TILE if dff % MOE_FF_TILE == 0 else dff

    def ff(i, f, nu):
        return jnp.where(i < nu[0], f, 0)

    return pl.pallas_call(
        _experts_kernel,
        grid_spec=pltpu.PrefetchScalarGridSpec(
            num_scalar_prefetch=2,
            grid=(n_rows // bm, dff // tf),
            in_specs=[
                pl.BlockSpec((bm, d), lambda i, f, be, nu: (i, 0)),
                pl.BlockSpec((1, d, tf), lambda i, f, be, nu: (be[i], 0, ff(i, f, nu))),
                pl.BlockSpec((1, d, tf), lambda i, f, be, nu: (be[i], 0, ff(i, f, nu))),
                pl.BlockSpec((1, tf, d), lambda i, f, be, nu: (be[i], ff(i, f, nu), 0)),
            ],
            out_specs=pl.BlockSpec((bm, d), lambda i, f, be, nu: (i, 0)),
        ),
        out_shape=jax.ShapeDtypeStruct((n_rows, d), F32),
        compiler_params=_cparams(("parallel", "arbitrary")),
        name="moe_experts",
    )(block_expert, n_used, xs, w_gate.astype(BF16), w_up.astype(BF16), w_down.astype(BF16))


def _combine_kernel(dest_ref, dnext_ref, x1_ref, route_ref, gate_ref, fn_ref, ys_hbm, o_ref, buf_ref, sem,
                    *, tm, final):
    i = pl.program_id(0)
    slot = lax.rem(i, 2)

    def gather(d_ref, s):
        def issue(g, carry):
            for j in range(ROW_DMA_UNROLL):
                t = g * ROW_DMA_UNROLL + j
                for kk in range(TOP_K):
                    d = d_ref[0, 0, TOP_K * t + kk]
                    _row_copy(ys_hbm.at[pl.ds(d, 1), :], buf_ref.at[s, kk, pl.ds(t, 1), :],
                              sem.at[s]).start(priority=kk)
            return carry

        lax.fori_loop(0, tm // ROW_DMA_UNROLL, issue, 0)

    @pl.when(i == 0)
    def _first_tile():
        gather(dest_ref, slot)

    @pl.when(i + 1 < pl.num_programs(0))
    def _next_tile():
        gather(dnext_ref, 1 - slot)

    def drain(g, carry):
        for _ in range(ROW_DMA_UNROLL * TOP_K):
            _row_copy(ys_hbm.at[pl.ds(0, 1), :], buf_ref.at[slot, 0, pl.ds(0, 1), :], sem.at[slot]).wait()
        return carry

    lax.fori_loop(0, tm // ROW_DMA_UNROLL, drain, 0)
    route = route_ref[...]
    w0 = route[:, ROUTE_W0:ROUTE_W0 + 1]
    w1 = route[:, ROUTE_W1:ROUTE_W1 + 1]
    f = w0 * buf_ref[slot, 0] + w1 * buf_ref[slot, 1]
    o_ref[...] = _finish(x1_ref[...] + gate_ref[0] * f, final, fn_ref)


def _moe_combine(ys, dest, x1, route, gate, final_norm, final, seq):
    t, d = x1.shape
    tm = min(MOE_TOK_TILE, seq)
    per_b = seq // tm
    dest3 = dest.reshape(t // tm, 1, TOP_K * tm)
    fn = final_norm.reshape(1, d).astype(F32)
    nt = t // tm
    return pl.pallas_call(
        functools.partial(_combine_kernel, tm=tm, final=final),
        grid=(nt,),
        in_specs=[
            pl.BlockSpec((1, 1, TOP_K * tm), lambda i: (i, 0, 0), memory_space=pltpu.SMEM),
            pl.BlockSpec((1, 1, TOP_K * tm), lambda i: (jnp.minimum(i + 1, nt - 1), 0, 0),
                         memory_space=pltpu.SMEM),
            pl.BlockSpec((tm, d), lambda i: (i, 0)),
            pl.BlockSpec((tm, LANES), lambda i: (i, 0)),
            pl.BlockSpec((1, 1, d), lambda i: (i // per_b, 0, 0)),
            pl.BlockSpec((1, d), lambda i: (0, 0)),
            pl.BlockSpec(memory_space=pl.ANY),
        ],
        out_specs=pl.BlockSpec((tm, d), lambda i: (i, 0)),
        out_shape=jax.ShapeDtypeStruct((t, d), F32),
        scratch_shapes=[pltpu.VMEM((2, TOP_K, tm, d), F32), pltpu.SemaphoreType.DMA((2,))],
        compiler_params=_cparams(("arbitrary",)),
        name="moe_combine",
    )(dest3, dest3, x1, route, gate, fn, ys)


def _moe_ffn(h_f32, x1, route, counts_slab, gate, w_gate, w_up, w_down, final_norm, final, seq):
    t, d = x1.shape
    bm = MOE_ROWS
    n_blocks = (t * TOP_K + bm - 1) // bm + N_EXPERTS
    n_rows = n_blocks * bm
    counts = counts_slab[0, :N_EXPERTS].astype(I32)
    padded = ((counts + bm - 1) // bm) * bm
    pend = jnp.cumsum(padded)
    pstart = pend - padded
    e = route[:, ROUTE_E0:ROUTE_E1 + 1].astype(I32)
    rank = route[:, ROUTE_R0:ROUTE_R1 + 1].astype(I32)
    dest = (jnp.sum(jnp.where(e[..., None] == jnp.arange(N_EXPERTS), pstart, 0), axis=-1) + rank)
    blk_start = jnp.arange(n_blocks, dtype=I32) * bm
    block_expert = jnp.minimum(jnp.sum(blk_start[:, None] >= pend[None, :], axis=1), N_EXPERTS - 1).astype(I32)
    n_used = (pend[-1:] // bm).astype(I32)
    fill = jnp.concatenate([jnp.stack([pstart + counts, pend], axis=1).reshape(-1),
                            n_used, jnp.full((1,), n_blocks, I32)]).astype(I32)
    xs = _moe_dispatch(h_f32, dest, fill, n_rows)
    ys = _moe_experts(xs, block_expert, n_used, w_gate, w_up, w_down)
    return _moe_combine(ys, dest, x1, route, gate, final_norm, final, seq)


def kernel(x, c, ada_w, ada_b, w_in, mla_q_norm, mla_w_uq, mla_kv_norm, mla_w_ukv, ssd_conv_w, ssd_conv_b, ssd_dt_bias, ssd_a_log, ssd_d, ssd_norm, gdn_conv_w, gdn_conv_b, gdn_dt_bias, gdn_a_log, gdn_norm, w_out, ffn_w_gate, ffn_w_up, ffn_w_down, moe_router, moe_w_gate, moe_w_up, moe_w_down, final_norm):
    bsz, seq, d = x.shape
    depth = w_in.shape[0]
    t = bsz * seq
    mod = _ada_modulation(c, ada_w, ada_b)
    tables = _rope_tables(seq)
    xf = x.reshape(t, d)
    for l in range(depth):
        sh1, sc1, g1, sh2, sc2, g2 = [m.reshape(bsz, 1, d) for m in jnp.split(mod[l], 6, axis=-1)]
        mla_c, misc, ssd_z, ssd_xbc, gdn_qkv, gdn_z = _in_proj(xf, sh1, sc1, _prep_w_in(w_in[l]), seq)
        wqa, wqb, wkv = _prep_mla_weights(mla_w_uq[l], mla_w_ukv[l])
        qt, k, vt = _mla_proj(mla_c, misc, tables, mla_q_norm[l], mla_kv_norm[l], wqa, wqb, wkv, bsz, seq)
        mla_o = _flash_attention(qt, k, vt, bsz, seq)
        ssd_y = _ssd_mixer(ssd_z, ssd_xbc, misc, ssd_conv_w[l], ssd_conv_b[l], ssd_dt_bias[l],
                           ssd_a_log[l], ssd_d[l], ssd_norm[l], bsz, seq)
        gdn_o = _gdn_mixer(gdn_qkv, gdn_z, misc, gdn_conv_w[l], gdn_conv_b[l], gdn_dt_bias[l],
                           gdn_a_log[l], gdn_norm[l], bsz, seq)
        final = l == depth - 1
        if l % 2 == 0:
            x1, h2 = _post_mixer(xf, mla_o, ssd_y, gdn_o, w_out[l], g1, sh2, sc2, seq)
            xf = _dense_ffn(h2, x1, g2, ffn_w_gate[l // 2], ffn_w_up[l // 2], ffn_w_down[l // 2],
                            final_norm, final, seq)
        else:
            x1, h2, route, counts = _post_mixer(xf, mla_o, ssd_y, gdn_o, w_out[l], g1, sh2, sc2, seq,
                                                w_router=moe_router[l // 2])
            xf = _moe_ffn(h2, x1, route, counts, g2, moe_w_gate[l // 2], moe_w_up[l // 2],
                          moe_w_down[l // 2], final_norm, final, seq)
    return xf.reshape(bsz, seq, d)
```
